```python
import jax, jax.numpy as jnp
from jax import lax
import numpy as np

D_MODEL = 2048
BATCH = 8
SEQ = 8192
DEPTH = 1

MEM_LEN = 256
D_CONV = 2048
CONV_WIDTH = 31
ML_HEADS = 4
D_ML = 2048
ML_HEAD_DIM = D_ML // ML_HEADS
ML_CHUNK = 64
QK_CONV_WIDTH = 4
XA_HEADS = 4
D_XA = 2048
XA_HEAD_DIM = D_XA // XA_HEADS
N_BRANCH = 3
EPS = 1e-6

IN_GROUPS = (
    ("glu_a", D_CONV), ("glu_b", D_CONV), ("z_conv", D_CONV),
    ("qk_ml", 2 * D_ML), ("v_ml", D_ML), ("o_ml", D_ML), ("z_ml", D_ML),
    ("if_ml", 2 * ML_HEADS),
    ("q_xa", D_XA), ("z_xa", D_XA),
    ("gates", N_BRANCH * D_MODEL),
)
N_IN = 3 * D_CONV + 5 * D_ML + 2 * ML_HEADS + 2 * D_XA + N_BRANCH * D_MODEL

kernel_name = "hybrid_conformer_mlstm_memxattn_gated"


def rms_norm(x, g):
    xf = x.astype(jnp.float32)
    y = xf * lax.rsqrt(jnp.mean(xf * xf, axis=-1, keepdims=True) + EPS)
    return (y * g.astype(jnp.float32)).astype(x.dtype)


def layer_norm(x, g, b):
    xf = x.astype(jnp.float32)
    mu = jnp.mean(xf, axis=-1, keepdims=True)
    var = jnp.mean(jnp.square(xf - mu), axis=-1, keepdims=True)
    y = (xf - mu) * lax.rsqrt(var + EPS)
    return (y * g.astype(jnp.float32) + b.astype(jnp.float32)).astype(x.dtype)


def causal_depthwise_conv(x, w):
    width, ch = w.shape
    return lax.conv_general_dilated(
        x, w[:, None, :].astype(x.dtype), window_strides=(1,),
        padding=((width - 1, 0),), dimension_numbers=("NWC", "WIO", "NWC"),
        feature_group_count=ch)


def in_cols(h, w_in, name):
    start = 0
    for n, width in IN_GROUPS:
        if n == name:
            return h @ w_in[:, start:start + width]
        start += width
    raise ValueError(name)


def mlstm_chunkwise(q, k, v, log_i, log_f):
    B, H, S, Dh = q.shape
    L = ML_CHUNK
    nc = S // L

    def to_chunks(a):
        return jnp.moveaxis(a.reshape(B, H, nc, L, *a.shape[3:]), 2, 0)

    xs = (to_chunks(q), to_chunks(k), to_chunks(v), to_chunks(log_i), to_chunks(log_f))
    causal = jnp.tril(jnp.ones((L, L), dtype=bool))

    def step(carry, chunk):
        C, n, m = carry
        qj, kj, vj, li, lf = chunk
        b = jnp.cumsum(lf, axis=-1)
        d = b[..., :, None] - b[..., None, :] + li[..., None, :]
        d = jnp.where(causal, d, -jnp.inf)
        inter = b + m[..., None]
        m_row = jnp.maximum(inter, jnp.max(d, axis=-1))
        w_intra = jnp.exp(d - m_row[..., None])
        w_inter = jnp.exp(inter - m_row)
        s = jnp.einsum("bhid,bhjd->bhij", qj, kj) * w_intra
        num = (jnp.einsum("bhij,bhje->bhie", s, vj)
               + w_inter[..., None] * jnp.einsum("bhid,bhde->bhie", qj, C))
        den = jnp.sum(s, axis=-1) + w_inter * jnp.einsum("bhid,bhd->bhi", qj, n)
        h = num / jnp.maximum(jnp.abs(den), jnp.exp(-m_row))[..., None]
        b_last = b[..., -1]
        g = b_last[..., None] - b + li
        m_new = jnp.maximum(b_last + m, jnp.max(g, axis=-1))
        decay = jnp.exp(b_last + m - m_new)
        wk = jnp.exp(g - m_new[..., None])
        C = decay[..., None, None] * C + jnp.einsum("bhj,bhjd,bhje->bhde", wk, kj, vj)
        n = decay[..., None] * n + jnp.einsum("bhj,bhjd->bhd", wk, kj)
        return (C, n, m_new), h

    init = (jnp.zeros((B, H, Dh, Dh), jnp.float32),
            jnp.zeros((B, H, Dh), jnp.float32),
            jnp.zeros((B, H), jnp.float32))
    _, hs = lax.scan(step, init, xs)
    return jnp.moveaxis(hs, 0, 2).reshape(B, H, S, Dh)


def _fwd_setup_inputs(seed: int = 0) -> dict:
    key = jax.random.key(seed)
    ks = jax.random.split(key, 24)
    f32 = jnp.float32

    def nrm(k, shape, scale):
        return jax.random.normal(k, shape, f32) * scale

    def gain(k, n):
        return 1.0 + 0.02 * jax.random.normal(k, (n,), f32)

    b_i = 0.1 * jax.random.normal(ks[3], (ML_HEADS,), f32)
    b_f = jnp.linspace(3.0, 6.0, ML_HEADS, dtype=f32) + 0.1 * jax.random.normal(ks[4], (ML_HEADS,), f32)
    return {
        "x": nrm(ks[0], (BATCH, SEQ, D_MODEL), 1.0),
        "mem": nrm(ks[1], (BATCH, MEM_LEN, D_MODEL), 1.0),
        "g_pre": gain(ks[2], D_MODEL),
        "w_in": nrm(ks[5], (D_MODEL, N_IN), D_MODEL ** -0.5),
        "b_if": jnp.concatenate([b_i, b_f]),
        "w_qk_conv": nrm(ks[6], (QK_CONV_WIDTH, 2 * D_ML), QK_CONV_WIDTH ** -0.5),
        "w_dw": nrm(ks[7], (CONV_WIDTH, D_CONV), CONV_WIDTH ** -0.5),
        "b_dw": nrm(ks[8], (D_CONV,), 0.02),
        "g_ln": gain(ks[9], D_CONV),
        "b_ln": nrm(ks[10], (D_CONV,), 0.02),
        "w_conv_out": nrm(ks[11], (D_CONV, D_MODEL), D_CONV ** -0.5),
        "g_ml_head": gain(ks[12], D_ML),
        "w_ml_out": nrm(ks[13], (D_ML, D_MODEL), D_ML ** -0.5),
        "g_mem": gain(ks[14], D_MODEL),
        "w_mem_kv": nrm(ks[15], (D_MODEL, 2 * D_XA), D_MODEL ** -0.5),
        "w_xa_out": nrm(ks[16], (D_XA, D_MODEL), D_XA ** -0.5),
        "w_out": nrm(ks[17], (D_MODEL, D_MODEL), D_MODEL ** -0.5),
        "g_post": gain(ks[18], D_MODEL),
    }


def _fwd_reference(x, mem, g_pre, w_in, b_if, w_qk_conv, w_dw, b_dw, g_ln, b_ln,
              w_conv_out, g_ml_head, w_ml_out, g_mem, w_mem_kv, w_xa_out,
              w_out, g_post):
    B, S, _ = x.shape
    f32 = jnp.float32
    for _layer in range(DEPTH):
        h = rms_norm(x, g_pre)

        u = in_cols(h, w_in, "glu_a") * jax.nn.sigmoid(in_cols(h, w_in, "glu_b"))
        u = causal_depthwise_conv(u, w_dw) + b_dw.astype(u.dtype)
        u = jax.nn.silu(layer_norm(u, g_ln, b_ln))
        y_conv = (u * jax.nn.silu(in_cols(h, w_in, "z_conv"))) @ w_conv_out

        qk = jax.nn.silu(causal_depthwise_conv(in_cols(h, w_in, "qk_ml"), w_qk_conv))
        q_ml, k_ml = jnp.split(qk, 2, axis=-1)
        v_ml = in_cols(h, w_in, "v_ml")

        def heads(t):
            return t.reshape(B, S, ML_HEADS, ML_HEAD_DIM).transpose(0, 2, 1, 3).astype(f32)

        gif = in_cols(h, w_in, "if_ml").astype(f32) + b_if.astype(f32)
        log_i = gif[..., :ML_HEADS].transpose(0, 2, 1)
        log_f = jax.nn.log_sigmoid(gif[..., ML_HEADS:]).transpose(0, 2, 1)
        hm = mlstm_chunkwise(heads(q_ml), heads(k_ml) * (ML_HEAD_DIM ** -0.5),
                             heads(v_ml), log_i, log_f)
        hm = hm.transpose(0, 2, 1, 3).reshape(B, S, D_ML)
        hm = jax.nn.sigmoid(in_cols(h, w_in, "o_ml").astype(f32)) * hm
        hm = hm.reshape(B, S, ML_HEADS, ML_HEAD_DIM)
        hm = hm * lax.rsqrt(jnp.mean(hm * hm, axis=-1, keepdims=True) + EPS)
        hm = (hm * g_ml_head.astype(f32).reshape(ML_HEADS, ML_HEAD_DIM)).reshape(B, S, D_ML)
        hm = hm.astype(x.dtype)
        y_ml = (hm * jax.nn.silu(in_cols(h, w_in, "z_ml"))) @ w_ml_out

        kv = rms_norm(mem, g_mem) @ w_mem_kv
        k_m, v_m = jnp.split(kv, 2, axis=-1)
        k_m = k_m.reshape(B, -1, XA_HEADS, XA_HEAD_DIM)
        v_m = v_m.reshape(B, -1, XA_HEADS, XA_HEAD_DIM)
        q_x = in_cols(h, w_in, "q_xa").reshape(B, S, XA_HEADS, XA_HEAD_DIM)
        scores = jnp.einsum("bshd,bmhd->bhsm", q_x, k_m).astype(f32) * (XA_HEAD_DIM ** -0.5)
        p = jax.nn.softmax(scores, axis=-1).astype(x.dtype)
        o_x = jnp.einsum("bhsm,bmhd->bshd", p, v_m).reshape(B, S, D_XA)
        y_xa = (o_x * jax.nn.silu(in_cols(h, w_in, "z_xa"))) @ w_xa_out

        g_c, g_m, g_x = jnp.split(jax.nn.sigmoid(in_cols(h, w_in, "gates")), N_BRANCH, axis=-1)
        merged = g_c * y_conv + g_m * y_ml.astype(x.dtype) + g_x * y_xa
        x = x + rms_norm(merged @ w_out, g_post)
    return x


import jax as _jax
import jax.numpy as _jnp

TWIN_FORMAT = 'train_step'
FWD_PARAMS = ['x', 'mem', 'g_pre', 'w_in', 'b_if', 'w_qk_conv', 'w_dw', 'b_dw', 'g_ln', 'b_ln', 'w_conv_out', 'g_ml_head', 'w_ml_out', 'g_mem', 'w_mem_kv', 'w_xa_out', 'w_out', 'g_post']
TWIN_WEIGHTS = ['g_pre', 'w_in', 'b_if', 'w_qk_conv', 'w_dw', 'b_dw', 'g_ln', 'b_ln', 'w_conv_out', 'g_ml_head', 'w_ml_out', 'g_mem', 'w_mem_kv', 'w_xa_out', 'w_out', 'g_post']
TWIN_DIFF_INPUT = 'x'
TWIN_INPUTS = ['x', 'mem', 'g_pre', 'w_in', 'b_if', 'w_qk_conv', 'w_dw', 'b_dw', 'g_ln', 'b_ln', 'w_conv_out', 'g_ml_head', 'w_ml_out', 'g_mem', 'w_mem_kv', 'w_xa_out', 'w_out', 'g_post', 'loss_target', 'm_g_pre', 'm_w_in', 'm_b_if', 'm_w_qk_conv', 'm_w_dw', 'm_b_dw', 'm_g_ln', 'm_b_ln', 'm_w_conv_out', 'm_g_ml_head', 'm_w_ml_out', 'm_g_mem', 'm_w_mem_kv', 'm_w_xa_out', 'm_w_out', 'm_g_post', 'v_g_pre', 'v_w_in', 'v_b_if', 'v_w_qk_conv', 'v_w_dw', 'v_b_dw', 'v_g_ln', 'v_b_ln', 'v_w_conv_out', 'v_g_ml_head', 'v_w_ml_out', 'v_g_mem', 'v_w_mem_kv', 'v_w_xa_out', 'v_w_out', 'v_g_post']
TWIN_OUTPUTS = ['loss', 'grad_x', 'grad_g_pre', 'grad_w_in', 'grad_b_if', 'grad_w_qk_conv', 'grad_w_dw', 'grad_b_dw', 'grad_g_ln', 'grad_b_ln', 'grad_w_conv_out', 'grad_g_ml_head', 'grad_w_ml_out', 'grad_g_mem', 'grad_w_mem_kv', 'grad_w_xa_out', 'grad_w_out', 'grad_g_post', 'delta_g_pre', 'delta_w_in', 'delta_b_if', 'delta_w_qk_conv', 'delta_w_dw', 'delta_b_dw', 'delta_g_ln', 'delta_b_ln', 'delta_w_conv_out', 'delta_g_ml_head', 'delta_w_ml_out', 'delta_g_mem', 'delta_w_mem_kv', 'delta_w_xa_out', 'delta_w_out', 'delta_g_post', 'new_m_g_pre', 'new_m_w_in', 'new_m_b_if', 'new_m_w_qk_conv', 'new_m_w_dw', 'new_m_b_dw', 'new_m_g_ln', 'new_m_b_ln', 'new_m_w_conv_out', 'new_m_g_ml_head', 'new_m_w_ml_out', 'new_m_g_mem', 'new_m_w_mem_kv', 'new_m_w_xa_out', 'new_m_w_out', 'new_m_g_post', 'new_v_g_pre', 'new_v_w_in', 'new_v_b_if', 'new_v_w_qk_conv', 'new_v_w_dw', 'new_v_b_dw', 'new_v_g_ln', 'new_v_b_ln', 'new_v_w_conv_out', 'new_v_g_ml_head', 'new_v_w_ml_out', 'new_v_g_mem', 'new_v_w_mem_kv', 'new_v_w_xa_out', 'new_v_w_out', 'new_v_g_post']
TWIN_LEAF_KINDS = {'loss': 'loss', 'grad_x': 'grad_x', 'grad_g_pre': 'grad_w', 'grad_w_in': 'grad_w', 'grad_b_if': 'grad_w', 'grad_w_qk_conv': 'grad_w', 'grad_w_dw': 'grad_w', 'grad_b_dw': 'grad_w', 'grad_g_ln': 'grad_w', 'grad_b_ln': 'grad_w', 'grad_w_conv_out': 'grad_w', 'grad_g_ml_head': 'grad_w', 'grad_w_ml_out': 'grad_w', 'grad_g_mem': 'grad_w', 'grad_w_mem_kv': 'grad_w', 'grad_w_xa_out': 'grad_w', 'grad_w_out': 'grad_w', 'grad_g_post': 'grad_w', 'delta_g_pre': 'delta_w', 'delta_w_in': 'delta_w', 'delta_b_if': 'delta_w', 'delta_w_qk_conv': 'delta_w', 'delta_w_dw': 'delta_w', 'delta_b_dw': 'delta_w', 'delta_g_ln': 'delta_w', 'delta_b_ln': 'delta_w', 'delta_w_conv_out': 'delta_w', 'delta_g_ml_head': 'delta_w', 'delta_w_ml_out': 'delta_w', 'delta_g_mem': 'delta_w', 'delta_w_mem_kv': 'delta_w', 'delta_w_xa_out': 'delta_w', 'delta_w_out': 'delta_w', 'delta_g_post': 'delta_w', 'new_m_g_pre': 'new_m', 'new_m_w_in': 'new_m', 'new_m_b_if': 'new_m', 'new_m_w_qk_conv': 'new_m', 'new_m_w_dw': 'new_m', 'new_m_b_dw': 'new_m', 'new_m_g_ln': 'new_m', 'new_m_b_ln': 'new_m', 'new_m_w_conv_out': 'new_m', 'new_m_g_ml_head': 'new_m', 'new_m_w_ml_out': 'new_m', 'new_m_g_mem': 'new_m', 'new_m_w_mem_kv': 'new_m', 'new_m_w_xa_out': 'new_m', 'new_m_w_out': 'new_m', 'new_m_g_post': 'new_m', 'new_v_g_pre': 'new_v', 'new_v_w_in': 'new_v', 'new_v_b_if': 'new_v', 'new_v_w_qk_conv': 'new_v', 'new_v_w_dw': 'new_v', 'new_v_b_dw': 'new_v', 'new_v_g_ln': 'new_v', 'new_v_b_ln': 'new_v', 'new_v_w_conv_out': 'new_v', 'new_v_g_ml_head': 'new_v', 'new_v_w_ml_out': 'new_v', 'new_v_g_mem': 'new_v', 'new_v_w_mem_kv': 'new_v', 'new_v_w_xa_out': 'new_v', 'new_v_w_out': 'new_v', 'new_v_g_post': 'new_v'}


def _forward(args):
    return _fwd_reference(*[args[k] for k in FWD_PARAMS])


def _output_shape():
    def fwd():
        inp = _fwd_setup_inputs(0)
        return _fwd_reference(*[inp[k] for k in FWD_PARAMS])
    out = _jax.eval_shape(fwd)
    return out.shape, out.dtype

N_MICROBATCH = 1
ADAM_LR = 0.001
ADAM_B1 = 0.9
ADAM_B2 = 0.999
ADAM_EPS = 1e-08
ADAM_WD = 0.01
ADAM_STEP = 10
PER_EXAMPLE_BATCH_AXIS = {'x': 0, 'mem': 0, 'loss_target': 0}
SHARED_INPUTS = []
_WEIGHT_DTYPES = {'g_pre': _jnp.float32, 'w_in': _jnp.float32, 'b_if': _jnp.float32, 'w_qk_conv': _jnp.float32, 'w_dw': _jnp.float32, 'b_dw': _jnp.float32, 'g_ln': _jnp.float32, 'b_ln': _jnp.float32, 'w_conv_out': _jnp.float32, 'g_ml_head': _jnp.float32, 'w_ml_out': _jnp.float32, 'g_mem': _jnp.float32, 'w_mem_kv': _jnp.float32, 'w_xa_out': _jnp.float32, 'w_out': _jnp.float32, 'g_post': _jnp.float32}
MOMENT_SCALE = {'g_pre': 3.525220e-01, 'w_in': 9.327985e-02, 'b_if': 2.614871e+00, 'w_qk_conv': 6.689737e-02, 'w_dw': 9.499041e-02, 'b_dw': 2.598480e-01, 'g_ln': 1.248316e-01, 'b_ln': 1.577813e-01, 'w_conv_out': 1.054468e-01, 'g_ml_head': 1.513552e-01, 'w_ml_out': 1.584762e-01, 'g_mem': 2.325681e-02, 'w_mem_kv': 1.634918e-02, 'w_xa_out': 1.719357e-02, 'w_out': 1.914983e-01, 'g_post': 3.199397e+01}


def _to_microbatches(a, axis):
    t = _jnp.moveaxis(a, axis, 0)
    t = t.reshape((N_MICROBATCH, t.shape[0] // N_MICROBATCH) + t.shape[1:])
    return _jnp.moveaxis(t, 1, axis + 1)


def setup_inputs(seed: int = 0) -> dict:
    inp = _fwd_setup_inputs(seed)
    key = _jax.random.fold_in(_jax.random.key(seed), 7919)
    shape, _ = _output_shape()
    out = dict(inp)
    out["loss_target"] = _jax.random.normal(_jax.random.fold_in(key, 0), shape, _jnp.float32)
    for i, name in enumerate(TWIN_WEIGHTS):
        w = inp[name].astype(_jnp.float32)
        if MOMENT_SCALE is None:
            s = _jnp.sqrt(_jnp.mean(_jnp.square(w)) + 1e-30)
        else:
            s = MOMENT_SCALE[name]
        km, kv = _jax.random.split(_jax.random.fold_in(key, i + 1))
        out[name] = w
        out["m_" + name] = s * _jax.random.normal(km, w.shape, _jnp.float32)
        out["v_" + name] = (s * s) * _jax.random.uniform(kv, w.shape, _jnp.float32, 0.5, 1.5)
    if N_MICROBATCH > 1:
        for name, axis in PER_EXAMPLE_BATCH_AXIS.items():
            out[name] = _to_microbatches(out[name], axis)
    return {'x': out['x'], 'mem': out['mem'], 'g_pre': out['g_pre'], 'w_in': out['w_in'], 'b_if': out['b_if'], 'w_qk_conv': out['w_qk_conv'], 'w_dw': out['w_dw'], 'b_dw': out['b_dw'], 'g_ln': out['g_ln'], 'b_ln': out['b_ln'], 'w_conv_out': out['w_conv_out'], 'g_ml_head': out['g_ml_head'], 'w_ml_out': out['w_ml_out'], 'g_mem': out['g_mem'], 'w_mem_kv': out['w_mem_kv'], 'w_xa_out': out['w_xa_out'], 'w_out': out['w_out'], 'g_post': out['g_post'], 'loss_target': out['loss_target'], 'm_g_pre': out['m_g_pre'], 'm_w_in': out['m_w_in'], 'm_b_if': out['m_b_if'], 'm_w_qk_conv': out['m_w_qk_conv'], 'm_w_dw': out['m_w_dw'], 'm_b_dw': out['m_b_dw'], 'm_g_ln': out['m_g_ln'], 'm_b_ln': out['m_b_ln'], 'm_w_conv_out': out['m_w_conv_out'], 'm_g_ml_head': out['m_g_ml_head'], 'm_w_ml_out': out['m_w_ml_out'], 'm_g_mem': out['m_g_mem'], 'm_w_mem_kv': out['m_w_mem_kv'], 'm_w_xa_out': out['m_w_xa_out'], 'm_w_out': out['m_w_out'], 'm_g_post': out['m_g_post'], 'v_g_pre': out['v_g_pre'], 'v_w_in': out['v_w_in'], 'v_b_if': out['v_b_if'], 'v_w_qk_conv': out['v_w_qk_conv'], 'v_w_dw': out['v_w_dw'], 'v_b_dw': out['v_b_dw'], 'v_g_ln': out['v_g_ln'], 'v_b_ln': out['v_b_ln'], 'v_w_conv_out': out['v_w_conv_out'], 'v_g_ml_head': out['v_g_ml_head'], 'v_w_ml_out': out['v_w_ml_out'], 'v_g_mem': out['v_g_mem'], 'v_w_mem_kv': out['v_w_mem_kv'], 'v_w_xa_out': out['v_w_xa_out'], 'v_w_out': out['v_w_out'], 'v_g_post': out['v_g_post']}


def _loss(weights, diff, rest, loss_target):
    with _jax.named_scope("forward"):
        args = {**rest, TWIN_DIFF_INPUT: diff, **{k: w.astype(_WEIGHT_DTYPES[k]) for k, w in weights.items()}}
        y = _forward(args)
    with _jax.named_scope("loss_head"):
        err = _jnp.square(y.astype(_jnp.float32) - loss_target)
        return 0.5 * _jnp.sum(_jnp.mean(err, axis=-1)) if err.ndim else 0.5 * err


def _adamw(w, g, m, v):
    m = ADAM_B1 * m + (1.0 - ADAM_B1) * g
    v = ADAM_B2 * v + (1.0 - ADAM_B2) * _jnp.square(g)
    m_hat = m / (1.0 - ADAM_B1 ** ADAM_STEP)
    v_hat = v / (1.0 - ADAM_B2 ** ADAM_STEP)
    delta = -ADAM_LR * (m_hat / (_jnp.sqrt(v_hat) + ADAM_EPS) + ADAM_WD * w)
    return delta, m, v


def reference(x, mem, g_pre, w_in, b_if, w_qk_conv, w_dw, b_dw, g_ln, b_ln, w_conv_out, g_ml_head, w_ml_out, g_mem, w_mem_kv, w_xa_out, w_out, g_post, loss_target, m_g_pre, m_w_in, m_b_if, m_w_qk_conv, m_w_dw, m_b_dw, m_g_ln, m_b_ln, m_w_conv_out, m_g_ml_head, m_w_ml_out, m_g_mem, m_w_mem_kv, m_w_xa_out, m_w_out, m_g_post, v_g_pre, v_w_in, v_b_if, v_w_qk_conv, v_w_dw, v_b_dw, v_g_ln, v_b_ln, v_w_conv_out, v_g_ml_head, v_w_ml_out, v_g_mem, v_w_mem_kv, v_w_xa_out, v_w_out, v_g_post):
    given = dict(x=x, mem=mem, g_pre=g_pre, w_in=w_in, b_if=b_if, w_qk_conv=w_qk_conv, w_dw=w_dw, b_dw=b_dw, g_ln=g_ln, b_ln=b_ln, w_conv_out=w_conv_out, g_ml_head=g_ml_head, w_ml_out=w_ml_out, g_mem=g_mem, w_mem_kv=w_mem_kv, w_xa_out=w_xa_out, w_out=w_out, g_post=g_post, loss_target=loss_target, m_g_pre=m_g_pre, m_w_in=m_w_in, m_b_if=m_b_if, m_w_qk_conv=m_w_qk_conv, m_w_dw=m_w_dw, m_b_dw=m_b_dw, m_g_ln=m_g_ln, m_b_ln=m_b_ln, m_w_conv_out=m_w_conv_out, m_g_ml_head=m_g_ml_head, m_w_ml_out=m_w_ml_out, m_g_mem=m_g_mem, m_w_mem_kv=m_w_mem_kv, m_w_xa_out=m_w_xa_out, m_w_out=m_w_out, m_g_post=m_g_post, v_g_pre=v_g_pre, v_w_in=v_w_in, v_b_if=v_b_if, v_w_qk_conv=v_w_qk_conv, v_w_dw=v_w_dw, v_b_dw=v_b_dw, v_g_ln=v_g_ln, v_b_ln=v_b_ln, v_w_conv_out=v_w_conv_out, v_g_ml_head=v_g_ml_head, v_w_ml_out=v_w_ml_out, v_g_mem=v_g_mem, v_w_mem_kv=v_w_mem_kv, v_w_xa_out=v_w_xa_out, v_w_out=v_w_out, v_g_post=v_g_post)
    weights = {n: given[n] for n in TWIN_WEIGHTS}
    shared = {n: given[n] for n in SHARED_INPUTS}
    per_example = {n: given[n] for n in ['x', 'mem']}
    grad_fn = _jax.value_and_grad(_loss, argnums=(0, 1))

    def one_microbatch(ex, loss_target):
        ex = dict(ex)
        diff = ex.pop(TWIN_DIFF_INPUT)
        return grad_fn(weights, diff, {**shared, **ex}, loss_target)

    if N_MICROBATCH == 1:
        loss, (grad_w, grad_x) = one_microbatch(per_example, given["loss_target"])
    else:
        def body(carry, xs):
            loss_sum, grad_sum = carry
            l_k, (gw_k, gx_k) = one_microbatch(xs[0], xs[1])
            with _jax.named_scope("update"):
                return (loss_sum + l_k, _jax.tree.map(_jnp.add, grad_sum, gw_k)), gx_k

        init = (_jnp.zeros((), _jnp.float32), _jax.tree.map(_jnp.zeros_like, weights))
        (loss, grad_w), grad_x = _jax.lax.scan(body, init, (per_example, given["loss_target"]))
    with _jax.named_scope("update"):
        delta_w, new_m, new_v = {}, {}, {}
        for n in TWIN_WEIGHTS:
            delta_w[n], new_m[n], new_v[n] = _adamw(weights[n], grad_w[n], given["m_" + n], given["v_" + n])
    return (loss, grad_x, *[grad_w[n] for n in TWIN_WEIGHTS], *[delta_w[n] for n in TWIN_WEIGHTS],
            *[new_m[n] for n in TWIN_WEIGHTS], *[new_v[n] for n in TWIN_WEIGHTS])
```

```python
import jax
import jax.numpy as jnp
from jax import lax
from jax.experimental import pallas as pl
from jax.experimental.pallas import tpu as pltpu

F32, BF16 = jnp.float32, jnp.bfloat16
EPS = 1e-6
N_HEADS = 4
CONV_W = 31
QK_W = 4
HALO = 32
QK_HALO = 8
ROW_TILE = 256
ML_CHUNK = 256
VMEM_LIMIT = 56 * 1024 * 1024
NEG = -1e30
ADAM_LR, ADAM_B1, ADAM_B2, ADAM_EPS, ADAM_WD, ADAM_STEP = 0.001, 0.9, 0.999, 1e-08, 0.01, 10
MESH = pl.DeviceIdType.MESH
N_DEV = 8

NN = (((1,), (0,)), ((), ()))
NT = (((1,), (1,)), ((), ()))
TN = (((0,), (0,)), ((), ()))


def _call(body, **kw):
    return pl.pallas_call(body, **kw)


def _cp(sem):
    return pltpu.CompilerParams(dimension_semantics=sem, vmem_limit_bytes=VMEM_LIMIT)


def _sds(shape, dt):
    return jax.ShapeDtypeStruct(shape, dt)


def _row(tm, d, col=0):
    return pl.BlockSpec((tm, d), lambda i: (i, col))


def _par(r, d):
    return pl.BlockSpec((r, d), lambda i: (0, 0))


def _prev(tm, hb, d, col=0):
    return pl.BlockSpec((hb, d), lambda i: (jnp.maximum(i * (tm // hb) - 1, 0), col))


def _next(tm, hb, d, nblk, col=0):
    return pl.BlockSpec((hb, d), lambda i: (jnp.minimum((i + 1) * (tm // hb), nblk - 1), col))


def _dot(a, b, dn):
    return lax.dot_general(a.astype(BF16), b.astype(BF16), dn, preferred_element_type=F32)


def _sg(x):
    return jax.nn.sigmoid(x)


def _dsilu(z, s):
    return s * (1.0 + z * (1.0 - s))


def _mean(x):
    return jnp.mean(x, axis=-1, keepdims=True)


def _pick(n, pref):
    if n <= pref:
        return n
    t = pref
    while n % t:
        t -= 128
    return t


def _mm(a, b, mode, out_dtype, name, tm=1024, tn=1024, tk=2048):
    if mode == "nn":
        (M, K), N = a.shape, b.shape[1]
    elif mode == "nt":
        (M, K), N = a.shape, b.shape[0]
    else:
        (K, M), N = a.shape, b.shape[1]
    tm, tn, tk = _pick(M, tm), _pick(N, tn), _pick(K, tk)
    nk = K // tk
    if mode == "nn":
        sa, sb, dn = pl.BlockSpec((tm, tk), lambda i, j, k: (i, k)), pl.BlockSpec((tk, tn), lambda i, j, k: (k, j)), NN
    elif mode == "nt":
        sa, sb, dn = pl.BlockSpec((tm, tk), lambda i, j, k: (i, k)), pl.BlockSpec((tn, tk), lambda i, j, k: (j, k)), NT
    else:
        sa, sb, dn = pl.BlockSpec((tk, tm), lambda i, j, k: (k, i)), pl.BlockSpec((tk, tn), lambda i, j, k: (k, j)), TN

    def body(a_ref, b_ref, o_ref, *acc):
        p = _dot(a_ref[...], b_ref[...], dn)
        if nk == 1:
            o_ref[...] = p.astype(out_dtype)
        else:
            acc_ref, k = acc[0], pl.program_id(2)

            @pl.when(k == 0)
            def _():
                acc_ref[...] = p

            @pl.when(k > 0)
            def _():
                acc_ref[...] += p

            @pl.when(k == nk - 1)
            def _():
                o_ref[...] = acc_ref[...].astype(out_dtype)

    return _call(body, name=name, grid=(M // tm, N // tn, nk), in_specs=[sa, sb],
                 out_specs=pl.BlockSpec((tm, tn), lambda i, j, k: (i, j)), out_shape=_sds((M, N), out_dtype),
                 scratch_shapes=[pltpu.VMEM((tm, tn), F32)] if nk > 1 else [],
                 compiler_params=_cp(("parallel", "parallel", "arbitrary")))(a, b)


def _rms_fwd(x, g, name):
    S, D = x.shape
    tm = min(ROW_TILE, S)

    def body(x_ref, g_ref, h_ref):
        xv = x_ref[...]
        r = lax.rsqrt(_mean(xv * xv) + EPS)
        h_ref[...] = (xv * r * g_ref[...]).astype(BF16)

    return _call(body, name=name, grid=(S // tm,), in_specs=[_row(tm, D), _par(1, D)], out_specs=_row(tm, D),
                 out_shape=_sds((S, D), BF16), compiler_params=_cp(("parallel",)))(x, g)


def _rms_bwd(x, g, ds, resid, name):
    S, D = x.shape
    tm = min(ROW_TILE, S)
    nd = len(ds)

    def body(*refs):
        x_ref, g_ref = refs[:2]
        d_refs = refs[2:2 + nd]
        r_ref = refs[2 + nd] if resid is not None else None
        dx_ref, dg_ref = refs[-2:]
        xv = x_ref[...]
        d = d_refs[0][...]
        for dr in d_refs[1:]:
            d = d + dr[...]
        r = lax.rsqrt(_mean(xv * xv) + EPS)
        dxh = d * g_ref[...]
        dx = r * dxh - xv * (r * r * r) * _mean(dxh * xv)
        if r_ref is not None:
            dx = dx + r_ref[...]
        dx_ref[...] = dx

        @pl.when(pl.program_id(0) == 0)
        def _():
            dg_ref[...] = jnp.zeros_like(dg_ref)

        dg_ref[...] += jnp.sum(d * xv * r, axis=0, keepdims=True)

    ins = [x, g, *ds] + ([resid] if resid is not None else [])
    specs = [_row(tm, D), _par(1, D)] + [_row(tm, D)] * (len(ins) - 2)
    return _call(body, name=name, grid=(S // tm,), in_specs=specs, out_specs=[_row(tm, D), _par(1, D)],
                 out_shape=[_sds((S, D), F32), _sds((1, D), F32)], compiler_params=_cp(("arbitrary",)))(*ins)


def _conv_fwd(P, w_dw, b_dw, g_ln, b_ln, D):
    S = P.shape[0]
    tm = min(ROW_TILE, S)

    def body(a_ref, b_ref, ap_ref, bp_ref, z_ref, w_ref, bd_ref, g_ref, bl_ref, ca_ref, u1_ref, buf):
        first = pl.program_id(0) == 0
        hp = ap_ref[...] * _sg(bp_ref[...])
        buf[0:HALO, :] = jnp.where(first, 0.0, hp)
        buf[HALO:HALO + tm, :] = a_ref[...] * _sg(b_ref[...])
        acc = jnp.zeros((tm, D), F32)
        for j in range(CONV_W):
            acc = acc + w_ref[j:j + 1, :] * buf[HALO - CONV_W + 1 + j:HALO - CONV_W + 1 + j + tm, :]
        u1 = acc + bd_ref[...]
        u1_ref[...] = u1
        xc = u1 - _mean(u1)
        r = lax.rsqrt(_mean(xc * xc) + EPS)
        u2 = xc * r * g_ref[...] + bl_ref[...]
        z = z_ref[...]
        ca_ref[...] = (u2 * _sg(u2) * (z * _sg(z))).astype(BF16)

    return _call(body, name="conv_fwd", grid=(S // tm,),
                 in_specs=[_row(tm, D, 0), _row(tm, D, 1), _prev(tm, HALO, D, 0), _prev(tm, HALO, D, 1), _row(tm, D, 2),
                           _par(HALO, D), _par(1, D), _par(1, D), _par(1, D)],
                 out_specs=[_row(tm, D), _row(tm, D)], out_shape=[_sds((S, D), BF16), _sds((S, D), F32)],
                 scratch_shapes=[pltpu.VMEM((tm + HALO, D), F32)], compiler_params=_cp(("parallel",)))(
        P, P, P, P, P, w_dw, b_dw, g_ln, b_ln)


def _conv_bwd_local(d_ca, u1, P, g_ln, b_ln, D):
    S = P.shape[0]
    tm = min(ROW_TILE, S)

    def body(dca_ref, u1_ref, z_ref, g_ref, bl_ref, du1_ref, dz_ref, dg_ref, db_ref):
        u1, z, dca, g = u1_ref[...], z_ref[...], dca_ref[...], g_ref[...]
        xc = u1 - _mean(u1)
        r = lax.rsqrt(_mean(xc * xc) + EPS)
        xh = xc * r
        u2 = xh * g + bl_ref[...]
        s2, sz = _sg(u2), _sg(z)
        d_u3 = dca * (z * sz)
        dz_ref[...] = (dca * (u2 * s2) * _dsilu(z, sz)).astype(BF16)
        d_u2 = d_u3 * _dsilu(u2, s2)
        dxh = d_u2 * g
        du1_ref[...] = r * (dxh - _mean(dxh) - xh * _mean(dxh * xh))

        @pl.when(pl.program_id(0) == 0)
        def _():
            dg_ref[...] = jnp.zeros_like(dg_ref)
            db_ref[...] = jnp.zeros_like(db_ref)

        dg_ref[...] += jnp.sum(d_u2 * xh, axis=0, keepdims=True)
        db_ref[...] += jnp.sum(d_u2, axis=0, keepdims=True)

    return _call(body, name="conv_bwd_local", grid=(S // tm,),
                 in_specs=[_row(tm, D), _row(tm, D), _row(tm, D, 2), _par(1, D), _par(1, D)],
                 out_specs=[_row(tm, D), _row(tm, D), _par(1, D), _par(1, D)],
                 out_shape=[_sds((S, D), F32), _sds((S, D), BF16), _sds((1, D), F32), _sds((1, D), F32)],
                 compiler_params=_cp(("arbitrary",)))(d_ca, u1, P, g_ln, b_ln)


def _conv_bwd_taps(d_u1, P, w_dw, D):
    S = P.shape[0]
    tm = min(ROW_TILE, S)
    nblk = S // HALO

    def body(d_ref, dn_ref, a_ref, b_ref, ap_ref, bp_ref, w_ref, da_ref, db_ref, gw_ref, gb_ref, dbuf, ubuf):
        i = pl.program_id(0)
        a, sb = a_ref[...], _sg(b_ref[...])
        d = d_ref[...]
        dbuf[0:tm, :] = d
        dbuf[tm:tm + HALO, :] = jnp.where(i == pl.num_programs(0) - 1, 0.0, dn_ref[...])
        ubuf[0:HALO, :] = jnp.where(i == 0, 0.0, ap_ref[...] * _sg(bp_ref[...]))
        ubuf[HALO:HALO + tm, :] = a * sb

        @pl.when(i == 0)
        def _():
            gw_ref[...] = jnp.zeros_like(gw_ref)
            gb_ref[...] = jnp.zeros_like(gb_ref)

        acc = jnp.zeros((tm, D), F32)
        for j in range(CONV_W):
            acc = acc + w_ref[j:j + 1, :] * dbuf[CONV_W - 1 - j:CONV_W - 1 - j + tm, :]
            sh = HALO - CONV_W + 1 + j
            gw_ref[j:j + 1, :] += jnp.sum(d * ubuf[sh:sh + tm, :], axis=0, keepdims=True)
        gb_ref[...] += jnp.sum(d, axis=0, keepdims=True)
        da_ref[...] = (acc * sb).astype(BF16)
        db_ref[...] = (acc * a * sb * (1.0 - sb)).astype(BF16)

    return _call(body, name="conv_bwd_taps", grid=(S // tm,),
                 in_specs=[_row(tm, D), _next(tm, HALO, D, nblk), _row(tm, D, 0), _row(tm, D, 1),
                           _prev(tm, HALO, D, 0), _prev(tm, HALO, D, 1), _par(HALO, D)],
                 out_specs=[_row(tm, D), _row(tm, D), _par(HALO, D), _par(1, D)],
                 out_shape=[_sds((S, D), BF16), _sds((S, D), BF16), _sds((HALO, D), F32), _sds((1, D), F32)],
                 scratch_shapes=[pltpu.VMEM((tm + HALO, D), F32), pltpu.VMEM((tm + HALO, D), F32)],
                 compiler_params=_cp(("arbitrary",)))(d_u1, d_u1, P, P, P, P, w_dw)


def _qk_fwd(P, wq, wk, D, col):
    S = P.shape[0]
    tm = min(ROW_TILE, S)
    scale = float(D // N_HEADS) ** -0.5

    def body(q_ref, k_ref, qp_ref, kp_ref, wq_ref, wk_ref, qo_ref, ko_ref, cq_ref, ck_ref, buf):
        first = pl.program_id(0) == 0
        for x_ref, p_ref, w_ref, o_ref, c_ref, sc in ((q_ref, qp_ref, wq_ref, qo_ref, cq_ref, 1.0),
                                                      (k_ref, kp_ref, wk_ref, ko_ref, ck_ref, scale)):
            buf[0:QK_HALO, :] = jnp.where(first, 0.0, p_ref[...])
            buf[QK_HALO:QK_HALO + tm, :] = x_ref[...]
            acc = jnp.zeros((tm, D), F32)
            for j in range(QK_W):
                sh = QK_HALO - QK_W + 1 + j
                acc = acc + w_ref[j:j + 1, :] * buf[sh:sh + tm, :]
            c_ref[...] = acc
            o_ref[...] = acc * _sg(acc) * sc

    return _call(body, name="qk_fwd", grid=(S // tm,),
                 in_specs=[_row(tm, D, col), _row(tm, D, col + 1), _prev(tm, QK_HALO, D, col), _prev(tm, QK_HALO, D, col + 1),
                           _par(QK_HALO, D), _par(QK_HALO, D)],
                 out_specs=[_row(tm, D)] * 4, out_shape=[_sds((S, D), F32)] * 4,
                 scratch_shapes=[pltpu.VMEM((tm + QK_HALO, D), F32)], compiler_params=_cp(("parallel",)))(P, P, P, P, wq, wk)


def _qk_bwd(dq, dk, cq, ck, P, wq, wk, D, col):
    S = P.shape[0]
    tm = min(ROW_TILE, S)
    nblk = S // QK_HALO
    scale = float(D // N_HEADS) ** -0.5

    def body(dq_ref, dqn_ref, cq_ref, cqn_ref, xq_ref, xqp_ref, wq_ref,
             dk_ref, dkn_ref, ck_ref, ckn_ref, xk_ref, xkp_ref, wk_ref,
             oq_ref, ok_ref, gq_ref, gk_ref, dbuf, xbuf):
        i = pl.program_id(0)
        last = i == pl.num_programs(0) - 1

        @pl.when(i == 0)
        def _():
            gq_ref[...] = jnp.zeros_like(gq_ref)
            gk_ref[...] = jnp.zeros_like(gk_ref)

        for d_ref, dn_ref, c_ref, cn_ref, x_ref, xp_ref, w_ref, o_ref, g_ref, sc in (
                (dq_ref, dqn_ref, cq_ref, cqn_ref, xq_ref, xqp_ref, wq_ref, oq_ref, gq_ref, 1.0),
                (dk_ref, dkn_ref, ck_ref, ckn_ref, xk_ref, xkp_ref, wk_ref, ok_ref, gk_ref, scale)):
            c, cn = c_ref[...], cn_ref[...]
            dc = d_ref[...] * sc * _dsilu(c, _sg(c))
            dcn = dn_ref[...] * sc * _dsilu(cn, _sg(cn))
            dbuf[0:tm, :] = dc
            dbuf[tm:tm + QK_HALO, :] = jnp.where(last, 0.0, dcn)
            xbuf[0:QK_HALO, :] = jnp.where(i == 0, 0.0, xp_ref[...])
            xbuf[QK_HALO:QK_HALO + tm, :] = x_ref[...]
            acc = jnp.zeros((tm, D), F32)
            for j in range(QK_W):
                acc = acc + w_ref[j:j + 1, :] * dbuf[QK_W - 1 - j:QK_W - 1 - j + tm, :]
                sh = QK_HALO - QK_W + 1 + j
                g_ref[j:j + 1, :] += jnp.sum(dc * xbuf[sh:sh + tm, :], axis=0, keepdims=True)
            o_ref[...] = acc.astype(BF16)

    one = [_row(tm, D), _next(tm, QK_HALO, D, nblk)]
    specs = (one + one + [_row(tm, D, col), _prev(tm, QK_HALO, D, col), _par(QK_HALO, D)]
             + one + one + [_row(tm, D, col + 1), _prev(tm, QK_HALO, D, col + 1), _par(QK_HALO, D)])
    return _call(body, name="qk_bwd", grid=(S // tm,), in_specs=specs,
                 out_specs=[_row(tm, D), _row(tm, D), _par(QK_HALO, D), _par(QK_HALO, D)],
                 out_shape=[_sds((S, D), BF16), _sds((S, D), BF16), _sds((QK_HALO, D), F32), _sds((QK_HALO, D), F32)],
                 scratch_shapes=[pltpu.VMEM((tm + QK_HALO, D), F32), pltpu.VMEM((tm + QK_HALO, D), F32)],
                 compiler_params=_cp(("arbitrary",)))(dq, dq, cq, cq, P, P, wq, dk, dk, ck, ck, P, P, wk)


def _gates_fwd(gif, b_if):
    S = gif.shape[0]
    tm = min(ROW_TILE, S)

    def body(g_ref, b_ref, o_ref):
        z = g_ref[...] + b_ref[...]
        lane = lax.broadcasted_iota(jnp.int32, z.shape, 1)
        ls = jnp.minimum(z, 0.0) - jnp.log(1.0 + jnp.exp(-jnp.abs(z)))
        o_ref[...] = jnp.where(lane < N_HEADS, z, jnp.where(lane < 2 * N_HEADS, ls, 0.0))

    return _call(body, name="gates_fwd", grid=(S // tm,), in_specs=[_row(tm, 128), _par(1, 128)], out_specs=_row(tm, 128),
                 out_shape=_sds((S, 128), F32), compiler_params=_cp(("parallel",)))(gif, b_if)


def _gates_bwd(dgl, gif, b_if):
    S = gif.shape[0]
    tm = min(ROW_TILE, S)

    def body(d_ref, g_ref, b_ref, o_ref, gb_ref):
        z = g_ref[...] + b_ref[...]
        lane = lax.broadcasted_iota(jnp.int32, z.shape, 1)
        d = d_ref[...]
        dz = jnp.where(lane < N_HEADS, d, jnp.where(lane < 2 * N_HEADS, d * _sg(-z), 0.0))
        o_ref[...] = dz.astype(BF16)

        @pl.when(pl.program_id(0) == 0)
        def _():
            gb_ref[...] = jnp.zeros_like(gb_ref)

        gb_ref[...] += jnp.sum(dz, axis=0, keepdims=True)

    return _call(body, name="gates_bwd", grid=(S // tm,), in_specs=[_row(tm, 128), _row(tm, 128), _par(1, 128)],
                 out_specs=[_row(tm, 128), _par(1, 128)], out_shape=[_sds((S, 128), BF16), _sds((1, 128), F32)],
                 compiler_params=_cp(("arbitrary",)))(dgl, gif, b_if)


def _chunk_gates(g8_ref, gt_ref, h, L):
    g8, gt = g8_ref[...], gt_ref[...]
    lane = lax.broadcasted_iota(jnp.int32, g8.shape, 1)
    sub = lax.broadcasted_iota(jnp.int32, gt.shape, 0)
    li_col = jnp.sum(jnp.where(lane == h, g8, 0.0), axis=1, keepdims=True)
    lf_col = jnp.sum(jnp.where(lane == h + N_HEADS, g8, 0.0), axis=1, keepdims=True)
    li_row = jnp.sum(jnp.where(sub == h, gt, 0.0), axis=0, keepdims=True)
    lf_row = jnp.sum(jnp.where(sub == h + N_HEADS, gt, 0.0), axis=0, keepdims=True)
    t = lax.broadcasted_iota(jnp.int32, (L, L), 0)
    s = lax.broadcasted_iota(jnp.int32, (L, L), 1)
    causal = s <= t
    b_col = jnp.sum(jnp.where(causal, lf_row, 0.0), axis=1, keepdims=True)
    b_row = jnp.sum(jnp.where(t <= s, lf_col, 0.0), axis=0, keepdims=True)
    return li_col, li_row, b_col, b_row, causal


def _chunk_fwd(q, k, v, C, n, m, li_row, b_col, b_row, causal):
    d = jnp.where(causal, b_col - b_row + li_row, NEG)
    inter = b_col + m
    m_row = jnp.maximum(inter, jnp.max(d, axis=1, keepdims=True))
    wi = jnp.exp(d - m_row)
    wn = jnp.exp(inter - m_row)
    s = _dot(q, k, NT) * wi
    num = _dot(s, v, NN) + wn * _dot(q, C, NN)
    den = jnp.sum(s, axis=1, keepdims=True) + wn * jnp.sum(q * n, axis=1, keepdims=True)
    e = jnp.exp(-m_row)
    inv = 1.0 / jnp.maximum(jnp.abs(den), e)
    return wi, wn, s, num * inv, den, e, inv


def _chunk_state(li_col, li_row, b_col, b_row, m, L):
    blast = b_col[L - 1:L, :]
    g_col = blast - b_col + li_col
    g_row = blast - b_row + li_row
    m_new = jnp.maximum(blast + m, jnp.max(g_row, axis=1, keepdims=True))
    decay = jnp.exp(blast + m - m_new)
    wk_col = jnp.exp(g_col - m_new)
    return m_new, decay, wk_col


def _mlstm_fwd(q, k, P, g8, gt, D, vcol):
    S = q.shape[0]
    H, dh = N_HEADS, D // N_HEADS
    L = min(ML_CHUNK, S)
    nc = S // L

    def body(q_ref, k_ref, v_ref, g8_ref, gt_ref, h_ref, cs_ref, ns_ref, ms_ref, C, n, m):
        h, j = pl.program_id(0), pl.program_id(1)

        @pl.when(j == 0)
        def _():
            C[...] = jnp.zeros_like(C)
            n[...] = jnp.zeros_like(n)
            m[...] = jnp.zeros_like(m)

        qv, kv, vv = q_ref[...], k_ref[...], v_ref[...]
        Cv, nv, mv = C[...], n[...], m[:, 0:1]
        cs_ref[...] = Cv.astype(BF16)
        ns_ref[...] = nv
        ms_ref[...] = m[...]
        li_col, li_row, b_col, b_row, causal = _chunk_gates(g8_ref, gt_ref, h, L)
        _, _, _, hv, _, _, _ = _chunk_fwd(qv, kv, vv, Cv, nv, mv, li_row, b_col, b_row, causal)
        h_ref[...] = hv
        m_new, decay, wk_col = _chunk_state(li_col, li_row, b_col, b_row, mv, L)
        kw = kv * wk_col
        C[...] = decay * Cv + _dot(kw, vv, TN)
        n[...] = decay * nv + jnp.sum(kw, axis=0, keepdims=True)
        m[...] = jnp.broadcast_to(m_new, m.shape)

    blk = lambda c0: pl.BlockSpec((L, dh), lambda h, j: (j, c0 + h))
    return _call(body, name="mlstm_fwd", grid=(H, nc),
                 in_specs=[blk(0), blk(0), blk(vcol * H), pl.BlockSpec((L, 8), lambda h, j: (j, 0)),
                           pl.BlockSpec((8, L), lambda h, j: (0, j))],
                 out_specs=[blk(0), pl.BlockSpec((None, None, dh, dh), lambda h, j: (h, j, 0, 0)),
                            pl.BlockSpec((None, None, 1, dh), lambda h, j: (h, j, 0, 0)),
                            pl.BlockSpec((None, None, 1, 128), lambda h, j: (h, j, 0, 0))],
                 out_shape=[_sds((S, D), F32), _sds((H, nc, dh, dh), BF16), _sds((H, nc, 1, dh), F32),
                            _sds((H, nc, 1, 128), F32)],
                 scratch_shapes=[pltpu.VMEM((dh, dh), F32), pltpu.VMEM((1, dh), F32), pltpu.VMEM((1, 128), F32)],
                 compiler_params=_cp(("arbitrary", "arbitrary")))(q, k, P, g8, gt)


def _mlstm_bwd(dhm, q, k, P, g8, gt, cs, ns, ms, D, vcol):
    S = q.shape[0]
    H, dh = N_HEADS, D // N_HEADS
    L = min(ML_CHUNK, S)
    nc = S // L

    def body(dh_ref, q_ref, k_ref, v_ref, g8_ref, gt_ref, cs_ref, ns_ref, ms_ref,
             dq_ref, dk_ref, dv_ref, dg_ref, dC, dnv):
        h, j = pl.program_id(0), pl.program_id(1)

        @pl.when(j == 0)
        def _():
            dC[...] = jnp.zeros_like(dC)
            dnv[...] = jnp.zeros_like(dnv)

        qv, kv, vv, dhv = q_ref[...], k_ref[...], v_ref[...], dh_ref[...]
        Cv, nv, mv = cs_ref[...], ns_ref[...], ms_ref[:, 0:1]
        li_col, li_row, b_col, b_row, causal = _chunk_gates(g8_ref, gt_ref, h, L)
        wi, wn, s, hv, den, e, inv = _chunk_fwd(qv, kv, vv, Cv, nv, mv, li_row, b_col, b_row, causal)
        dn = dhv * inv
        dd = -jnp.sum(dhv * hv, axis=1, keepdims=True) * inv * jnp.where(jnp.abs(den) > e, jnp.sign(den), 0.0)
        dS = _dot(dn, vv, NT) + dd
        dqk = dS * wi
        dq_i = wn * (_dot(dn, Cv, NT) + dd * nv)
        dq = _dot(dqk, kv, NN) + dq_i
        dv = _dot(s, dn, TN)
        m_new, decay, wk_col = _chunk_state(li_col, li_row, b_col, b_row, mv, L)
        dCv, dnvv = dC[...], dnv[...]
        dk_i = wk_col * (_dot(vv, dCv, NT) + dnvv)
        dk = _dot(dqk, qv, TN) + dk_i
        dv = dv + _dot(kv * wk_col, dCv, NN)
        qw = qv * wn
        dC[...] = decay * dCv + _dot(qw, dn, TN)
        dnv[...] = decay * dnvv + jnp.sum(qw * dd, axis=0, keepdims=True)
        dq_ref[...] = dq
        dk_ref[...] = dk
        dv_ref[...] = dv.astype(BF16)
        G = dS * s
        t = lax.broadcasted_iota(jnp.int32, (L, L), 0)
        r = lax.broadcasted_iota(jnp.int32, (L, L), 1)
        eye = t == r
        col_row = jnp.sum(G, axis=0, keepdims=True)
        col_g = jnp.sum(jnp.where(eye, col_row, 0.0), axis=1, keepdims=True)
        row_i = jnp.sum(qv * dq_i, axis=1, keepdims=True)
        col_i = jnp.sum(kv * dk_i, axis=1, keepdims=True)
        dF = jnp.sum(G, axis=1, keepdims=True) - col_g + row_i
        across = decay * (jnp.sum(jnp.sum(dCv * Cv.astype(F32), axis=0, keepdims=True), axis=1, keepdims=True)
                          + jnp.sum(dnvv * nv, axis=1, keepdims=True))
        dlf = jnp.sum(jnp.where(t >= r, dF, 0.0) + jnp.where(t < r, col_i, 0.0), axis=0, keepdims=True) + across
        dli = col_row + jnp.sum(jnp.where(eye, col_i, 0.0), axis=0, keepdims=True)
        sub = lax.broadcasted_iota(jnp.int32, (8, L), 0)
        dg_ref[...] = jnp.where(sub == 0, dli, jnp.where(sub == 1, dlf, 0.0))

    blk = lambda c0: pl.BlockSpec((L, dh), lambda h, j: (nc - 1 - j, c0 + h))
    st = lambda r, c: pl.BlockSpec((None, None, r, c), lambda h, j: (h, nc - 1 - j, 0, 0))
    return _call(body, name="mlstm_bwd", grid=(H, nc),
                 in_specs=[blk(0), blk(0), blk(0), blk(vcol * H), pl.BlockSpec((L, 8), lambda h, j: (nc - 1 - j, 0)),
                           pl.BlockSpec((8, L), lambda h, j: (0, nc - 1 - j)), st(dh, dh), st(1, dh), st(1, 128)],
                 out_specs=[blk(0), blk(0), blk(0), pl.BlockSpec((None, 8, L), lambda h, j: (h, 0, nc - 1 - j))],
                 out_shape=[_sds((S, D), F32), _sds((S, D), F32), _sds((S, D), BF16), _sds((H, 8, S), F32)],
                 scratch_shapes=[pltpu.VMEM((dh, dh), F32), pltpu.VMEM((1, dh), F32)],
                 compiler_params=_cp(("arbitrary", "arbitrary")))(dhm, q, k, P, g8, gt, cs, ns, ms)


def _ml_post_fwd(hm, P, g, D, ocol, zcol):
    S = hm.shape[0]
    tm = min(ROW_TILE, S)
    dh = D // N_HEADS

    def body(h_ref, o_ref, z_ref, g_ref, cb_ref):
        for hd in range(N_HEADS):
            sl = slice(hd * dh, (hd + 1) * dh)
            x = _sg(o_ref[:, sl]) * h_ref[:, sl]
            r = lax.rsqrt(_mean(x * x) + EPS)
            z = z_ref[:, sl]
            cb_ref[:, sl] = (x * r * g_ref[:, sl] * (z * _sg(z))).astype(BF16)

    return _call(body, name="ml_post_fwd", grid=(S // tm,),
                 in_specs=[_row(tm, D), _row(tm, D, ocol), _row(tm, D, zcol), _par(1, D)], out_specs=_row(tm, D),
                 out_shape=_sds((S, D), BF16), compiler_params=_cp(("parallel",)))(hm, P, P, g)


def _ml_post_bwd(d_cb, hm, P, g, D, ocol, zcol):
    S = hm.shape[0]
    tm = min(ROW_TILE, S)
    dh = D // N_HEADS

    def body(d_ref, h_ref, o_ref, z_ref, g_ref, dh_ref, do_ref, dz_ref, dg_ref):
        @pl.when(pl.program_id(0) == 0)
        def _():
            dg_ref[...] = jnp.zeros_like(dg_ref)

        for hd in range(N_HEADS):
            sl = slice(hd * dh, (hd + 1) * dh)
            hv, z, gv, d = h_ref[:, sl], z_ref[:, sl], g_ref[:, sl], d_ref[:, sl]
            so, sz = _sg(o_ref[:, sl]), _sg(z)
            x = so * hv
            r = lax.rsqrt(_mean(x * x) + EPS)
            xh = x * r
            d3 = d * (z * sz)
            dz_ref[:, sl] = (d * xh * gv * _dsilu(z, sz)).astype(BF16)
            dg_ref[:, sl] += jnp.sum(d3 * xh, axis=0, keepdims=True)
            dxh = d3 * gv
            d2 = r * (dxh - xh * _mean(dxh * xh))
            do_ref[:, sl] = (d2 * hv * so * (1.0 - so)).astype(BF16)
            dh_ref[:, sl] = d2 * so

    return _call(body, name="ml_post_bwd", grid=(S // tm,),
                 in_specs=[_row(tm, D), _row(tm, D), _row(tm, D, ocol), _row(tm, D, zcol), _par(1, D)],
                 out_specs=[_row(tm, D), _row(tm, D), _row(tm, D), _par(1, D)],
                 out_shape=[_sds((S, D), F32), _sds((S, D), BF16), _sds((S, D), BF16), _sds((1, D), F32)],
                 compiler_params=_cp(("arbitrary",)))(d_cb, hm, P, P, g)


def _attn_scores(q, kh, scale):
    sc = _dot(q, kh, NT) * scale
    ex = jnp.exp(sc - jnp.max(sc, axis=1, keepdims=True))
    return ex / jnp.sum(ex, axis=1, keepdims=True)


def _attn_fwd(P, kv, D, qcol, zcol):
    S, M = P.shape[0], kv.shape[0]
    tm = min(ROW_TILE, S)
    dh = D // N_HEADS
    scale = float(dh) ** -0.5

    def body(q_ref, z_ref, kv_ref, cc_ref):
        for hd in range(N_HEADS):
            sl = slice(hd * dh, (hd + 1) * dh)
            p = _attn_scores(q_ref[:, sl], kv_ref[:, sl], scale)
            o = _dot(p, kv_ref[:, D + hd * dh:D + (hd + 1) * dh], NN)
            z = z_ref[:, sl]
            cc_ref[:, sl] = (o * (z * _sg(z))).astype(BF16)

    return _call(body, name="attn_fwd", grid=(S // tm,),
                 in_specs=[_row(tm, D, qcol), _row(tm, D, zcol), _par(M, 2 * D)], out_specs=_row(tm, D),
                 out_shape=_sds((S, D), BF16), compiler_params=_cp(("parallel",)))(P, P, kv)


def _attn_bwd(d_cc, P, kv, D, qcol, zcol):
    S, M = P.shape[0], kv.shape[0]
    tm = min(ROW_TILE, S)
    dh = D // N_HEADS
    scale = float(dh) ** -0.5

    def body(d_ref, q_ref, z_ref, kv_ref, dq_ref, dz_ref, dkv_ref):
        @pl.when(pl.program_id(0) == 0)
        def _():
            dkv_ref[...] = jnp.zeros_like(dkv_ref)

        for hd in range(N_HEADS):
            sl = slice(hd * dh, (hd + 1) * dh)
            vl = slice(D + hd * dh, D + (hd + 1) * dh)
            q, kh, vh = q_ref[:, sl], kv_ref[:, sl], kv_ref[:, vl]
            p = _attn_scores(q, kh, scale)
            o = _dot(p, vh, NN)
            z, d = z_ref[:, sl], d_ref[:, sl]
            sz = _sg(z)
            do = d * (z * sz)
            dz_ref[:, sl] = (d * o * _dsilu(z, sz)).astype(BF16)
            dp = _dot(do, vh, NT)
            ds = p * (dp - jnp.sum(p * dp, axis=1, keepdims=True)) * scale
            dq_ref[:, sl] = _dot(ds, kh, NN).astype(BF16)
            dkv_ref[:, sl] += _dot(ds, q, TN)
            dkv_ref[:, vl] += _dot(p, do, TN)

    return _call(body, name="attn_bwd", grid=(S // tm,),
                 in_specs=[_row(tm, D), _row(tm, D, qcol), _row(tm, D, zcol), _par(M, 2 * D)],
                 out_specs=[_row(tm, D), _row(tm, D), _par(M, 2 * D)],
                 out_shape=[_sds((S, D), BF16), _sds((S, D), BF16), _sds((M, 2 * D), F32)],
                 compiler_params=_cp(("arbitrary",)))(d_cc, P, P, kv)


def _merge_fwd(yc, ym, yx, P, D, gcol):
    S = yc.shape[0]
    tm = min(ROW_TILE, S)

    def body(c_ref, m_ref, x_ref, gc_ref, gm_ref, gx_ref, o_ref):
        o_ref[...] = (_sg(gc_ref[...]) * c_ref[...] + _sg(gm_ref[...]) * m_ref[...]
                      + _sg(gx_ref[...]) * x_ref[...]).astype(BF16)

    return _call(body, name="merge_fwd", grid=(S // tm,),
                 in_specs=[_row(tm, D)] * 3 + [_row(tm, D, gcol), _row(tm, D, gcol + 1), _row(tm, D, gcol + 2)],
                 out_specs=_row(tm, D), out_shape=_sds((S, D), BF16), compiler_params=_cp(("parallel",)))(yc, ym, yx, P, P, P)


def _merge_bwd(dm, yc, ym, yx, P, D, gcol):
    S = yc.shape[0]
    tm = min(ROW_TILE, S)

    def body(d_ref, c_ref, m_ref, x_ref, gc_ref, gm_ref, gx_ref, dc_ref, dmm_ref, dx_ref, dgc_ref, dgm_ref, dgx_ref):
        d = d_ref[...]
        for y_ref, g_ref, dy_ref, dg_ref in ((c_ref, gc_ref, dc_ref, dgc_ref), (m_ref, gm_ref, dmm_ref, dgm_ref),
                                             (x_ref, gx_ref, dx_ref, dgx_ref)):
            s = _sg(g_ref[...])
            dy_ref[...] = (d * s).astype(BF16)
            dg_ref[...] = (d * y_ref[...] * s * (1.0 - s)).astype(BF16)

    return _call(body, name="merge_bwd", grid=(S // tm,),
                 in_specs=[_row(tm, D)] * 4 + [_row(tm, D, gcol), _row(tm, D, gcol + 1), _row(tm, D, gcol + 2)],
                 out_specs=[_row(tm, D)] * 6, out_shape=[_sds((S, D), BF16)] * 6,
                 compiler_params=_cp(("parallel",)))(dm, yc, ym, yx, P, P, P)


def _post(o, x, tgt, g):
    S, D = x.shape
    tm = min(ROW_TILE, S)

    def body(o_ref, x_ref, t_ref, g_ref, dy_ref, do_ref, dg_ref, l_ref):
        ov, gv = o_ref[...], g_ref[...]
        r = lax.rsqrt(_mean(ov * ov) + EPS)
        e = x_ref[...] + ov * r * gv - t_ref[...]
        dy = e / D
        dy_ref[...] = dy
        dxh = dy * gv
        do_ref[...] = (r * dxh - ov * (r * r * r) * _mean(dxh * ov)).astype(BF16)

        @pl.when(pl.program_id(0) == 0)
        def _():
            dg_ref[...] = jnp.zeros_like(dg_ref)
            l_ref[...] = jnp.zeros_like(l_ref)

        dg_ref[...] += jnp.sum(dy * ov * r, axis=0, keepdims=True)
        l_ref[...] += jnp.sum(e * e, axis=0, keepdims=True)

    return _call(body, name="post", grid=(S // tm,), in_specs=[_row(tm, D)] * 3 + [_par(1, D)],
                 out_specs=[_row(tm, D), _row(tm, D), _par(1, D), _par(1, D)],
                 out_shape=[_sds((S, D), F32), _sds((S, D), BF16), _sds((1, D), F32), _sds((1, D), F32)],
                 compiler_params=_cp(("arbitrary",)))(o, x, tgt, g)


def _chip_peers():
    x, y, c = lax.axis_index("x"), lax.axis_index("y"), lax.axis_index("c")
    return x, y, c, [(1 - x, y), (x, 1 - y), (1 - x, 1 - y)]


def _gather_chips(shards):
    T = len(shards)

    def body(*refs):
        ins, outs = refs[:T], refs[T:2 * T]
        send, recv, loc = refs[2 * T:]
        x, y, c, chips = _chip_peers()
        mine = 2 * x + y
        copies = []
        for t in range(T):
            lc = pltpu.make_async_copy(ins[t], outs[t].at[mine], loc.at[t])
            lc.start()
            copies.append(lc)
        sends = []
        for t in range(T):
            for j, (px, py) in enumerate(chips):
                cp = pltpu.make_async_remote_copy(src_ref=ins[t], dst_ref=outs[t].at[mine], send_sem=send.at[t, j],
                                                  recv_sem=recv.at[t, j], device_id=(px, py, c), device_id_type=MESH)
                cp.start()
                sends.append(cp)
        for t in range(T):
            for j, (px, py) in enumerate(chips):
                pltpu.make_async_remote_copy(src_ref=ins[t], dst_ref=outs[t].at[2 * px + py], send_sem=send.at[t, j],
                                             recv_sem=recv.at[t, j], device_id=(px, py, c), device_id_type=MESH).wait_recv()
        for cp in sends:
            cp.wait_send()
        for lc in copies:
            lc.wait()

    anyspec = pl.BlockSpec(memory_space=pl.ANY)
    return _call(body, name="gather_weights", in_specs=[anyspec] * T, out_specs=[anyspec] * T,
                 out_shape=[_sds((4, *s.shape), s.dtype) for s in shards],
                 scratch_shapes=[pltpu.SemaphoreType.DMA((T, 3)), pltpu.SemaphoreType.DMA((T, 3)), pltpu.SemaphoreType.DMA((T,))],
                 compiler_params=pltpu.CompilerParams(has_side_effects=True))(*shards)


def _scatter_grads(blocks, small):
    T = len(blocks)

    def body(*refs):
        ins, sm = refs[:T], refs[T]
        outs, smo = refs[T + 1:2 * T + 1], refs[2 * T + 1]
        send, recv, loc = refs[2 * T + 2:]
        x, y, c = lax.axis_index("x"), lax.axis_index("y"), lax.axis_index("c")
        me = 4 * x + 2 * y + c
        peers = [(x ^ ((k >> 2) & 1), y ^ ((k >> 1) & 1), c ^ (k & 1)) for k in range(1, N_DEV)]
        copies = []
        for t in range(T):
            lc = pltpu.make_async_copy(ins[t].at[2 * x + y], outs[t].at[me], loc.at[t])
            lc.start()
            copies.append(lc)
        lc = pltpu.make_async_copy(sm, smo.at[me], loc.at[T])
        lc.start()
        copies.append(lc)

        def remote(t, j, peer, sender):
            px, py, pc = peer
            src = sm if t == T else ins[t].at[2 * px + py]
            dst = (smo if t == T else outs[t]).at[sender]
            return pltpu.make_async_remote_copy(src_ref=src, dst_ref=dst, send_sem=send.at[t, j], recv_sem=recv.at[t, j],
                                                device_id=peer, device_id_type=MESH)

        sends = []
        for t in range(T + 1):
            for j, peer in enumerate(peers):
                cp = remote(t, j, peer, me)
                cp.start()
                sends.append(cp)
        for t in range(T + 1):
            for j, (px, py, pc) in enumerate(peers):
                remote(t, j, (x, y, c), 4 * px + 2 * py + pc).wait_recv()
        for cp in sends:
            cp.wait_send()
        for lc in copies:
            lc.wait()

    anyspec = pl.BlockSpec(memory_space=pl.ANY)
    return _call(body, name="scatter_grads", in_specs=[anyspec] * (T + 1), out_specs=[anyspec] * (T + 1),
                 out_shape=[_sds((N_DEV, *b.shape[1:]), b.dtype) for b in blocks] + [_sds((N_DEV, *small.shape), small.dtype)],
                 scratch_shapes=[pltpu.SemaphoreType.DMA((T + 1, N_DEV - 1)), pltpu.SemaphoreType.DMA((T + 1, N_DEV - 1)),
                                 pltpu.SemaphoreType.DMA((T + 1,))],
                 compiler_params=pltpu.CompilerParams(has_side_effects=True))(*blocks, small)


def _adamw(land, w, m, v, name):
    R, C = w.shape
    tr = R if R * C * 4 <= (1 << 20) else max(16, ((1 << 20) // (C * 4)) // 16 * 16)
    while R % tr:
        tr -= 16
    c1, c2 = 1.0 - ADAM_B1 ** ADAM_STEP, 1.0 - ADAM_B2 ** ADAM_STEP

    def body(l_ref, w_ref, m_ref, v_ref, g_ref, d_ref, mo_ref, vo_ref):
        g = l_ref[0].astype(F32)
        for k in range(1, N_DEV):
            g = g + l_ref[k].astype(F32)
        mn = ADAM_B1 * m_ref[...] + (1.0 - ADAM_B1) * g
        vn = ADAM_B2 * v_ref[...] + (1.0 - ADAM_B2) * (g * g)
        g_ref[...] = g
        mo_ref[...] = mn
        vo_ref[...] = vn
        d_ref[...] = -ADAM_LR * ((mn / c1) / (jnp.sqrt(vn / c2) + ADAM_EPS) + ADAM_WD * w_ref[...])

    blk = pl.BlockSpec((tr, C), lambda i: (i, 0))
    return _call(body, name=name, grid=(R // tr,), in_specs=[pl.BlockSpec((N_DEV, tr, C), lambda i: (0, i, 0)), blk, blk, blk],
                 out_specs=[blk] * 4, out_shape=[_sds((R, C), F32)] * 4, compiler_params=_cp(("parallel",)))(land, w, m, v)


def _pad_rows(a, rows):
    return jnp.pad(a, ((0, rows - a.shape[0]), (0, 0)))


def _local_step(x, mem, tgt, g_pre, w_main, w_if, b_if, wq, wk, w_dw, b_dw, g_ln, b_ln, w_conv_out, g_ml_head, w_ml_out,
                g_mem, w_mem_kv, w_xa_out, w_out, g_post):
    S, D = x.shape
    row = lambda a: a.reshape(1, -1)
    bif = jnp.pad(b_if, (0, 128 - b_if.shape[0])).reshape(1, 128)
    wdw, wq8, wk8 = _pad_rows(w_dw, HALO), _pad_rows(wq, QK_HALO), _pad_rows(wk, QK_HALO)

    h = _rms_fwd(x, row(g_pre), "rms_pre")
    P = _mm(h, w_main, "nn", F32, "proj_in")
    gif = _mm(h, w_if, "nn", F32, "proj_if")
    ca, u1 = _conv_fwd(P, wdw, row(b_dw), row(g_ln), row(b_ln), D)
    q, k, cq, ck = _qk_fwd(P, wq8, wk8, D, 3)
    gl = _gates_fwd(gif, bif)
    g8 = gl[:, :8]
    gt = g8.T
    hm, cs, ns, ms = _mlstm_fwd(q, k, P, g8, gt, D, 5)
    cb = _ml_post_fwd(hm, P, row(g_ml_head), D, 6, 7)
    mn = _rms_fwd(mem, row(g_mem), "rms_mem")
    kv = _mm(mn, w_mem_kv, "nn", BF16, "proj_kv")
    cc = _attn_fwd(P, kv, D, 8, 9)
    yc = _mm(ca, w_conv_out, "nn", F32, "out_conv")
    ym = _mm(cb, w_ml_out, "nn", F32, "out_ml")
    yx = _mm(cc, w_xa_out, "nn", F32, "out_xa")
    merged = _merge_fwd(yc, ym, yx, P, D, 10)
    o = _mm(merged, w_out, "nn", F32, "out_proj")
    dy, d_o, gg_post, sq = _post(o, x, tgt, row(g_post))

    d_merged = _mm(d_o, w_out, "nt", F32, "d_merged")
    gw_out = _mm(merged, d_o, "tn", F32, "gw_out")
    d_yc, d_ym, d_yx, d_gc, d_gm, d_gx = _merge_bwd(d_merged, yc, ym, yx, P, D, 10)
    d_ca = _mm(d_yc, w_conv_out, "nt", F32, "d_ca")
    gw_conv_out = _mm(ca, d_yc, "tn", F32, "gw_conv_out")
    d_cb = _mm(d_ym, w_ml_out, "nt", F32, "d_cb")
    gw_ml_out = _mm(cb, d_ym, "tn", F32, "gw_ml_out")
    d_cc = _mm(d_yx, w_xa_out, "nt", F32, "d_cc")
    gw_xa_out = _mm(cc, d_yx, "tn", F32, "gw_xa_out")

    d_u1, d_zc, gg_ln, gb_ln = _conv_bwd_local(d_ca, u1, P, row(g_ln), row(b_ln), D)
    d_a, d_b, gw_dw, gb_dw = _conv_bwd_taps(d_u1, P, wdw, D)

    d_qx, d_zx, d_kv = _attn_bwd(d_cc, P, kv, D, 8, 9)
    d_kvb = d_kv.astype(BF16)
    gw_mem_kv = _mm(mn, d_kvb, "tn", F32, "gw_mem_kv")
    d_mn = _mm(d_kvb, w_mem_kv, "nt", F32, "d_mn")
    _, gg_mem = _rms_bwd(mem, row(g_mem), [d_mn], None, "rms_mem_bwd")

    d_hm, d_om, d_zm, gg_ml = _ml_post_bwd(d_cb, hm, P, row(g_ml_head), D, 6, 7)
    dq, dk, d_v, dg = _mlstm_bwd(d_hm, q, k, P, g8, gt, cs, ns, ms, D, 5)
    d_qp, d_kp, gwq, gwk = _qk_bwd(dq, dk, cq, ck, P, wq8, wk8, D, 3)
    dgl = jnp.pad(jnp.concatenate([dg[:, 0, :].T, dg[:, 1, :].T], axis=1), ((0, 0), (0, 128 - 2 * N_HEADS)))
    d_gif, gb_if = _gates_bwd(dgl, gif, bif)

    dP = jnp.concatenate([d_a, d_b, d_zc, d_qp, d_kp, d_v, d_om, d_zm, d_qx, d_zx, d_gc, d_gm, d_gx], axis=1)
    d_h1 = _mm(dP, w_main, "nt", F32, "d_h_main")
    d_h2 = _mm(d_gif, w_if, "nt", F32, "d_h_if")
    gw_main = _mm(h, dP, "tn", F32, "gw_main")
    gw_if = _mm(h, d_gif, "tn", F32, "gw_if")
    grad_x, gg_pre = _rms_bwd(x, row(g_pre), [d_h1, d_h2], dy, "rms_pre_bwd")

    grads = dict(g_pre=gg_pre[0], w_main=gw_main, w_if=gw_if, b_if=gb_if[0, :2 * N_HEADS], wq=gwq[:QK_W], wk=gwk[:QK_W],
                 w_dw=gw_dw[:CONV_W], b_dw=gb_dw[0], g_ln=gg_ln[0], b_ln=gb_ln[0], w_conv_out=gw_conv_out,
                 g_ml_head=gg_ml[0], w_ml_out=gw_ml_out, g_mem=gg_mem[0], w_mem_kv=gw_mem_kv, w_xa_out=gw_xa_out,
                 w_out=gw_out, g_post=gg_post[0])
    return jnp.sum(sq), grad_x, grads


SMALL = ("g_pre", "b_if", "b_dw", "g_ln", "b_ln", "g_ml_head", "g_mem", "g_post")


def kernel(x, mem, g_pre, w_in, b_if, w_qk_conv, w_dw, b_dw, g_ln, b_ln, w_conv_out, g_ml_head, w_ml_out, g_mem, w_mem_kv, w_xa_out, w_out, g_post, loss_target, m_g_pre, m_w_in, m_b_if, m_w_qk_conv, m_w_dw, m_b_dw, m_g_ln, m_b_ln, m_w_conv_out, m_g_ml_head, m_w_ml_out, m_g_mem, m_w_mem_kv, m_w_xa_out, m_w_out, m_g_post, v_g_pre, v_w_in, v_b_if, v_w_qk_conv, v_w_dw, v_b_dw, v_g_ln, v_b_ln, v_w_conv_out, v_g_ml_head, v_w_ml_out, v_g_mem, v_w_mem_kv, v_w_xa_out, v_w_out, v_g_post):
    W = dict(g_pre=g_pre, w_in=w_in, b_if=b_if, w_qk_conv=w_qk_conv, w_dw=w_dw, b_dw=b_dw, g_ln=g_ln, b_ln=b_ln,
             w_conv_out=w_conv_out, g_ml_head=g_ml_head, w_ml_out=w_ml_out, g_mem=g_mem, w_mem_kv=w_mem_kv,
             w_xa_out=w_xa_out, w_out=w_out, g_post=g_post)
    Mo = dict(g_pre=m_g_pre, w_in=m_w_in, b_if=m_b_if, w_qk_conv=m_w_qk_conv, w_dw=m_w_dw, b_dw=m_b_dw, g_ln=m_g_ln,
              b_ln=m_b_ln, w_conv_out=m_w_conv_out, g_ml_head=m_g_ml_head, w_ml_out=m_w_ml_out, g_mem=m_g_mem,
              w_mem_kv=m_w_mem_kv, w_xa_out=m_w_xa_out, w_out=m_w_out, g_post=m_g_post)
    Vo = dict(g_pre=v_g_pre, w_in=v_w_in, b_if=v_b_if, w_qk_conv=v_w_qk_conv, w_dw=v_w_dw, b_dw=v_b_dw, g_ln=v_g_ln,
              b_ln=v_b_ln, w_conv_out=v_w_conv_out, g_ml_head=v_g_ml_head, w_ml_out=v_w_ml_out, g_mem=v_g_mem,
              w_mem_kv=v_w_mem_kv, w_xa_out=v_w_xa_out, w_out=v_w_out, g_post=v_g_post)
    D = x.shape[-1]
    n_in = 4 * w_in.shape[1]
    off_if = 8 * D

    big = ("w_in", "w_conv_out", "w_ml_out", "w_mem_kv", "w_xa_out", "w_out")
    got = _gather_chips([W[n].astype(BF16) for n in big] + [w_qk_conv, w_dw])
    cols = lambda a: jnp.transpose(a, (1, 0, 2)).reshape(a.shape[1], -1)
    rows = lambda a: a.reshape(-1, a.shape[2])
    w_in_f = cols(got[0])
    w_main = jnp.concatenate([w_in_f[:, :off_if], w_in_f[:, off_if + 2 * N_HEADS:]], axis=1)
    w_if = jnp.pad(w_in_f[:, off_if:off_if + 2 * N_HEADS], ((0, 0), (0, 128 - 2 * N_HEADS)))
    wqk_f = cols(got[6])

    sq, grad_x, G = _local_step(
        x[0], mem[0], loss_target[0], g_pre, w_main, w_if, b_if, wqk_f[:, :D], wqk_f[:, D:], cols(got[7]), b_dw, g_ln, b_ln,
        rows(got[1]), g_ml_head, rows(got[2]), g_mem, cols(got[3]), rows(got[4]), rows(got[5]), g_post)
    loss = lax.psum(0.5 * sq / D, ("x", "y", "c"))

    colblk = lambda a: jnp.transpose(a.reshape(a.shape[0], 4, -1), (1, 0, 2))
    rowblk = lambda a: a.reshape(4, -1, a.shape[1])
    gw_in = jnp.concatenate([G["w_main"][:, :off_if], G["w_if"][:, :2 * N_HEADS], G["w_main"][:, off_if:]], axis=1)
    assert gw_in.shape[1] == n_in
    blocks = dict(w_in=colblk(gw_in).astype(BF16), w_conv_out=rowblk(G["w_conv_out"]).astype(BF16),
                  w_ml_out=rowblk(G["w_ml_out"]).astype(BF16), w_mem_kv=colblk(G["w_mem_kv"]).astype(BF16),
                  w_xa_out=rowblk(G["w_xa_out"]).astype(BF16), w_out=rowblk(G["w_out"]).astype(BF16),
                  w_qk_conv=colblk(jnp.concatenate([G["wq"], G["wk"]], axis=1)), w_dw=colblk(G["w_dw"]))
    sharded = tuple(blocks)
    pad = lambda a: jnp.pad(a, (0, D - a.shape[0]))
    small_g = jnp.stack([pad(G[n]) for n in SMALL])
    lands = _scatter_grads([blocks[n] for n in sharded], small_g)

    out = {}
    for n, land in zip(sharded, lands[:-1]):
        out[n] = _adamw(land, W[n], Mo[n], Vo[n], "adamw_" + n)
    res = _adamw(lands[-1], jnp.stack([pad(W[n]) for n in SMALL]), jnp.stack([pad(Mo[n]) for n in SMALL]),
                 jnp.stack([pad(Vo[n]) for n in SMALL]), "adamw_small")
    for i, n in enumerate(SMALL):
        out[n] = tuple(r[i, :W[n].shape[0]] for r in res)
    order = ("g_pre", "w_in", "b_if", "w_qk_conv", "w_dw", "b_dw", "g_ln", "b_ln", "w_conv_out", "g_ml_head", "w_ml_out",
             "g_mem", "w_mem_kv", "w_xa_out", "w_out", "g_post")
    return (loss, grad_x[None], *[out[n][0] for n in order], *[out[n][1] for n in order], *[out[n][2] for n in order],
            *[out[n][3] for n in order])
```

```python
import jax
import jax.numpy as jnp
from jax import lax
from jax.experimental import pallas as pl
from jax.experimental.pallas import tpu as pltpu

F32, BF16 = jnp.float32, jnp.bfloat16
EPS = 1e-6
N_HEADS = 4
CONV_W = 31
QK_W = 4
HALO = 32
QK_HALO = 8
ROW_TILE = 256
ML_CHUNK = 256
VMEM_LIMIT = 56 * 1024 * 1024
NEG = -1e30
ADAM_LR, ADAM_B1, ADAM_B2, ADAM_EPS, ADAM_WD, ADAM_STEP = 0.001, 0.9, 0.999, 1e-08, 0.01, 10
MESH = pl.DeviceIdType.MESH
N_DEV = 8

NN = (((1,), (0,)), ((), ()))
NT = (((1,), (1,)), ((), ()))
TN = (((0,), (0,)), ((), ()))


def _call(body, **kw):
    return pl.pallas_call(body, **kw)


def _cp(sem):
    return pltpu.CompilerParams(dimension_semantics=sem, vmem_limit_bytes=VMEM_LIMIT)


def _sds(shape, dt):
    return jax.ShapeDtypeStruct(shape, dt)


def _row(tm, d, col=0):
    return pl.BlockSpec((tm, d), lambda i: (i, col))


def _par(r, d):
    return pl.BlockSpec((r, d), lambda i: (0, 0))


def _prev(tm, hb, d, col=0):
    return pl.BlockSpec((hb, d), lambda i: (jnp.maximum(i * (tm // hb) - 1, 0), col))


def _next(tm, hb, d, nblk, col=0):
    return pl.BlockSpec((hb, d), lambda i: (jnp.minimum((i + 1) * (tm // hb), nblk - 1), col))


def _dot(a, b, dn):
    return lax.dot_general(a.astype(BF16), b.astype(BF16), dn, preferred_element_type=F32)


def _sg(x):
    return jax.nn.sigmoid(x)


def _dsilu(z, s):
    return s * (1.0 + z * (1.0 - s))


def _mean(x):
    return jnp.mean(x, axis=-1, keepdims=True)


def _pick(n, pref):
    if n <= pref:
        return n
    t = pref
    while n % t:
        t -= 128
    return t


def _mm(a, b, mode, out_dtype, name, tm=1024, tn=1024, tk=2048):
    if mode == "nn":
        (M, K), N = a.shape, b.shape[1]
    elif mode == "nt":
        (M, K), N = a.shape, b.shape[0]
    else:
        (K, M), N = a.shape, b.shape[1]
    tm, tn, tk = _pick(M, tm), _pick(N, tn), _pick(K, tk)
    nk = K // tk
    if mode == "nn":
        sa, sb, dn = pl.BlockSpec((tm, tk), lambda i, j, k: (i, k)), pl.BlockSpec((tk, tn), lambda i, j, k: (k, j)), NN
    elif mode == "nt":
        sa, sb, dn = pl.BlockSpec((tm, tk), lambda i, j, k: (i, k)), pl.BlockSpec((tn, tk), lambda i, j, k: (j, k)), NT
    else:
        sa, sb, dn = pl.BlockSpec((tk, tm), lambda i, j, k: (k, i)), pl.BlockSpec((tk, tn), lambda i, j, k: (k, j)), TN

    def body(a_ref, b_ref, o_ref, *acc):
        p = _dot(a_ref[...], b_ref[...], dn)
        if nk == 1:
            o_ref[...] = p.astype(out_dtype)
        else:
            acc_ref, k = acc[0], pl.program_id(2)

            @pl.when(k == 0)
            def _():
                acc_ref[...] = p

            @pl.when(k > 0)
            def _():
                acc_ref[...] += p

            @pl.when(k == nk - 1)
            def _():
                o_ref[...] = acc_ref[...].astype(out_dtype)

    return _call(body, name=name, grid=(M // tm, N // tn, nk), in_specs=[sa, sb],
                 out_specs=pl.BlockSpec((tm, tn), lambda i, j, k: (i, j)), out_shape=_sds((M, N), out_dtype),
                 scratch_shapes=[pltpu.VMEM((tm, tn), F32)] if nk > 1 else [],
                 compiler_params=_cp(("parallel", "parallel", "arbitrary")))(a, b)


def _rms_fwd(x, g, name):
    S, D = x.shape
    tm = min(ROW_TILE, S)

    def body(x_ref, g_ref, h_ref):
        xv = x_ref[...]
        r = lax.rsqrt(_mean(xv * xv) + EPS)
        h_ref[...] = (xv * r * g_ref[...]).astype(BF16)

    return _call(body, name=name, grid=(S // tm,), in_specs=[_row(tm, D), _par(1, D)], out_specs=_row(tm, D),
                 out_shape=_sds((S, D), BF16), compiler_params=_cp(("parallel",)))(x, g)


def _rms_bwd(x, g, ds, resid, name):
    S, D = x.shape
    tm = min(ROW_TILE, S)
    nd = len(ds)

    def body(*refs):
        x_ref, g_ref = refs[:2]
        d_refs = refs[2:2 + nd]
        r_ref = refs[2 + nd] if resid is not None else None
        dx_ref, dg_ref = refs[-2:]
        xv = x_ref[...]
        d = d_refs[0][...]
        for dr in d_refs[1:]:
            d = d + dr[...]
        r = lax.rsqrt(_mean(xv * xv) + EPS)
        dxh = d * g_ref[...]
        dx = r * dxh - xv * (r * r * r) * _mean(dxh * xv)
        if r_ref is not None:
            dx = dx + r_ref[...]
        dx_ref[...] = dx

        @pl.when(pl.program_id(0) == 0)
        def _():
            dg_ref[...] = jnp.zeros_like(dg_ref)

        dg_ref[...] += jnp.sum(d * xv * r, axis=0, keepdims=True)

    ins = [x, g, *ds] + ([resid] if resid is not None else [])
    specs = [_row(tm, D), _par(1, D)] + [_row(tm, D)] * (len(ins) - 2)
    return _call(body, name=name, grid=(S // tm,), in_specs=specs, out_specs=[_row(tm, D), _par(1, D)],
                 out_shape=[_sds((S, D), F32), _sds((1, D), F32)], compiler_params=_cp(("arbitrary",)))(*ins)


def _conv_fwd(P, w_dw, b_dw, g_ln, b_ln, D):
    S = P.shape[0]
    tm = min(ROW_TILE, S)

    def body(a_ref, b_ref, ap_ref, bp_ref, z_ref, w_ref, bd_ref, g_ref, bl_ref, ca_ref, u1_ref, buf):
        first = pl.program_id(0) == 0
        hp = ap_ref[...] * _sg(bp_ref[...])
        buf[0:HALO, :] = jnp.where(first, 0.0, hp)
        buf[HALO:HALO + tm, :] = a_ref[...] * _sg(b_ref[...])
        acc = jnp.zeros((tm, D), F32)
        for j in range(CONV_W):
            acc = acc + w_ref[j:j + 1, :] * buf[HALO - CONV_W + 1 + j:HALO - CONV_W + 1 + j + tm, :]
        u1 = acc + bd_ref[...]
        u1_ref[...] = u1
        xc = u1 - _mean(u1)
        r = lax.rsqrt(_mean(xc * xc) + EPS)
        u2 = xc * r * g_ref[...] + bl_ref[...]
        z = z_ref[...]
        ca_ref[...] = (u2 * _sg(u2) * (z * _sg(z))).astype(BF16)

    return _call(body, name="conv_fwd", grid=(S // tm,),
                 in_specs=[_row(tm, D, 0), _row(tm, D, 1), _prev(tm, HALO, D, 0), _prev(tm, HALO, D, 1), _row(tm, D, 2),
                           _par(HALO, D), _par(1, D), _par(1, D), _par(1, D)],
                 out_specs=[_row(tm, D), _row(tm, D)], out_shape=[_sds((S, D), BF16), _sds((S, D), F32)],
                 scratch_shapes=[pltpu.VMEM((tm + HALO, D), F32)], compiler_params=_cp(("parallel",)))(
        P, P, P, P, P, w_dw, b_dw, g_ln, b_ln)


def _conv_bwd_local(d_ca, u1, P, g_ln, b_ln, D):
    S = P.shape[0]
    tm = min(ROW_TILE, S)

    def body(dca_ref, u1_ref, z_ref, g_ref, bl_ref, du1_ref, dz_ref, dg_ref, db_ref):
        u1, z, dca, g = u1_ref[...], z_ref[...], dca_ref[...], g_ref[...]
        xc = u1 - _mean(u1)
        r = lax.rsqrt(_mean(xc * xc) + EPS)
        xh = xc * r
        u2 = xh * g + bl_ref[...]
        s2, sz = _sg(u2), _sg(z)
        d_u3 = dca * (z * sz)
        dz_ref[...] = (dca * (u2 * s2) * _dsilu(z, sz)).astype(BF16)
        d_u2 = d_u3 * _dsilu(u2, s2)
        dxh = d_u2 * g
        du1_ref[...] = r * (dxh - _mean(dxh) - xh * _mean(dxh * xh))

        @pl.when(pl.program_id(0) == 0)
        def _():
            dg_ref[...] = jnp.zeros_like(dg_ref)
            db_ref[...] = jnp.zeros_like(db_ref)

        dg_ref[...] += jnp.sum(d_u2 * xh, axis=0, keepdims=True)
        db_ref[...] += jnp.sum(d_u2, axis=0, keepdims=True)

    return _call(body, name="conv_bwd_local", grid=(S // tm,),
                 in_specs=[_row(tm, D), _row(tm, D), _row(tm, D, 2), _par(1, D), _par(1, D)],
                 out_specs=[_row(tm, D), _row(tm, D), _par(1, D), _par(1, D)],
                 out_shape=[_sds((S, D), F32), _sds((S, D), BF16), _sds((1, D), F32), _sds((1, D), F32)],
                 compiler_params=_cp(("arbitrary",)))(d_ca, u1, P, g_ln, b_ln)


def _conv_bwd_taps(d_u1, P, w_dw, D):
    S = P.shape[0]
    tm = min(ROW_TILE, S)
    nblk = S // HALO

    def body(d_ref, dn_ref, a_ref, b_ref, ap_ref, bp_ref, w_ref, da_ref, db_ref, gw_ref, gb_ref, dbuf, ubuf):
        i = pl.program_id(0)
        a, sb = a_ref[...], _sg(b_ref[...])
        d = d_ref[...]
        dbuf[0:tm, :] = d
        dbuf[tm:tm + HALO, :] = jnp.where(i == pl.num_programs(0) - 1, 0.0, dn_ref[...])
        ubuf[0:HALO, :] = jnp.where(i == 0, 0.0, ap_ref[...] * _sg(bp_ref[...]))
        ubuf[HALO:HALO + tm, :] = a * sb

        @pl.when(i == 0)
        def _():
            gw_ref[...] = jnp.zeros_like(gw_ref)
            gb_ref[...] = jnp.zeros_like(gb_ref)

        acc = jnp.zeros((tm, D), F32)
        for j in range(CONV_W):
            acc = acc + w_ref[j:j + 1, :] * dbuf[CONV_W - 1 - j:CONV_W - 1 - j + tm, :]
            sh = HALO - CONV_W + 1 + j
            gw_ref[j:j + 1, :] += jnp.sum(d * ubuf[sh:sh + tm, :], axis=0, keepdims=True)
        gb_ref[...] += jnp.sum(d, axis=0, keepdims=True)
        da_ref[...] = (acc * sb).astype(BF16)
        db_ref[...] = (acc * a * sb * (1.0 - sb)).astype(BF16)

    return _call(body, name="conv_bwd_taps", grid=(S // tm,),
                 in_specs=[_row(tm, D), _next(tm, HALO, D, nblk), _row(tm, D, 0), _row(tm, D, 1),
                           _prev(tm, HALO, D, 0), _prev(tm, HALO, D, 1), _par(HALO, D)],
                 out_specs=[_row(tm, D), _row(tm, D), _par(HALO, D), _par(1, D)],
                 out_shape=[_sds((S, D), BF16), _sds((S, D), BF16), _sds((HALO, D), F32), _sds((1, D), F32)],
                 scratch_shapes=[pltpu.VMEM((tm + HALO, D), F32), pltpu.VMEM((tm + HALO, D), F32)],
                 compiler_params=_cp(("arbitrary",)))(d_u1, d_u1, P, P, P, P, w_dw)


def _qk_fwd(P, wq, wk, D, col):
    S = P.shape[0]
    tm = min(ROW_TILE, S)
    scale = float(D // N_HEADS) ** -0.5

    def body(q_ref, k_ref, qp_ref, kp_ref, wq_ref, wk_ref, qo_ref, ko_ref, cq_ref, ck_ref, buf):
        first = pl.program_id(0) == 0
        for x_ref, p_ref, w_ref, o_ref, c_ref, sc in ((q_ref, qp_ref, wq_ref, qo_ref, cq_ref, 1.0),
                                                      (k_ref, kp_ref, wk_ref, ko_ref, ck_ref, scale)):
            buf[0:QK_HALO, :] = jnp.where(first, 0.0, p_ref[...])
            buf[QK_HALO:QK_HALO + tm, :] = x_ref[...]
            acc = jnp.zeros((tm, D), F32)
            for j in range(QK_W):
                sh = QK_HALO - QK_W + 1 + j
                acc = acc + w_ref[j:j + 1, :] * buf[sh:sh + tm, :]
            c_ref[...] = acc
            o_ref[...] = acc * _sg(acc) * sc

    return _call(body, name="qk_fwd", grid=(S // tm,),
                 in_specs=[_row(tm, D, col), _row(tm, D, col + 1), _prev(tm, QK_HALO, D, col), _prev(tm, QK_HALO, D, col + 1),
                           _par(QK_HALO, D), _par(QK_HALO, D)],
                 out_specs=[_row(tm, D)] * 4, out_shape=[_sds((S, D), F32)] * 4,
                 scratch_shapes=[pltpu.VMEM((tm + QK_HALO, D), F32)], compiler_params=_cp(("parallel",)))(P, P, P, P, wq, wk)


def _qk_bwd(dq, dk, cq, ck, P, wq, wk, D, col):
    S = P.shape[0]
    tm = min(ROW_TILE, S)
    nblk = S // QK_HALO
    scale = float(D // N_HEADS) ** -0.5

    def body(dq_ref, dqn_ref, cq_ref, cqn_ref, xq_ref, xqp_ref, wq_ref,
             dk_ref, dkn_ref, ck_ref, ckn_ref, xk_ref, xkp_ref, wk_ref,
             oq_ref, ok_ref, gq_ref, gk_ref, dbuf, xbuf):
        i = pl.program_id(0)
        last = i == pl.num_programs(0) - 1

        @pl.when(i == 0)
        def _():
            gq_ref[...] = jnp.zeros_like(gq_ref)
            gk_ref[...] = jnp.zeros_like(gk_ref)

        for d_ref, dn_ref, c_ref, cn_ref, x_ref, xp_ref, w_ref, o_ref, g_ref, sc in (
                (dq_ref, dqn_ref, cq_ref, cqn_ref, xq_ref, xqp_ref, wq_ref, oq_ref, gq_ref, 1.0),
                (dk_ref, dkn_ref, ck_ref, ckn_ref, xk_ref, xkp_ref, wk_ref, ok_ref, gk_ref, scale)):
            c, cn = c_ref[...], cn_ref[...]
            dc = d_ref[...] * sc * _dsilu(c, _sg(c))
            dcn = dn_ref[...] * sc * _dsilu(cn, _sg(cn))
            dbuf[0:tm, :] = dc
            dbuf[tm:tm + QK_HALO, :] = jnp.where(last, 0.0, dcn)
            xbuf[0:QK_HALO, :] = jnp.where(i == 0, 0.0, xp_ref[...])
            xbuf[QK_HALO:QK_HALO + tm, :] = x_ref[...]
            acc = jnp.zeros((tm, D), F32)
            for j in range(QK_W):
                acc = acc + w_ref[j:j + 1, :] * dbuf[QK_W - 1 - j:QK_W - 1 - j + tm, :]
                sh = QK_HALO - QK_W + 1 + j
                g_ref[j:j + 1, :] += jnp.sum(dc * xbuf[sh:sh + tm, :], axis=0, keepdims=True)
            o_ref[...] = acc.astype(BF16)

    one = [_row(tm, D), _next(tm, QK_HALO, D, nblk)]
    specs = (one + one + [_row(tm, D, col), _prev(tm, QK_HALO, D, col), _par(QK_HALO, D)]
             + one + one + [_row(tm, D, col + 1), _prev(tm, QK_HALO, D, col + 1), _par(QK_HALO, D)])
    return _call(body, name="qk_bwd", grid=(S // tm,), in_specs=specs,
                 out_specs=[_row(tm, D), _row(tm, D), _par(QK_HALO, D), _par(QK_HALO, D)],
                 out_shape=[_sds((S, D), BF16), _sds((S, D), BF16), _sds((QK_HALO, D), F32), _sds((QK_HALO, D), F32)],
                 scratch_shapes=[pltpu.VMEM((tm + QK_HALO, D), F32), pltpu.VMEM((tm + QK_HALO, D), F32)],
                 compiler_params=_cp(("arbitrary",)))(dq, dq, cq, cq, P, P, wq, dk, dk, ck, ck, P, P, wk)


def _gates_fwd(gif, b_if):
    S = gif.shape[0]
    tm = min(ROW_TILE, S)

    def body(g_ref, b_ref, o_ref):
        z = g_ref[...] + b_ref[...]
        lane = lax.broadcasted_iota(jnp.int32, z.shape, 1)
        ls = jnp.minimum(z, 0.0) - jnp.log(1.0 + jnp.exp(-jnp.abs(z)))
        o_ref[...] = jnp.where(lane < N_HEADS, z, jnp.where(lane < 2 * N_HEADS, ls, 0.0))

    return _call(body, name="gates_fwd", grid=(S // tm,), in_specs=[_row(tm, 128), _par(1, 128)], out_specs=_row(tm, 128),
                 out_shape=_sds((S, 128), F32), compiler_params=_cp(("parallel",)))(gif, b_if)


def _gates_bwd(dgl, gif, b_if):
    S = gif.shape[0]
    tm = min(ROW_TILE, S)

    def body(d_ref, g_ref, b_ref, o_ref, gb_ref):
        z = g_ref[...] + b_ref[...]
        lane = lax.broadcasted_iota(jnp.int32, z.shape, 1)
        d = d_ref[...]
        dz = jnp.where(lane < N_HEADS, d, jnp.where(lane < 2 * N_HEADS, d * _sg(-z), 0.0))
        o_ref[...] = dz.astype(BF16)

        @pl.when(pl.program_id(0) == 0)
        def _():
            gb_ref[...] = jnp.zeros_like(gb_ref)

        gb_ref[...] += jnp.sum(dz, axis=0, keepdims=True)

    return _call(body, name="gates_bwd", grid=(S // tm,), in_specs=[_row(tm, 128), _row(tm, 128), _par(1, 128)],
                 out_specs=[_row(tm, 128), _par(1, 128)], out_shape=[_sds((S, 128), BF16), _sds((1, 128), F32)],
                 compiler_params=_cp(("arbitrary",)))(dgl, gif, b_if)


def _chunk_gates(g8_ref, gt_ref, h, L):
    g8, gt = g8_ref[...], gt_ref[...]
    lane = lax.broadcasted_iota(jnp.int32, g8.shape, 1)
    sub = lax.broadcasted_iota(jnp.int32, gt.shape, 0)
    li_col = jnp.sum(jnp.where(lane == h, g8, 0.0), axis=1, keepdims=True)
    lf_col = jnp.sum(jnp.where(lane == h + N_HEADS, g8, 0.0), axis=1, keepdims=True)
    li_row = jnp.sum(jnp.where(sub == h, gt, 0.0), axis=0, keepdims=True)
    lf_row = jnp.sum(jnp.where(sub == h + N_HEADS, gt, 0.0), axis=0, keepdims=True)
    t = lax.broadcasted_iota(jnp.int32, (L, L), 0)
    s = lax.broadcasted_iota(jnp.int32, (L, L), 1)
    causal = s <= t
    b_col = jnp.sum(jnp.where(causal, lf_row, 0.0), axis=1, keepdims=True)
    b_row = jnp.sum(jnp.where(t <= s, lf_col, 0.0), axis=0, keepdims=True)
    return li_col, li_row, b_col, b_row, causal


def _chunk_fwd(q, k, v, C, n, m, li_row, b_col, b_row, causal):
    d = jnp.where(causal, b_col - b_row + li_row, NEG)
    inter = b_col + m
    m_row = jnp.maximum(inter, jnp.max(d, axis=1, keepdims=True))
    wi = jnp.exp(d - m_row)
    wn = jnp.exp(inter - m_row)
    s = _dot(q, k, NT) * wi
    num = _dot(s, v, NN) + wn * _dot(q, C, NN)
    den = jnp.sum(s, axis=1, keepdims=True) + wn * jnp.sum(q * n, axis=1, keepdims=True)
    e = jnp.exp(-m_row)
    inv = 1.0 / jnp.maximum(jnp.abs(den), e)
    return wi, wn, s, num * inv, den, e, inv


def _chunk_state(li_col, li_row, b_col, b_row, m, L):
    blast = b_col[L - 1:L, :]
    g_col = blast - b_col + li_col
    g_row = blast - b_row + li_row
    m_new = jnp.maximum(blast + m, jnp.max(g_row, axis=1, keepdims=True))
    decay = jnp.exp(blast + m - m_new)
    wk_col = jnp.exp(g_col - m_new)
    return m_new, decay, wk_col


def _mlstm_fwd(q, k, P, g8, gt, D, vcol):
    S = q.shape[0]
    H, dh = N_HEADS, D // N_HEADS
    L = min(ML_CHUNK, S)
    nc = S // L

    def body(q_ref, k_ref, v_ref, g8_ref, gt_ref, h_ref, cs_ref, ns_ref, ms_ref, C, n, m):
        h, j = pl.program_id(0), pl.program_id(1)

        @pl.when(j == 0)
        def _():
            C[...] = jnp.zeros_like(C)
            n[...] = jnp.zeros_like(n)
            m[...] = jnp.zeros_like(m)

        qv, kv, vv = q_ref[...], k_ref[...], v_ref[...]
        Cv, nv, mv = C[...], n[...], m[:, 0:1]
        cs_ref[...] = Cv.astype(BF16)
        ns_ref[...] = nv
        ms_ref[...] = m[...]
        li_col, li_row, b_col, b_row, causal = _chunk_gates(g8_ref, gt_ref, h, L)
        _, _, _, hv, _, _, _ = _chunk_fwd(qv, kv, vv, Cv, nv, mv, li_row, b_col, b_row, causal)
        h_ref[...] = hv
        m_new, decay, wk_col = _chunk_state(li_col, li_row, b_col, b_row, mv, L)
        kw = kv * wk_col
        C[...] = decay * Cv + _dot(kw, vv, TN)
        n[...] = decay * nv + jnp.sum(kw, axis=0, keepdims=True)
        m[...] = jnp.broadcast_to(m_new, m.shape)

    blk = lambda c0: pl.BlockSpec((L, dh), lambda h, j: (j, c0 + h))
    return _call(body, name="mlstm_fwd", grid=(H, nc),
                 in_specs=[blk(0), blk(0), blk(vcol * H), pl.BlockSpec((L, 8), lambda h, j: (j, 0)),
                           pl.BlockSpec((8, L), lambda h, j: (0, j))],
                 out_specs=[blk(0), pl.BlockSpec((None, None, dh, dh), lambda h, j: (h, j, 0, 0)),
                            pl.BlockSpec((None, None, 1, dh), lambda h, j: (h, j, 0, 0)),
                            pl.BlockSpec((None, None, 1, 128), lambda h, j: (h, j, 0, 0))],
                 out_shape=[_sds((S, D), F32), _sds((H, nc, dh, dh), BF16), _sds((H, nc, 1, dh), F32),
                            _sds((H, nc, 1, 128), F32)],
                 scratch_shapes=[pltpu.VMEM((dh, dh), F32), pltpu.VMEM((1, dh), F32), pltpu.VMEM((1, 128), F32)],
                 compiler_params=_cp(("arbitrary", "arbitrary")))(q, k, P, g8, gt)


def _mlstm_bwd(dhm, q, k, P, g8, gt, cs, ns, ms, D, vcol):
    S = q.shape[0]
    H, dh = N_HEADS, D // N_HEADS
    L = min(ML_CHUNK, S)
    nc = S // L

    def body(dh_ref, q_ref, k_ref, v_ref, g8_ref, gt_ref, cs_ref, ns_ref, ms_ref,
             dq_ref, dk_ref, dv_ref, dg_ref, dC, dnv):
        h, j = pl.program_id(0), pl.program_id(1)

        @pl.when(j == 0)
        def _():
            dC[...] = jnp.zeros_like(dC)
            dnv[...] = jnp.zeros_like(dnv)

        qv, kv, vv, dhv = q_ref[...], k_ref[...], v_ref[...], dh_ref[...]
        Cv, nv, mv = cs_ref[...], ns_ref[...], ms_ref[:, 0:1]
        li_col, li_row, b_col, b_row, causal = _chunk_gates(g8_ref, gt_ref, h, L)
        wi, wn, s, hv, den, e, inv = _chunk_fwd(qv, kv, vv, Cv, nv, mv, li_row, b_col, b_row, causal)
        dn = dhv * inv
        dd = -jnp.sum(dhv * hv, axis=1, keepdims=True) * inv * jnp.where(jnp.abs(den) > e, jnp.sign(den), 0.0)
        dS = _dot(dn, vv, NT) + dd
        dqk = dS * wi
        dq_i = wn * (_dot(dn, Cv, NT) + dd * nv)
        dq = _dot(dqk, kv, NN) + dq_i
        dv = _dot(s, dn, TN)
        m_new, decay, wk_col = _chunk_state(li_col, li_row, b_col, b_row, mv, L)
        dCv, dnvv = dC[...], dnv[...]
        dk_i = wk_col * (_dot(vv, dCv, NT) + dnvv)
        dk = _dot(dqk, qv, TN) + dk_i
        dv = dv + _dot(kv * wk_col, dCv, NN)
        qw = qv * wn
        dC[...] = decay * dCv + _dot(qw, dn, TN)
        dnv[...] = decay * dnvv + jnp.sum(qw * dd, axis=0, keepdims=True)
        dq_ref[...] = dq
        dk_ref[...] = dk
        dv_ref[...] = dv.astype(BF16)
        G = dS * s
        t = lax.broadcasted_iota(jnp.int32, (L, L), 0)
        r = lax.broadcasted_iota(jnp.int32, (L, L), 1)
        eye = t == r
        col_row = jnp.sum(G, axis=0, keepdims=True)
        col_g = jnp.sum(jnp.where(eye, col_row, 0.0), axis=1, keepdims=True)
        row_i = jnp.sum(qv * dq_i, axis=1, keepdims=True)
        col_i = jnp.sum(kv * dk_i, axis=1, keepdims=True)
        dF = jnp.sum(G, axis=1, keepdims=True) - col_g + row_i
        across = decay * (jnp.sum(jnp.sum(dCv * Cv.astype(F32), axis=0, keepdims=True), axis=1, keepdims=True)
                          + jnp.sum(dnvv * nv, axis=1, keepdims=True))
        dlf = jnp.sum(jnp.where(t >= r, dF, 0.0) + jnp.where(t < r, col_i, 0.0), axis=0, keepdims=True) + across
        dli = col_row + jnp.sum(jnp.where(eye, col_i, 0.0), axis=0, keepdims=True)
        sub = lax.broadcasted_iota(jnp.int32, (8, L), 0)
        dg_ref[...] = jnp.where(sub == 0, dli, jnp.where(sub == 1, dlf, 0.0))

    blk = lambda c0: pl.BlockSpec((L, dh), lambda h, j: (nc - 1 - j, c0 + h))
    st = lambda r, c: pl.BlockSpec((None, None, r, c), lambda h, j: (h, nc - 1 - j, 0, 0))
    return _call(body, name="mlstm_bwd", grid=(H, nc),
                 in_specs=[blk(0), blk(0), blk(0), blk(vcol * H), pl.BlockSpec((L, 8), lambda h, j: (nc - 1 - j, 0)),
                           pl.BlockSpec((8, L), lambda h, j: (0, nc - 1 - j)), st(dh, dh), st(1, dh), st(1, 128)],
                 out_specs=[blk(0), blk(0), blk(0), pl.BlockSpec((None, 8, L), lambda h, j: (h, 0, nc - 1 - j))],
                 out_shape=[_sds((S, D), F32), _sds((S, D), F32), _sds((S, D), BF16), _sds((H, 8, S), F32)],
                 scratch_shapes=[pltpu.VMEM((dh, dh), F32), pltpu.VMEM((1, dh), F32)],
                 compiler_params=_cp(("arbitrary", "arbitrary")))(dhm, q, k, P, g8, gt, cs, ns, ms)


def _ml_post_fwd(hm, P, g, D, ocol, zcol):
    S = hm.shape[0]
    tm = min(ROW_TILE, S)
    dh = D // N_HEADS

    def body(h_ref, o_ref, z_ref, g_ref, cb_ref):
        for hd in range(N_HEADS):
            sl = slice(hd * dh, (hd + 1) * dh)
            x = _sg(o_ref[:, sl]) * h_ref[:, sl]
            r = lax.rsqrt(_mean(x * x) + EPS)
            z = z_ref[:, sl]
            cb_ref[:, sl] = (x * r * g_ref[:, sl] * (z * _sg(z))).astype(BF16)

    return _call(body, name="ml_post_fwd", grid=(S // tm,),
                 in_specs=[_row(tm, D), _row(tm, D, ocol), _row(tm, D, zcol), _par(1, D)], out_specs=_row(tm, D),
                 out_shape=_sds((S, D), BF16), compiler_params=_cp(("parallel",)))(hm, P, P, g)


def _ml_post_bwd(d_cb, hm, P, g, D, ocol, zcol):
    S = hm.shape[0]
    tm = min(ROW_TILE, S)
    dh = D // N_HEADS

    def body(d_ref, h_ref, o_ref, z_ref, g_ref, dh_ref, do_ref, dz_ref, dg_ref):
        @pl.when(pl.program_id(0) == 0)
        def _():
            dg_ref[...] = jnp.zeros_like(dg_ref)

        for hd in range(N_HEADS):
            sl = slice(hd * dh, (hd + 1) * dh)
            hv, z, gv, d = h_ref[:, sl], z_ref[:, sl], g_ref[:, sl], d_ref[:, sl]
            so, sz = _sg(o_ref[:, sl]), _sg(z)
            x = so * hv
            r = lax.rsqrt(_mean(x * x) + EPS)
            xh = x * r
            d3 = d * (z * sz)
            dz_ref[:, sl] = (d * xh * gv * _dsilu(z, sz)).astype(BF16)
            dg_ref[:, sl] += jnp.sum(d3 * xh, axis=0, keepdims=True)
            dxh = d3 * gv
            d2 = r * (dxh - xh * _mean(dxh * xh))
            do_ref[:, sl] = (d2 * hv * so * (1.0 - so)).astype(BF16)
            dh_ref[:, sl] = d2 * so

    return _call(body, name="ml_post_bwd", grid=(S // tm,),
                 in_specs=[_row(tm, D), _row(tm, D), _row(tm, D, ocol), _row(tm, D, zcol), _par(1, D)],
                 out_specs=[_row(tm, D), _row(tm, D), _row(tm, D), _par(1, D)],
                 out_shape=[_sds((S, D), F32), _sds((S, D), BF16), _sds((S, D), BF16), _sds((1, D), F32)],
                 compiler_params=_cp(("arbitrary",)))(d_cb, hm, P, P, g)


def _attn_scores(q, kh, scale):
    sc = _dot(q, kh, NT) * scale
    ex = jnp.exp(sc - jnp.max(sc, axis=1, keepdims=True))
    return ex / jnp.sum(ex, axis=1, keepdims=True)


def _attn_fwd(P, kv, D, qcol, zcol):
    S, M = P.shape[0], kv.shape[0]
    tm = min(ROW_TILE, S)
    dh = D // N_HEADS
    scale = float(dh) ** -0.5

    def body(q_ref, z_ref, kv_ref, cc_ref):
        for hd in range(N_HEADS):
            sl = slice(hd * dh, (hd + 1) * dh)
            p = _attn_scores(q_ref[:, sl], kv_ref[:, sl], scale)
            o = _dot(p, kv_ref[:, D + hd * dh:D + (hd + 1) * dh], NN)
            z = z_ref[:, sl]
            cc_ref[:, sl] = (o * (z * _sg(z))).astype(BF16)

    return _call(body, name="attn_fwd", grid=(S // tm,),
                 in_specs=[_row(tm, D, qcol), _row(tm, D, zcol), _par(M, 2 * D)], out_specs=_row(tm, D),
                 out_shape=_sds((S, D), BF16), compiler_params=_cp(("parallel",)))(P, P, kv)


def _attn_bwd(d_cc, P, kv, D, qcol, zcol):
    S, M = P.shape[0], kv.shape[0]
    tm = min(ROW_TILE, S)
    dh = D // N_HEADS
    scale = float(dh) ** -0.5

    def body(d_ref, q_ref, z_ref, kv_ref, dq_ref, dz_ref, dkv_ref):
        @pl.when(pl.program_id(0) == 0)
        def _():
            dkv_ref[...] = jnp.zeros_like(dkv_ref)

        for hd in range(N_HEADS):
            sl = slice(hd * dh, (hd + 1) * dh)
            vl = slice(D + hd * dh, D + (hd + 1) * dh)
            q, kh, vh = q_ref[:, sl], kv_ref[:, sl], kv_ref[:, vl]
            p = _attn_scores(q, kh, scale)
            o = _dot(p, vh, NN)
            z, d = z_ref[:, sl], d_ref[:, sl]
            sz = _sg(z)
            do = d * (z * sz)
            dz_ref[:, sl] = (d * o * _dsilu(z, sz)).astype(BF16)
            dp = _dot(do, vh, NT)
            ds = p * (dp - jnp.sum(p * dp, axis=1, keepdims=True)) * scale
            dq_ref[:, sl] = _dot(ds, kh, NN).astype(BF16)
            dkv_ref[:, sl] += _dot(ds, q, TN)
            dkv_ref[:, vl] += _dot(p, do, TN)

    return _call(body, name="attn_bwd", grid=(S // tm,),
                 in_specs=[_row(tm, D), _row(tm, D, qcol), _row(tm, D, zcol), _par(M, 2 * D)],
                 out_specs=[_row(tm, D), _row(tm, D), _par(M, 2 * D)],
                 out_shape=[_sds((S, D), BF16), _sds((S, D), BF16), _sds((M, 2 * D), F32)],
                 compiler_params=_cp(("arbitrary",)))(d_cc, P, P, kv)


def _merge_fwd(yc, ym, yx, P, D, gcol):
    S = yc.shape[0]
    tm = min(ROW_TILE, S)

    def body(c_ref, m_ref, x_ref, gc_ref, gm_ref, gx_ref, o_ref):
        o_ref[...] = (_sg(gc_ref[...]) * c_ref[...] + _sg(gm_ref[...]) * m_ref[...]
                      + _sg(gx_ref[...]) * x_ref[...]).astype(BF16)

    return _call(body, name="merge_fwd", grid=(S // tm,),
                 in_specs=[_row(tm, D)] * 3 + [_row(tm, D, gcol), _row(tm, D, gcol + 1), _row(tm, D, gcol + 2)],
                 out_specs=_row(tm, D), out_shape=_sds((S, D), BF16), compiler_params=_cp(("parallel",)))(yc, ym, yx, P, P, P)


def _merge_bwd(dm, yc, ym, yx, P, D, gcol):
    S = yc.shape[0]
    tm = min(ROW_TILE, S)

    def body(d_ref, c_ref, m_ref, x_ref, gc_ref, gm_ref, gx_ref, dc_ref, dmm_ref, dx_ref, dgc_ref, dgm_ref, dgx_ref):
        d = d_ref[...]
        for y_ref, g_ref, dy_ref, dg_ref in ((c_ref, gc_ref, dc_ref, dgc_ref), (m_ref, gm_ref, dmm_ref, dgm_ref),
                                             (x_ref, gx_ref, dx_ref, dgx_ref)):
            s = _sg(g_ref[...])
            dy_ref[...] = (d * s).astype(BF16)
            dg_ref[...] = (d * y_ref[...] * s * (1.0 - s)).astype(BF16)

    return _call(body, name="merge_bwd", grid=(S // tm,),
                 in_specs=[_row(tm, D)] * 4 + [_row(tm, D, gcol), _row(tm, D, gcol + 1), _row(tm, D, gcol + 2)],
                 out_specs=[_row(tm, D)] * 6, out_shape=[_sds((S, D), BF16)] * 6,
                 compiler_params=_cp(("parallel",)))(dm, yc, ym, yx, P, P, P)


def _post(o, x, tgt, g):
    S, D = x.shape
    tm = min(ROW_TILE, S)

    def body(o_ref, x_ref, t_ref, g_ref, dy_ref, do_ref, dg_ref, l_ref):
        ov, gv = o_ref[...], g_ref[...]
        r = lax.rsqrt(_mean(ov * ov) + EPS)
        e = x_ref[...] + ov * r * gv - t_ref[...]
        dy = e / D
        dy_ref[...] = dy
        dxh = dy * gv
        do_ref[...] = (r * dxh - ov * (r * r * r) * _mean(dxh * ov)).astype(BF16)

        @pl.when(pl.program_id(0) == 0)
        def _():
            dg_ref[...] = jnp.zeros_like(dg_ref)
            l_ref[...] = jnp.zeros_like(l_ref)

        dg_ref[...] += jnp.sum(dy * ov * r, axis=0, keepdims=True)
        l_ref[...] += jnp.sum(e * e, axis=0, keepdims=True)

    return _call(body, name="post", grid=(S // tm,), in_specs=[_row(tm, D)] * 3 + [_par(1, D)],
                 out_specs=[_row(tm, D), _row(tm, D), _par(1, D), _par(1, D)],
                 out_shape=[_sds((S, D), F32), _sds((S, D), BF16), _sds((1, D), F32), _sds((1, D), F32)],
                 compiler_params=_cp(("arbitrary",)))(o, x, tgt, g)


def _chip_peers():
    x, y, c = lax.axis_index("x"), lax.axis_index("y"), lax.axis_index("c")
    return x, y, c, [(1 - x, y), (x, 1 - y), (1 - x, 1 - y)]


def _gather_chips(halved, whole):
    TH, TW = len(halved), len(whole)
    T = TH + TW

    def body(*refs):
        ins, outs = refs[:T], refs[T:2 * T]
        send, recv, fsend, frecv, loc = refs[2 * T:]
        x, y, c, chips = _chip_peers()
        mine = 2 * x + y
        copies = []
        for t in range(T):
            lc = pltpu.make_async_copy(ins[t], outs[t].at[mine], loc.at[t])
            lc.start()
            copies.append(lc)

        def over_ici(t, j, chip, slot):
            src = ins[t].at[c] if t < TH else ins[t]
            dst = outs[t].at[slot, c] if t < TH else outs[t].at[slot]
            return pltpu.make_async_remote_copy(src_ref=src, dst_ref=dst, send_sem=send.at[t, j], recv_sem=recv.at[t, j],
                                                device_id=(*chip, c), device_id_type=MESH)

        def to_sibling(t, j, slot, half):
            place = outs[t].at[slot, half]
            return pltpu.make_async_remote_copy(src_ref=place, dst_ref=place, send_sem=fsend.at[t, j], recv_sem=frecv.at[t, j],
                                                device_id=(x, y, 1 - c), device_id_type=MESH)

        sends = []
        for t in range(T):
            for j, chip in enumerate(chips):
                cp = over_ici(t, j, chip, mine)
                cp.start()
                sends.append(cp)
        for t in range(T):
            for j, (px, py) in enumerate(chips):
                over_ici(t, j, (px, py), 2 * px + py).wait_recv()
                if t < TH:
                    cp = to_sibling(t, j, 2 * px + py, c)
                    cp.start()
                    sends.append(cp)
        for t in range(TH):
            for j, (px, py) in enumerate(chips):
                to_sibling(t, j, 2 * px + py, 1 - c).wait_recv()
        for cp in sends:
            cp.wait_send()
        for lc in copies:
            lc.wait()

    anyspec = pl.BlockSpec(memory_space=pl.ANY)
    dma = pltpu.SemaphoreType.DMA
    return _call(body, name="gather_weights", in_specs=[anyspec] * T, out_specs=[anyspec] * T,
                 out_shape=[_sds((4, *s.shape), s.dtype) for s in (*halved, *whole)],
                 scratch_shapes=[dma((T, 3)), dma((T, 3)), dma((TH, 3)), dma((TH, 3)), dma((T,))],
                 compiler_params=pltpu.CompilerParams(has_side_effects=True))(*halved, *whole)


def _to_sibling(tensors, name):
    T = len(tensors)

    def body(*refs):
        ins, outs = refs[:T], refs[T:2 * T]
        send, recv = refs[2 * T:]
        x, y, c = lax.axis_index("x"), lax.axis_index("y"), lax.axis_index("c")
        cps = [pltpu.make_async_remote_copy(src_ref=ins[t], dst_ref=outs[t], send_sem=send.at[t], recv_sem=recv.at[t],
                                            device_id=(x, y, 1 - c), device_id_type=MESH) for t in range(T)]
        for cp in cps:
            cp.start()
        for cp in cps:
            cp.wait_recv()
        for cp in cps:
            cp.wait_send()

    anyspec = pl.BlockSpec(memory_space=pl.ANY)
    return _call(body, name=name, in_specs=[anyspec] * T, out_specs=[anyspec] * T,
                 out_shape=[_sds(a.shape, a.dtype) for a in tensors],
                 scratch_shapes=[pltpu.SemaphoreType.DMA((T,)), pltpu.SemaphoreType.DMA((T,))],
                 compiler_params=pltpu.CompilerParams(has_side_effects=True))(*tensors)


def _share_halves(tensors):
    T = len(tensors)

    def body(*refs):
        ins, outs = refs[:T], refs[T:2 * T]
        send, recv, loc = refs[2 * T:]
        x, y, c = lax.axis_index("x"), lax.axis_index("y"), lax.axis_index("c")
        copies = [pltpu.make_async_copy(ins[t], outs[t].at[c], loc.at[t]) for t in range(T)]
        cps = [pltpu.make_async_remote_copy(src_ref=ins[t], dst_ref=outs[t].at[c], send_sem=send.at[t], recv_sem=recv.at[t],
                                            device_id=(x, y, 1 - c), device_id_type=MESH) for t in range(T)]
        for cp in copies + cps:
            cp.start()
        for t in range(T):
            pltpu.make_async_remote_copy(src_ref=ins[t], dst_ref=outs[t].at[1 - c], send_sem=send.at[t], recv_sem=recv.at[t],
                                         device_id=(x, y, 1 - c), device_id_type=MESH).wait_recv()
        for cp in cps:
            cp.wait_send()
        for cp in copies:
            cp.wait()

    anyspec = pl.BlockSpec(memory_space=pl.ANY)
    dma = pltpu.SemaphoreType.DMA
    return _call(body, name="share_halves", in_specs=[anyspec] * T, out_specs=[anyspec] * T,
                 out_shape=[_sds((2, *a.shape), a.dtype) for a in tensors], scratch_shapes=[dma((T,)), dma((T,)), dma((T,))],
                 compiler_params=pltpu.CompilerParams(has_side_effects=True))(*tensors)


def _chip_scatter(tensors):
    T = len(tensors)

    def body(*refs):
        ins, outs = refs[:T], refs[T:2 * T]
        send, recv, loc = refs[2 * T:]
        x, y, c, chips = _chip_peers()
        mine = 2 * x + y
        copies = []
        for t in range(T):
            lc = pltpu.make_async_copy(ins[t].at[mine], outs[t].at[mine], loc.at[t])
            lc.start()
            copies.append(lc)

        def over_ici(t, j, chip, src_slot, dst_slot):
            return pltpu.make_async_remote_copy(src_ref=ins[t].at[src_slot], dst_ref=outs[t].at[dst_slot], send_sem=send.at[t, j],
                                                recv_sem=recv.at[t, j], device_id=(*chip, c), device_id_type=MESH)

        sends = []
        for t in range(T):
            for j, (px, py) in enumerate(chips):
                cp = over_ici(t, j, (px, py), 2 * px + py, mine)
                cp.start()
                sends.append(cp)
        for t in range(T):
            for j, (px, py) in enumerate(chips):
                over_ici(t, j, (px, py), mine, 2 * px + py).wait_recv()
        for cp in sends:
            cp.wait_send()
        for lc in copies:
            lc.wait()

    anyspec = pl.BlockSpec(memory_space=pl.ANY)
    dma = pltpu.SemaphoreType.DMA
    return _call(body, name="chip_scatter", in_specs=[anyspec] * T, out_specs=[anyspec] * T,
                 out_shape=[_sds(a.shape, a.dtype) for a in tensors], scratch_shapes=[dma((T, 3)), dma((T, 3)), dma((T,))],
                 compiler_params=pltpu.CompilerParams(has_side_effects=True))(*tensors)


def _row_tile(R, C, slots):
    tr = max(16, ((8 << 20) // (C * 4 * slots)) // 16 * 16)
    if tr >= R:
        return R
    while R % tr:
        tr -= 16
    return tr


def _add_pair(a, b, name):
    lead, R, C = a.shape
    tr = _row_tile(R, C, 3)

    def body(a_ref, b_ref, o_ref):
        o_ref[...] = (a_ref[...].astype(F32) + b_ref[...].astype(F32)).astype(BF16)

    blk = pl.BlockSpec((None, tr, C), lambda k, i: (k, i, 0))
    return _call(body, name=name, grid=(lead, R // tr), in_specs=[blk, blk], out_specs=blk, out_shape=_sds(a.shape, BF16),
                 compiler_params=_cp(("parallel", "parallel")))(a, b)


def _sum_chips(land, name):
    _, R, C = land.shape
    tr = _row_tile(R, C, 5)

    def body(l_ref, o_ref):
        o_ref[...] = ((l_ref[0].astype(F32) + l_ref[1].astype(F32)) + l_ref[2].astype(F32)) + l_ref[3].astype(F32)

    return _call(body, name=name, grid=(R // tr,), in_specs=[pl.BlockSpec((4, tr, C), lambda i: (0, i, 0))],
                 out_specs=pl.BlockSpec((tr, C), lambda i: (i, 0)), out_shape=_sds((R, C), F32),
                 compiler_params=_cp(("parallel",)))(land)


def _scatter_grads(blocks, small):
    T = len(blocks)

    def body(*refs):
        ins, sm = refs[:T], refs[T]
        outs, smo = refs[T + 1:2 * T + 1], refs[2 * T + 1]
        send, recv, loc = refs[2 * T + 2:]
        x, y, c = lax.axis_index("x"), lax.axis_index("y"), lax.axis_index("c")
        me = 4 * x + 2 * y + c
        peers = [(x ^ ((k >> 2) & 1), y ^ ((k >> 1) & 1), c ^ (k & 1)) for k in range(1, N_DEV)]
        copies = []
        for t in range(T):
            lc = pltpu.make_async_copy(ins[t].at[2 * x + y], outs[t].at[me], loc.at[t])
            lc.start()
            copies.append(lc)
        lc = pltpu.make_async_copy(sm, smo.at[me], loc.at[T])
        lc.start()
        copies.append(lc)

        def remote(t, j, peer, sender):
            px, py, pc = peer
            src = sm if t == T else ins[t].at[2 * px + py]
            dst = (smo if t == T else outs[t]).at[sender]
            return pltpu.make_async_remote_copy(src_ref=src, dst_ref=dst, send_sem=send.at[t, j], recv_sem=recv.at[t, j],
                                                device_id=peer, device_id_type=MESH)

        sends = []
        for t in range(T + 1):
            for j, peer in enumerate(peers):
                cp = remote(t, j, peer, me)
                cp.start()
                sends.append(cp)
        for t in range(T + 1):
            for j, (px, py, pc) in enumerate(peers):
                remote(t, j, (x, y, c), 4 * px + 2 * py + pc).wait_recv()
        for cp in sends:
            cp.wait_send()
        for lc in copies:
            lc.wait()

    anyspec = pl.BlockSpec(memory_space=pl.ANY)
    return _call(body, name="scatter_grads", in_specs=[anyspec] * (T + 1), out_specs=[anyspec] * (T + 1),
                 out_shape=[_sds((N_DEV, *b.shape[1:]), b.dtype) for b in blocks] + [_sds((N_DEV, *small.shape), small.dtype)],
                 scratch_shapes=[pltpu.SemaphoreType.DMA((T + 1, N_DEV - 1)), pltpu.SemaphoreType.DMA((T + 1, N_DEV - 1)),
                                 pltpu.SemaphoreType.DMA((T + 1,))],
                 compiler_params=pltpu.CompilerParams(has_side_effects=True))(*blocks, small)


def _adamw(land, w, m, v, name):
    R, C = w.shape
    tr = R if R * C * 4 <= (1 << 20) else max(16, ((1 << 20) // (C * 4)) // 16 * 16)
    while R % tr:
        tr -= 16
    c1, c2 = 1.0 - ADAM_B1 ** ADAM_STEP, 1.0 - ADAM_B2 ** ADAM_STEP
    summed = land.ndim == 2

    def body(l_ref, w_ref, m_ref, v_ref, g_ref, d_ref, mo_ref, vo_ref):
        g = l_ref[...] if summed else l_ref[0].astype(F32)
        for k in range(1, 1 if summed else N_DEV):
            g = g + l_ref[k].astype(F32)
        mn = ADAM_B1 * m_ref[...] + (1.0 - ADAM_B1) * g
        vn = ADAM_B2 * v_ref[...] + (1.0 - ADAM_B2) * (g * g)
        g_ref[...] = g
        mo_ref[...] = mn
        vo_ref[...] = vn
        d_ref[...] = -ADAM_LR * ((mn / c1) / (jnp.sqrt(vn / c2) + ADAM_EPS) + ADAM_WD * w_ref[...])

    blk = pl.BlockSpec((tr, C), lambda i: (i, 0))
    lspec = blk if summed else pl.BlockSpec((N_DEV, tr, C), lambda i: (0, i, 0))
    return _call(body, name=name, grid=(R // tr,), in_specs=[lspec, blk, blk, blk],
                 out_specs=[blk] * 4, out_shape=[_sds((R, C), F32)] * 4, compiler_params=_cp(("parallel",)))(land, w, m, v)


def _pad_rows(a, rows):
    return jnp.pad(a, ((0, rows - a.shape[0]), (0, 0)))


def _local_step(x, mem, tgt, g_pre, w_main, w_if, b_if, wq, wk, w_dw, b_dw, g_ln, b_ln, w_conv_out, g_ml_head, w_ml_out,
                g_mem, w_mem_kv, w_xa_out, w_out, g_post):
    S, D = x.shape
    row = lambda a: a.reshape(1, -1)
    bif = jnp.pad(b_if, (0, 128 - b_if.shape[0])).reshape(1, 128)
    wdw, wq8, wk8 = _pad_rows(w_dw, HALO), _pad_rows(wq, QK_HALO), _pad_rows(wk, QK_HALO)

    h = _rms_fwd(x, row(g_pre), "rms_pre")
    P = _mm(h, w_main, "nn", F32, "proj_in")
    gif = _mm(h, w_if, "nn", F32, "proj_if")
    ca, u1 = _conv_fwd(P, wdw, row(b_dw), row(g_ln), row(b_ln), D)
    q, k, cq, ck = _qk_fwd(P, wq8, wk8, D, 3)
    gl = _gates_fwd(gif, bif)
    g8 = gl[:, :8]
    gt = g8.T
    hm, cs, ns, ms = _mlstm_fwd(q, k, P, g8, gt, D, 5)
    cb = _ml_post_fwd(hm, P, row(g_ml_head), D, 6, 7)
    mn = _rms_fwd(mem, row(g_mem), "rms_mem")
    kv = _mm(mn, w_mem_kv, "nn", BF16, "proj_kv")
    cc = _attn_fwd(P, kv, D, 8, 9)
    yc = _mm(ca, w_conv_out, "nn", F32, "out_conv")
    ym = _mm(cb, w_ml_out, "nn", F32, "out_ml")
    yx = _mm(cc, w_xa_out, "nn", F32, "out_xa")
    merged = _merge_fwd(yc, ym, yx, P, D, 10)
    o = _mm(merged, w_out, "nn", F32, "out_proj")
    dy, d_o, gg_post, sq = _post(o, x, tgt, row(g_post))

    d_merged = _mm(d_o, w_out, "nt", F32, "d_merged")
    gw_out = _mm(merged, d_o, "tn", F32, "gw_out")
    d_yc, d_ym, d_yx, d_gc, d_gm, d_gx = _merge_bwd(d_merged, yc, ym, yx, P, D, 10)
    d_ca = _mm(d_yc, w_conv_out, "nt", F32, "d_ca")
    gw_conv_out = _mm(ca, d_yc, "tn", F32, "gw_conv_out")
    d_cb = _mm(d_ym, w_ml_out, "nt", F32, "d_cb")
    gw_ml_out = _mm(cb, d_ym, "tn", F32, "gw_ml_out")
    d_cc = _mm(d_yx, w_xa_out, "nt", F32, "d_cc")
    gw_xa_out = _mm(cc, d_yx, "tn", F32, "gw_xa_out")

    d_u1, d_zc, gg_ln, gb_ln = _conv_bwd_local(d_ca, u1, P, row(g_ln), row(b_ln), D)
    d_a, d_b, gw_dw, gb_dw = _conv_bwd_taps(d_u1, P, wdw, D)

    d_qx, d_zx, d_kv = _attn_bwd(d_cc, P, kv, D, 8, 9)
    d_kvb = d_kv.astype(BF16)
    gw_mem_kv = _mm(mn, d_kvb, "tn", F32, "gw_mem_kv")
    d_mn = _mm(d_kvb, w_mem_kv, "nt", F32, "d_mn")
    _, gg_mem = _rms_bwd(mem, row(g_mem), [d_mn], None, "rms_mem_bwd")

    d_hm, d_om, d_zm, gg_ml = _ml_post_bwd(d_cb, hm, P, row(g_ml_head), D, 6, 7)
    dq, dk, d_v, dg = _mlstm_bwd(d_hm, q, k, P, g8, gt, cs, ns, ms, D, 5)
    d_qp, d_kp, gwq, gwk = _qk_bwd(dq, dk, cq, ck, P, wq8, wk8, D, 3)
    dgl = jnp.pad(jnp.concatenate([dg[:, 0, :].T, dg[:, 1, :].T], axis=1), ((0, 0), (0, 128 - 2 * N_HEADS)))
    d_gif, gb_if = _gates_bwd(dgl, gif, bif)

    dP = jnp.concatenate([d_a, d_b, d_zc, d_qp, d_kp, d_v, d_om, d_zm, d_qx, d_zx, d_gc, d_gm, d_gx], axis=1)
    d_h1 = _mm(dP, w_main, "nt", F32, "d_h_main")
    d_h2 = _mm(d_gif, w_if, "nt", F32, "d_h_if")
    gw_main = _mm(h, dP, "tn", F32, "gw_main")
    gw_if = _mm(h, d_gif, "tn", F32, "gw_if")
    grad_x, gg_pre = _rms_bwd(x, row(g_pre), [d_h1, d_h2], dy, "rms_pre_bwd")

    grads = dict(g_pre=gg_pre[0], w_main=gw_main, w_if=gw_if, b_if=gb_if[0, :2 * N_HEADS], wq=gwq[:QK_W], wk=gwk[:QK_W],
                 w_dw=gw_dw[:CONV_W], b_dw=gb_dw[0], g_ln=gg_ln[0], b_ln=gb_ln[0], w_conv_out=gw_conv_out,
                 g_ml_head=gg_ml[0], w_ml_out=gw_ml_out, g_mem=gg_mem[0], w_mem_kv=gw_mem_kv, w_xa_out=gw_xa_out,
                 w_out=gw_out, g_post=gg_post[0])
    return jnp.sum(sq), grad_x, grads


SMALL = ("g_pre", "b_if", "b_dw", "g_ln", "b_ln", "g_ml_head", "g_mem", "g_post")


def kernel(x, mem, g_pre, w_in, b_if, w_qk_conv, w_dw, b_dw, g_ln, b_ln, w_conv_out, g_ml_head, w_ml_out, g_mem, w_mem_kv, w_xa_out, w_out, g_post, loss_target, m_g_pre, m_w_in, m_b_if, m_w_qk_conv, m_w_dw, m_b_dw, m_g_ln, m_b_ln, m_w_conv_out, m_g_ml_head, m_w_ml_out, m_g_mem, m_w_mem_kv, m_w_xa_out, m_w_out, m_g_post, v_g_pre, v_w_in, v_b_if, v_w_qk_conv, v_w_dw, v_b_dw, v_g_ln, v_b_ln, v_w_conv_out, v_g_ml_head, v_w_ml_out, v_g_mem, v_w_mem_kv, v_w_xa_out, v_w_out, v_g_post):
    W = dict(g_pre=g_pre, w_in=w_in, b_if=b_if, w_qk_conv=w_qk_conv, w_dw=w_dw, b_dw=b_dw, g_ln=g_ln, b_ln=b_ln,
             w_conv_out=w_conv_out, g_ml_head=g_ml_head, w_ml_out=w_ml_out, g_mem=g_mem, w_mem_kv=w_mem_kv,
             w_xa_out=w_xa_out, w_out=w_out, g_post=g_post)
    Mo = dict(g_pre=m_g_pre, w_in=m_w_in, b_if=m_b_if, w_qk_conv=m_w_qk_conv, w_dw=m_w_dw, b_dw=m_b_dw, g_ln=m_g_ln,
              b_ln=m_b_ln, w_conv_out=m_w_conv_out, g_ml_head=m_g_ml_head, w_ml_out=m_w_ml_out, g_mem=m_g_mem,
              w_mem_kv=m_w_mem_kv, w_xa_out=m_w_xa_out, w_out=m_w_out, g_post=m_g_post)
    Vo = dict(g_pre=v_g_pre, w_in=v_w_in, b_if=v_b_if, w_qk_conv=v_w_qk_conv, w_dw=v_w_dw, b_dw=v_b_dw, g_ln=v_g_ln,
              b_ln=v_b_ln, w_conv_out=v_w_conv_out, g_ml_head=v_g_ml_head, w_ml_out=v_w_ml_out, g_mem=v_g_mem,
              w_mem_kv=v_w_mem_kv, w_xa_out=v_w_xa_out, w_out=v_w_out, g_post=v_g_post)
    D = x.shape[-1]
    n_in = 4 * w_in.shape[1]
    off_if = 8 * D

    big = ("w_in", "w_conv_out", "w_ml_out", "w_mem_kv", "w_xa_out", "w_out")
    halves = lambda a: a.reshape(2, a.shape[0] // 2, a.shape[1])
    got = _gather_chips([halves(W[n].astype(BF16)) for n in big], [w_qk_conv, w_dw])
    got = [g.reshape(4, -1, g.shape[-1]) for g in got]
    cols = lambda a: jnp.transpose(a, (1, 0, 2)).reshape(a.shape[1], -1)
    rows = lambda a: a.reshape(-1, a.shape[2])
    w_in_f = cols(got[0])
    w_main = jnp.concatenate([w_in_f[:, :off_if], w_in_f[:, off_if + 2 * N_HEADS:]], axis=1)
    w_if = jnp.pad(w_in_f[:, off_if:off_if + 2 * N_HEADS], ((0, 0), (0, 128 - 2 * N_HEADS)))
    wqk_f = cols(got[6])

    sq, grad_x, G = _local_step(
        x[0], mem[0], loss_target[0], g_pre, w_main, w_if, b_if, wqk_f[:, :D], wqk_f[:, D:], cols(got[7]), b_dw, g_ln, b_ln,
        rows(got[1]), g_ml_head, rows(got[2]), g_mem, cols(got[3]), rows(got[4]), rows(got[5]), g_post)
    loss = lax.psum(0.5 * sq / D, ("x", "y", "c"))

    colblk = lambda a: jnp.transpose(a.reshape(a.shape[0], 4, -1), (1, 0, 2))
    colblk_h = lambda a: jnp.transpose(a.reshape(2, a.shape[0] // 2, 4, -1), (0, 2, 1, 3))
    rowblk_h = lambda a: jnp.transpose(a.reshape(4, 2, -1, a.shape[1]), (1, 0, 2, 3))
    gw_in = jnp.concatenate([G["w_main"][:, :off_if], G["w_if"][:, :2 * N_HEADS], G["w_main"][:, off_if:]], axis=1)
    assert gw_in.shape[1] == n_in
    gh = dict(w_in=colblk_h(gw_in), w_conv_out=rowblk_h(G["w_conv_out"]), w_ml_out=rowblk_h(G["w_ml_out"]),
              w_mem_kv=colblk_h(G["w_mem_kv"]), w_xa_out=rowblk_h(G["w_xa_out"]), w_out=rowblk_h(G["w_out"]))
    c = lax.axis_index("c")
    gh = {n: a.astype(BF16) for n, a in gh.items()}
    mine = [lax.dynamic_index_in_dim(gh[n], c, 0, keepdims=False) for n in big]
    theirs = _to_sibling([lax.dynamic_index_in_dim(gh[n], 1 - c, 0, keepdims=False) for n in big], "pair_swap")
    chip_sums = [_add_pair(a, b, "chip_sum_" + n) for n, a, b in zip(big, mine, theirs)]
    landed = _chip_scatter(chip_sums)
    summed = _share_halves([_sum_chips(a, "sum_chips_" + n) for n, a in zip(big, landed)])
    out = {}
    for n, g in zip(big, summed):
        out[n] = _adamw(g.reshape(-1, g.shape[-1]), W[n], Mo[n], Vo[n], "adamw_" + n)

    pad = lambda a: jnp.pad(a, (0, D - a.shape[0]))
    small_g = jnp.stack([pad(G[n]) for n in SMALL])
    lands = _scatter_grads([colblk(jnp.concatenate([G["wq"], G["wk"]], axis=1)), colblk(G["w_dw"])], small_g)
    for n, land in zip(("w_qk_conv", "w_dw"), lands[:-1]):
        out[n] = _adamw(land, W[n], Mo[n], Vo[n], "adamw_" + n)
    res = _adamw(lands[-1], jnp.stack([pad(W[n]) for n in SMALL]), jnp.stack([pad(Mo[n]) for n in SMALL]),
                 jnp.stack([pad(Vo[n]) for n in SMALL]), "adamw_small")
    for i, n in enumerate(SMALL):
        out[n] = tuple(r[i, :W[n].shape[0]] for r in res)
    order = ("g_pre", "w_in", "b_if", "w_qk_conv", "w_dw", "b_dw", "g_ln", "b_ln", "w_conv_out", "g_ml_head", "w_ml_out",
             "g_mem", "w_mem_kv", "w_xa_out", "w_out", "g_post")
    return (loss, grad_x[None], *[out[n][0] for n in order], *[out[n][1] for n in order], *[out[n][2] for n in order],
            *[out[n][3] for n in order])
```

```python
import jax
import jax.numpy as jnp
from jax import lax
from jax.experimental import pallas as pl
from jax.experimental.pallas import tpu as pltpu

F32, BF16 = jnp.float32, jnp.bfloat16
EPS = 1e-6
N_HEADS = 4
CONV_W = 31
QK_W = 4
HALO = 32
QK_HALO = 8
ROW_TILE = 256
ML_CHUNK = 256
VMEM_LIMIT = 56 * 1024 * 1024
NEG = -1e30
ADAM_LR, ADAM_B1, ADAM_B2, ADAM_EPS, ADAM_WD, ADAM_STEP = 0.001, 0.9, 0.999, 1e-08, 0.01, 10
MESH = pl.DeviceIdType.MESH
N_DEV = 8

NN = (((1,), (0,)), ((), ()))
NT = (((1,), (1,)), ((), ()))
TN = (((0,), (0,)), ((), ()))


def _call(body, **kw):
    return pl.pallas_call(body, **kw)


def _cp(sem):
    return pltpu.CompilerParams(dimension_semantics=sem, vmem_limit_bytes=VMEM_LIMIT)


def _sds(shape, dt):
    return jax.ShapeDtypeStruct(shape, dt)


def _row(tm, d, col=0):
    return pl.BlockSpec((tm, d), lambda i: (i, col))


def _par(r, d):
    return pl.BlockSpec((r, d), lambda i: (0, 0))


def _prev(tm, hb, d, col=0):
    return pl.BlockSpec((hb, d), lambda i: (jnp.maximum(i * (tm // hb) - 1, 0), col))


def _next(tm, hb, d, nblk, col=0):
    return pl.BlockSpec((hb, d), lambda i: (jnp.minimum((i + 1) * (tm // hb), nblk - 1), col))


def _dot(a, b, dn):
    return lax.dot_general(a.astype(BF16), b.astype(BF16), dn, preferred_element_type=F32)


def _sg(x):
    return jax.nn.sigmoid(x)


def _dsilu(z, s):
    return s * (1.0 + z * (1.0 - s))


def _mean(x):
    return jnp.mean(x, axis=-1, keepdims=True)


def _pick(n, pref):
    if n <= pref:
        return n
    t = pref
    while n % t:
        t -= 128
    return t


def _mm(a, b, mode, out_dtype, name, tm=1024, tn=1024, tk=2048):
    if mode == "nn":
        (M, K), N = a.shape, b.shape[1]
    elif mode == "nt":
        (M, K), N = a.shape, b.shape[0]
    else:
        (K, M), N = a.shape, b.shape[1]
    tm, tn, tk = _pick(M, tm), _pick(N, tn), _pick(K, tk)
    nk = K // tk
    if mode == "nn":
        sa, sb, dn = pl.BlockSpec((tm, tk), lambda i, j, k: (i, k)), pl.BlockSpec((tk, tn), lambda i, j, k: (k, j)), NN
    elif mode == "nt":
        sa, sb, dn = pl.BlockSpec((tm, tk), lambda i, j, k: (i, k)), pl.BlockSpec((tn, tk), lambda i, j, k: (j, k)), NT
    else:
        sa, sb, dn = pl.BlockSpec((tk, tm), lambda i, j, k: (k, i)), pl.BlockSpec((tk, tn), lambda i, j, k: (k, j)), TN

    def body(a_ref, b_ref, o_ref, *acc):
        p = _dot(a_ref[...], b_ref[...], dn)
        if nk == 1:
            o_ref[...] = p.astype(out_dtype)
        else:
            acc_ref, k = acc[0], pl.program_id(2)

            @pl.when(k == 0)
            def _():
                acc_ref[...] = p

            @pl.when(k > 0)
            def _():
                acc_ref[...] += p

            @pl.when(k == nk - 1)
            def _():
                o_ref[...] = acc_ref[...].astype(out_dtype)

    return _call(body, name=name, grid=(M // tm, N // tn, nk), in_specs=[sa, sb],
                 out_specs=pl.BlockSpec((tm, tn), lambda i, j, k: (i, j)), out_shape=_sds((M, N), out_dtype),
                 scratch_shapes=[pltpu.VMEM((tm, tn), F32)] if nk > 1 else [],
                 compiler_params=_cp(("parallel", "parallel", "arbitrary")))(a, b)


def _rms_fwd(x, g, name):
    S, D = x.shape
    tm = min(ROW_TILE, S)

    def body(x_ref, g_ref, h_ref):
        xv = x_ref[...]
        r = lax.rsqrt(_mean(xv * xv) + EPS)
        h_ref[...] = (xv * r * g_ref[...]).astype(BF16)

    return _call(body, name=name, grid=(S // tm,), in_specs=[_row(tm, D), _par(1, D)], out_specs=_row(tm, D),
                 out_shape=_sds((S, D), BF16), compiler_params=_cp(("parallel",)))(x, g)


def _rms_bwd(x, g, ds, resid, name):
    S, D = x.shape
    tm = min(ROW_TILE, S)
    nd = len(ds)

    def body(*refs):
        x_ref, g_ref = refs[:2]
        d_refs = refs[2:2 + nd]
        r_ref = refs[2 + nd] if resid is not None else None
        dx_ref, dg_ref = refs[-2:]
        xv = x_ref[...]
        d = d_refs[0][...]
        for dr in d_refs[1:]:
            d = d + dr[...]
        r = lax.rsqrt(_mean(xv * xv) + EPS)
        dxh = d * g_ref[...]
        dx = r * dxh - xv * (r * r * r) * _mean(dxh * xv)
        if r_ref is not None:
            dx = dx + r_ref[...]
        dx_ref[...] = dx

        @pl.when(pl.program_id(0) == 0)
        def _():
            dg_ref[...] = jnp.zeros_like(dg_ref)

        dg_ref[...] += jnp.sum(d * xv * r, axis=0, keepdims=True)

    ins = [x, g, *ds] + ([resid] if resid is not None else [])
    specs = [_row(tm, D), _par(1, D)] + [_row(tm, D)] * (len(ins) - 2)
    return _call(body, name=name, grid=(S // tm,), in_specs=specs, out_specs=[_row(tm, D), _par(1, D)],
                 out_shape=[_sds((S, D), F32), _sds((1, D), F32)], compiler_params=_cp(("arbitrary",)))(*ins)


def _conv_fwd(P, w_dw, b_dw, g_ln, b_ln, D):
    S = P.shape[0]
    tm = min(ROW_TILE, S)

    def body(a_ref, b_ref, ap_ref, bp_ref, z_ref, w_ref, bd_ref, g_ref, bl_ref, ca_ref, u1_ref, buf):
        first = pl.program_id(0) == 0
        hp = ap_ref[...] * _sg(bp_ref[...])
        buf[0:HALO, :] = jnp.where(first, 0.0, hp)
        buf[HALO:HALO + tm, :] = a_ref[...] * _sg(b_ref[...])
        acc = jnp.zeros((tm, D), F32)
        for j in range(CONV_W):
            acc = acc + w_ref[j:j + 1, :] * buf[HALO - CONV_W + 1 + j:HALO - CONV_W + 1 + j + tm, :]
        u1 = acc + bd_ref[...]
        u1_ref[...] = u1
        xc = u1 - _mean(u1)
        r = lax.rsqrt(_mean(xc * xc) + EPS)
        u2 = xc * r * g_ref[...] + bl_ref[...]
        z = z_ref[...]
        ca_ref[...] = (u2 * _sg(u2) * (z * _sg(z))).astype(BF16)

    return _call(body, name="conv_fwd", grid=(S // tm,),
                 in_specs=[_row(tm, D, 0), _row(tm, D, 1), _prev(tm, HALO, D, 0), _prev(tm, HALO, D, 1), _row(tm, D, 2),
                           _par(HALO, D), _par(1, D), _par(1, D), _par(1, D)],
                 out_specs=[_row(tm, D), _row(tm, D)], out_shape=[_sds((S, D), BF16), _sds((S, D), F32)],
                 scratch_shapes=[pltpu.VMEM((tm + HALO, D), F32)], compiler_params=_cp(("parallel",)))(
        P, P, P, P, P, w_dw, b_dw, g_ln, b_ln)


def _conv_bwd_local(d_ca, u1, P, g_ln, b_ln, D):
    S = P.shape[0]
    tm = min(ROW_TILE, S)

    def body(dca_ref, u1_ref, z_ref, g_ref, bl_ref, du1_ref, dz_ref, dg_ref, db_ref):
        u1, z, dca, g = u1_ref[...], z_ref[...], dca_ref[...], g_ref[...]
        xc = u1 - _mean(u1)
        r = lax.rsqrt(_mean(xc * xc) + EPS)
        xh = xc * r
        u2 = xh * g + bl_ref[...]
        s2, sz = _sg(u2), _sg(z)
        d_u3 = dca * (z * sz)
        dz_ref[...] = (dca * (u2 * s2) * _dsilu(z, sz)).astype(BF16)
        d_u2 = d_u3 * _dsilu(u2, s2)
        dxh = d_u2 * g
        du1_ref[...] = r * (dxh - _mean(dxh) - xh * _mean(dxh * xh))

        @pl.when(pl.program_id(0) == 0)
        def _():
            dg_ref[...] = jnp.zeros_like(dg_ref)
            db_ref[...] = jnp.zeros_like(db_ref)

        dg_ref[...] += jnp.sum(d_u2 * xh, axis=0, keepdims=True)
        db_ref[...] += jnp.sum(d_u2, axis=0, keepdims=True)

    return _call(body, name="conv_bwd_local", grid=(S // tm,),
                 in_specs=[_row(tm, D), _row(tm, D), _row(tm, D, 2), _par(1, D), _par(1, D)],
                 out_specs=[_row(tm, D), _row(tm, D), _par(1, D), _par(1, D)],
                 out_shape=[_sds((S, D), F32), _sds((S, D), BF16), _sds((1, D), F32), _sds((1, D), F32)],
                 compiler_params=_cp(("arbitrary",)))(d_ca, u1, P, g_ln, b_ln)


def _conv_bwd_taps(d_u1, P, w_dw, D):
    S = P.shape[0]
    tm = min(ROW_TILE, S)
    nblk = S // HALO

    def body(d_ref, dn_ref, a_ref, b_ref, ap_ref, bp_ref, w_ref, da_ref, db_ref, gw_ref, gb_ref, dbuf, ubuf):
        i = pl.program_id(0)
        a, sb = a_ref[...], _sg(b_ref[...])
        d = d_ref[...]
        dbuf[0:tm, :] = d
        dbuf[tm:tm + HALO, :] = jnp.where(i == pl.num_programs(0) - 1, 0.0, dn_ref[...])
        ubuf[0:HALO, :] = jnp.where(i == 0, 0.0, ap_ref[...] * _sg(bp_ref[...]))
        ubuf[HALO:HALO + tm, :] = a * sb

        @pl.when(i == 0)
        def _():
            gw_ref[...] = jnp.zeros_like(gw_ref)
            gb_ref[...] = jnp.zeros_like(gb_ref)

        acc = jnp.zeros((tm, D), F32)
        for j in range(CONV_W):
            acc = acc + w_ref[j:j + 1, :] * dbuf[CONV_W - 1 - j:CONV_W - 1 - j + tm, :]
            sh = HALO - CONV_W + 1 + j
            gw_ref[j:j + 1, :] += jnp.sum(d * ubuf[sh:sh + tm, :], axis=0, keepdims=True)
        gb_ref[...] += jnp.sum(d, axis=0, keepdims=True)
        da_ref[...] = (acc * sb).astype(BF16)
        db_ref[...] = (acc * a * sb * (1.0 - sb)).astype(BF16)

    return _call(body, name="conv_bwd_taps", grid=(S // tm,),
                 in_specs=[_row(tm, D), _next(tm, HALO, D, nblk), _row(tm, D, 0), _row(tm, D, 1),
                           _prev(tm, HALO, D, 0), _prev(tm, HALO, D, 1), _par(HALO, D)],
                 out_specs=[_row(tm, D), _row(tm, D), _par(HALO, D), _par(1, D)],
                 out_shape=[_sds((S, D), BF16), _sds((S, D), BF16), _sds((HALO, D), F32), _sds((1, D), F32)],
                 scratch_shapes=[pltpu.VMEM((tm + HALO, D), F32), pltpu.VMEM((tm + HALO, D), F32)],
                 compiler_params=_cp(("arbitrary",)))(d_u1, d_u1, P, P, P, P, w_dw)


def _qk_fwd(P, wq, wk, D, col):
    S = P.shape[0]
    tm = min(ROW_TILE, S)
    scale = float(D // N_HEADS) ** -0.5

    def body(q_ref, k_ref, qp_ref, kp_ref, wq_ref, wk_ref, qo_ref, ko_ref, cq_ref, ck_ref, buf):
        first = pl.program_id(0) == 0
        for x_ref, p_ref, w_ref, o_ref, c_ref, sc in ((q_ref, qp_ref, wq_ref, qo_ref, cq_ref, 1.0),
                                                      (k_ref, kp_ref, wk_ref, ko_ref, ck_ref, scale)):
            buf[0:QK_HALO, :] = jnp.where(first, 0.0, p_ref[...])
            buf[QK_HALO:QK_HALO + tm, :] = x_ref[...]
            acc = jnp.zeros((tm, D), F32)
            for j in range(QK_W):
                sh = QK_HALO - QK_W + 1 + j
                acc = acc + w_ref[j:j + 1, :] * buf[sh:sh + tm, :]
            c_ref[...] = acc
            o_ref[...] = acc * _sg(acc) * sc

    return _call(body, name="qk_fwd", grid=(S // tm,),
                 in_specs=[_row(tm, D, col), _row(tm, D, col + 1), _prev(tm, QK_HALO, D, col), _prev(tm, QK_HALO, D, col + 1),
                           _par(QK_HALO, D), _par(QK_HALO, D)],
                 out_specs=[_row(tm, D)] * 4, out_shape=[_sds((S, D), F32)] * 4,
                 scratch_shapes=[pltpu.VMEM((tm + QK_HALO, D), F32)], compiler_params=_cp(("parallel",)))(P, P, P, P, wq, wk)


def _qk_bwd(dq, dk, cq, ck, P, wq, wk, D, col):
    S = P.shape[0]
    tm = min(ROW_TILE, S)
    nblk = S // QK_HALO
    scale = float(D // N_HEADS) ** -0.5

    def body(dq_ref, dqn_ref, cq_ref, cqn_ref, xq_ref, xqp_ref, wq_ref,
             dk_ref, dkn_ref, ck_ref, ckn_ref, xk_ref, xkp_ref, wk_ref,
             oq_ref, ok_ref, gq_ref, gk_ref, dbuf, xbuf):
        i = pl.program_id(0)
        last = i == pl.num_programs(0) - 1

        @pl.when(i == 0)
        def _():
            gq_ref[...] = jnp.zeros_like(gq_ref)
            gk_ref[...] = jnp.zeros_like(gk_ref)

        for d_ref, dn_ref, c_ref, cn_ref, x_ref, xp_ref, w_ref, o_ref, g_ref, sc in (
                (dq_ref, dqn_ref, cq_ref, cqn_ref, xq_ref, xqp_ref, wq_ref, oq_ref, gq_ref, 1.0),
                (dk_ref, dkn_ref, ck_ref, ckn_ref, xk_ref, xkp_ref, wk_ref, ok_ref, gk_ref, scale)):
            c, cn = c_ref[...], cn_ref[...]
            dc = d_ref[...] * sc * _dsilu(c, _sg(c))
            dcn = dn_ref[...] * sc * _dsilu(cn, _sg(cn))
            dbuf[0:tm, :] = dc
            dbuf[tm:tm + QK_HALO, :] = jnp.where(last, 0.0, dcn)
            xbuf[0:QK_HALO, :] = jnp.where(i == 0, 0.0, xp_ref[...])
            xbuf[QK_HALO:QK_HALO + tm, :] = x_ref[...]
            acc = jnp.zeros((tm, D), F32)
            for j in range(QK_W):
                acc = acc + w_ref[j:j + 1, :] * dbuf[QK_W - 1 - j:QK_W - 1 - j + tm, :]
                sh = QK_HALO - QK_W + 1 + j
                g_ref[j:j + 1, :] += jnp.sum(dc * xbuf[sh:sh + tm, :], axis=0, keepdims=True)
            o_ref[...] = acc.astype(BF16)

    one = [_row(tm, D), _next(tm, QK_HALO, D, nblk)]
    specs = (one + one + [_row(tm, D, col), _prev(tm, QK_HALO, D, col), _par(QK_HALO, D)]
             + one + one + [_row(tm, D, col + 1), _prev(tm, QK_HALO, D, col + 1), _par(QK_HALO, D)])
    return _call(body, name="qk_bwd", grid=(S // tm,), in_specs=specs,
                 out_specs=[_row(tm, D), _row(tm, D), _par(QK_HALO, D), _par(QK_HALO, D)],
                 out_shape=[_sds((S, D), BF16), _sds((S, D), BF16), _sds((QK_HALO, D), F32), _sds((QK_HALO, D), F32)],
                 scratch_shapes=[pltpu.VMEM((tm + QK_HALO, D), F32), pltpu.VMEM((tm + QK_HALO, D), F32)],
                 compiler_params=_cp(("arbitrary",)))(dq, dq, cq, cq, P, P, wq, dk, dk, ck, ck, P, P, wk)


def _gates_fwd(gif, b_if):
    S = gif.shape[0]
    tm = min(ROW_TILE, S)

    def body(g_ref, b_ref, o_ref):
        z = g_ref[...] + b_ref[...]
        lane = lax.broadcasted_iota(jnp.int32, z.shape, 1)
        ls = jnp.minimum(z, 0.0) - jnp.log(1.0 + jnp.exp(-jnp.abs(z)))
        o_ref[...] = jnp.where(lane < N_HEADS, z, jnp.where(lane < 2 * N_HEADS, ls, 0.0))

    return _call(body, name="gates_fwd", grid=(S // tm,), in_specs=[_row(tm, 128), _par(1, 128)], out_specs=_row(tm, 128),
                 out_shape=_sds((S, 128), F32), compiler_params=_cp(("parallel",)))(gif, b_if)


def _gates_bwd(dgl, gif, b_if):
    S = gif.shape[0]
    tm = min(ROW_TILE, S)

    def body(d_ref, g_ref, b_ref, o_ref, gb_ref):
        z = g_ref[...] + b_ref[...]
        lane = lax.broadcasted_iota(jnp.int32, z.shape, 1)
        d = d_ref[...]
        dz = jnp.where(lane < N_HEADS, d, jnp.where(lane < 2 * N_HEADS, d * _sg(-z), 0.0))
        o_ref[...] = dz.astype(BF16)

        @pl.when(pl.program_id(0) == 0)
        def _():
            gb_ref[...] = jnp.zeros_like(gb_ref)

        gb_ref[...] += jnp.sum(dz, axis=0, keepdims=True)

    return _call(body, name="gates_bwd", grid=(S // tm,), in_specs=[_row(tm, 128), _row(tm, 128), _par(1, 128)],
                 out_specs=[_row(tm, 128), _par(1, 128)], out_shape=[_sds((S, 128), BF16), _sds((1, 128), F32)],
                 compiler_params=_cp(("arbitrary",)))(dgl, gif, b_if)


def _chunk_gates(g8_ref, gt_ref, h, L):
    g8, gt = g8_ref[...], gt_ref[...]
    lane = lax.broadcasted_iota(jnp.int32, g8.shape, 1)
    sub = lax.broadcasted_iota(jnp.int32, gt.shape, 0)
    li_col = jnp.sum(jnp.where(lane == h, g8, 0.0), axis=1, keepdims=True)
    lf_col = jnp.sum(jnp.where(lane == h + N_HEADS, g8, 0.0), axis=1, keepdims=True)
    li_row = jnp.sum(jnp.where(sub == h, gt, 0.0), axis=0, keepdims=True)
    lf_row = jnp.sum(jnp.where(sub == h + N_HEADS, gt, 0.0), axis=0, keepdims=True)
    t = lax.broadcasted_iota(jnp.int32, (L, L), 0)
    s = lax.broadcasted_iota(jnp.int32, (L, L), 1)
    causal = s <= t
    b_col = jnp.sum(jnp.where(causal, lf_row, 0.0), axis=1, keepdims=True)
    b_row = jnp.sum(jnp.where(t <= s, lf_col, 0.0), axis=0, keepdims=True)
    return li_col, li_row, b_col, b_row, causal


def _chunk_fwd(q, k, v, C, n, m, li_row, b_col, b_row, causal):
    d = jnp.where(causal, b_col - b_row + li_row, NEG)
    inter = b_col + m
    m_row = jnp.maximum(inter, jnp.max(d, axis=1, keepdims=True))
    wi = jnp.exp(d - m_row)
    wn = jnp.exp(inter - m_row)
    s = _dot(q, k, NT) * wi
    num = _dot(s, v, NN) + wn * _dot(q, C, NN)
    den = jnp.sum(s, axis=1, keepdims=True) + wn * jnp.sum(q * n, axis=1, keepdims=True)
    e = jnp.exp(-m_row)
    inv = 1.0 / jnp.maximum(jnp.abs(den), e)
    return wi, wn, s, num * inv, den, e, inv


def _chunk_state(li_col, li_row, b_col, b_row, m, L):
    blast = b_col[L - 1:L, :]
    g_col = blast - b_col + li_col
    g_row = blast - b_row + li_row
    m_new = jnp.maximum(blast + m, jnp.max(g_row, axis=1, keepdims=True))
    decay = jnp.exp(blast + m - m_new)
    wk_col = jnp.exp(g_col - m_new)
    return m_new, decay, wk_col


def _mlstm_fwd(q, k, P, g8, gt, D, vcol):
    S = q.shape[0]
    H, dh = N_HEADS, D // N_HEADS
    L = min(ML_CHUNK, S)
    nc = S // L

    def body(q_ref, k_ref, v_ref, g8_ref, gt_ref, h_ref, cs_ref, ns_ref, ms_ref, C, n, m):
        h, j = pl.program_id(0), pl.program_id(1)

        @pl.when(j == 0)
        def _():
            C[...] = jnp.zeros_like(C)
            n[...] = jnp.zeros_like(n)
            m[...] = jnp.zeros_like(m)

        qv, kv, vv = q_ref[...], k_ref[...], v_ref[...]
        Cv, nv, mv = C[...], n[...], m[:, 0:1]
        cs_ref[...] = Cv.astype(BF16)
        ns_ref[...] = nv
        ms_ref[...] = m[...]
        li_col, li_row, b_col, b_row, causal = _chunk_gates(g8_ref, gt_ref, h, L)
        _, _, _, hv, _, _, _ = _chunk_fwd(qv, kv, vv, Cv, nv, mv, li_row, b_col, b_row, causal)
        h_ref[...] = hv
        m_new, decay, wk_col = _chunk_state(li_col, li_row, b_col, b_row, mv, L)
        kw = kv * wk_col
        C[...] = decay * Cv + _dot(kw, vv, TN)
        n[...] = decay * nv + jnp.sum(kw, axis=0, keepdims=True)
        m[...] = jnp.broadcast_to(m_new, m.shape)

    blk = lambda c0: pl.BlockSpec((L, dh), lambda h, j: (j, c0 + h))
    return _call(body, name="mlstm_fwd", grid=(H, nc),
                 in_specs=[blk(0), blk(0), blk(vcol * H), pl.BlockSpec((L, 8), lambda h, j: (j, 0)),
                           pl.BlockSpec((8, L), lambda h, j: (0, j))],
                 out_specs=[blk(0), pl.BlockSpec((None, None, dh, dh), lambda h, j: (h, j, 0, 0)),
                            pl.BlockSpec((None, None, 1, dh), lambda h, j: (h, j, 0, 0)),
                            pl.BlockSpec((None, None, 1, 128), lambda h, j: (h, j, 0, 0))],
                 out_shape=[_sds((S, D), F32), _sds((H, nc, dh, dh), BF16), _sds((H, nc, 1, dh), F32),
                            _sds((H, nc, 1, 128), F32)],
                 scratch_shapes=[pltpu.VMEM((dh, dh), F32), pltpu.VMEM((1, dh), F32), pltpu.VMEM((1, 128), F32)],
                 compiler_params=_cp(("arbitrary", "arbitrary")))(q, k, P, g8, gt)


def _mlstm_bwd(dhm, q, k, P, g8, gt, cs, ns, ms, D, vcol):
    S = q.shape[0]
    H, dh = N_HEADS, D // N_HEADS
    L = min(ML_CHUNK, S)
    nc = S // L

    def body(dh_ref, q_ref, k_ref, v_ref, g8_ref, gt_ref, cs_ref, ns_ref, ms_ref,
             dq_ref, dk_ref, dv_ref, dg_ref, dC, dnv):
        h, j = pl.program_id(0), pl.program_id(1)

        @pl.when(j == 0)
        def _():
            dC[...] = jnp.zeros_like(dC)
            dnv[...] = jnp.zeros_like(dnv)

        qv, kv, vv, dhv = q_ref[...], k_ref[...], v_ref[...], dh_ref[...]
        Cv, nv, mv = cs_ref[...], ns_ref[...], ms_ref[:, 0:1]
        li_col, li_row, b_col, b_row, causal = _chunk_gates(g8_ref, gt_ref, h, L)
        wi, wn, s, hv, den, e, inv = _chunk_fwd(qv, kv, vv, Cv, nv, mv, li_row, b_col, b_row, causal)
        dn = dhv * inv
        dd = -jnp.sum(dhv * hv, axis=1, keepdims=True) * inv * jnp.where(jnp.abs(den) > e, jnp.sign(den), 0.0)
        dS = _dot(dn, vv, NT) + dd
        dqk = dS * wi
        dq_i = wn * (_dot(dn, Cv, NT) + dd * nv)
        dq = _dot(dqk, kv, NN) + dq_i
        dv = _dot(s, dn, TN)
        m_new, decay, wk_col = _chunk_state(li_col, li_row, b_col, b_row, mv, L)
        dCv, dnvv = dC[...], dnv[...]
        dk_i = wk_col * (_dot(vv, dCv, NT) + dnvv)
        dk = _dot(dqk, qv, TN) + dk_i
        dv = dv + _dot(kv * wk_col, dCv, NN)
        qw = qv * wn
        dC[...] = decay * dCv + _dot(qw, dn, TN)
        dnv[...] = decay * dnvv + jnp.sum(qw * dd, axis=0, keepdims=True)
        dq_ref[...] = dq
        dk_ref[...] = dk
        dv_ref[...] = dv.astype(BF16)
        G = dS * s
        t = lax.broadcasted_iota(jnp.int32, (L, L), 0)
        r = lax.broadcasted_iota(jnp.int32, (L, L), 1)
        eye = t == r
        col_row = jnp.sum(G, axis=0, keepdims=True)
        col_g = jnp.sum(jnp.where(eye, col_row, 0.0), axis=1, keepdims=True)
        row_i = jnp.sum(qv * dq_i, axis=1, keepdims=True)
        col_i = jnp.sum(kv * dk_i, axis=1, keepdims=True)
        dF = jnp.sum(G, axis=1, keepdims=True) - col_g + row_i
        across = decay * (jnp.sum(jnp.sum(dCv * Cv.astype(F32), axis=0, keepdims=True), axis=1, keepdims=True)
                          + jnp.sum(dnvv * nv, axis=1, keepdims=True))
        dlf = jnp.sum(jnp.where(t >= r, dF, 0.0) + jnp.where(t < r, col_i, 0.0), axis=0, keepdims=True) + across
        dli = col_row + jnp.sum(jnp.where(eye, col_i, 0.0), axis=0, keepdims=True)
        sub = lax.broadcasted_iota(jnp.int32, (8, L), 0)
        dg_ref[...] = jnp.where(sub == 0, dli, jnp.where(sub == 1, dlf, 0.0))

    blk = lambda c0: pl.BlockSpec((L, dh), lambda h, j: (nc - 1 - j, c0 + h))
    st = lambda r, c: pl.BlockSpec((None, None, r, c), lambda h, j: (h, nc - 1 - j, 0, 0))
    return _call(body, name="mlstm_bwd", grid=(H, nc),
                 in_specs=[blk(0), blk(0), blk(0), blk(vcol * H), pl.BlockSpec((L, 8), lambda h, j: (nc - 1 - j, 0)),
                           pl.BlockSpec((8, L), lambda h, j: (0, nc - 1 - j)), st(dh, dh), st(1, dh), st(1, 128)],
                 out_specs=[blk(0), blk(0), blk(0), pl.BlockSpec((None, 8, L), lambda h, j: (h, 0, nc - 1 - j))],
                 out_shape=[_sds((S, D), F32), _sds((S, D), F32), _sds((S, D), BF16), _sds((H, 8, S), F32)],
                 scratch_shapes=[pltpu.VMEM((dh, dh), F32), pltpu.VMEM((1, dh), F32)],
                 compiler_params=_cp(("arbitrary", "arbitrary")))(dhm, q, k, P, g8, gt, cs, ns, ms)


def _ml_post_fwd(hm, P, g, D, ocol, zcol):
    S = hm.shape[0]
    tm = min(ROW_TILE, S)
    dh = D // N_HEADS

    def body(h_ref, o_ref, z_ref, g_ref, cb_ref):
        for hd in range(N_HEADS):
            sl = slice(hd * dh, (hd + 1) * dh)
            x = _sg(o_ref[:, sl]) * h_ref[:, sl]
            r = lax.rsqrt(_mean(x * x) + EPS)
            z = z_ref[:, sl]
            cb_ref[:, sl] = (x * r * g_ref[:, sl] * (z * _sg(z))).astype(BF16)

    return _call(body, name="ml_post_fwd", grid=(S // tm,),
                 in_specs=[_row(tm, D), _row(tm, D, ocol), _row(tm, D, zcol), _par(1, D)], out_specs=_row(tm, D),
                 out_shape=_sds((S, D), BF16), compiler_params=_cp(("parallel",)))(hm, P, P, g)


def _ml_post_bwd(d_cb, hm, P, g, D, ocol, zcol):
    S = hm.shape[0]
    tm = min(ROW_TILE, S)
    dh = D // N_HEADS

    def body(d_ref, h_ref, o_ref, z_ref, g_ref, dh_ref, do_ref, dz_ref, dg_ref):
        @pl.when(pl.program_id(0) == 0)
        def _():
            dg_ref[...] = jnp.zeros_like(dg_ref)

        for hd in range(N_HEADS):
            sl = slice(hd * dh, (hd + 1) * dh)
            hv, z, gv, d = h_ref[:, sl], z_ref[:, sl], g_ref[:, sl], d_ref[:, sl]
            so, sz = _sg(o_ref[:, sl]), _sg(z)
            x = so * hv
            r = lax.rsqrt(_mean(x * x) + EPS)
            xh = x * r
            d3 = d * (z * sz)
            dz_ref[:, sl] = (d * xh * gv * _dsilu(z, sz)).astype(BF16)
            dg_ref[:, sl] += jnp.sum(d3 * xh, axis=0, keepdims=True)
            dxh = d3 * gv
            d2 = r * (dxh - xh * _mean(dxh * xh))
            do_ref[:, sl] = (d2 * hv * so * (1.0 - so)).astype(BF16)
            dh_ref[:, sl] = d2 * so

    return _call(body, name="ml_post_bwd", grid=(S // tm,),
                 in_specs=[_row(tm, D), _row(tm, D), _row(tm, D, ocol), _row(tm, D, zcol), _par(1, D)],
                 out_specs=[_row(tm, D), _row(tm, D), _row(tm, D), _par(1, D)],
                 out_shape=[_sds((S, D), F32), _sds((S, D), BF16), _sds((S, D), BF16), _sds((1, D), F32)],
                 compiler_params=_cp(("arbitrary",)))(d_cb, hm, P, P, g)


def _attn_scores(q, kh, scale):
    sc = _dot(q, kh, NT) * scale
    ex = jnp.exp(sc - jnp.max(sc, axis=1, keepdims=True))
    return ex / jnp.sum(ex, axis=1, keepdims=True)


def _attn_fwd(P, kv, D, qcol, zcol):
    S, M = P.shape[0], kv.shape[0]
    tm = min(ROW_TILE, S)
    dh = D // N_HEADS
    scale = float(dh) ** -0.5

    def body(q_ref, z_ref, kv_ref, cc_ref):
        for hd in range(N_HEADS):
            sl = slice(hd * dh, (hd + 1) * dh)
            p = _attn_scores(q_ref[:, sl], kv_ref[:, sl], scale)
            o = _dot(p, kv_ref[:, D + hd * dh:D + (hd + 1) * dh], NN)
            z = z_ref[:, sl]
            cc_ref[:, sl] = (o * (z * _sg(z))).astype(BF16)

    return _call(body, name="attn_fwd", grid=(S // tm,),
                 in_specs=[_row(tm, D, qcol), _row(tm, D, zcol), _par(M, 2 * D)], out_specs=_row(tm, D),
                 out_shape=_sds((S, D), BF16), compiler_params=_cp(("parallel",)))(P, P, kv)


def _attn_bwd(d_cc, P, kv, D, qcol, zcol):
    S, M = P.shape[0], kv.shape[0]
    tm = min(ROW_TILE, S)
    dh = D // N_HEADS
    scale = float(dh) ** -0.5

    def body(d_ref, q_ref, z_ref, kv_ref, dq_ref, dz_ref, dkv_ref):
        @pl.when(pl.program_id(0) == 0)
        def _():
            dkv_ref[...] = jnp.zeros_like(dkv_ref)

        for hd in range(N_HEADS):
            sl = slice(hd * dh, (hd + 1) * dh)
            vl = slice(D + hd * dh, D + (hd + 1) * dh)
            q, kh, vh = q_ref[:, sl], kv_ref[:, sl], kv_ref[:, vl]
            p = _attn_scores(q, kh, scale)
            o = _dot(p, vh, NN)
            z, d = z_ref[:, sl], d_ref[:, sl]
            sz = _sg(z)
            do = d * (z * sz)
            dz_ref[:, sl] = (d * o * _dsilu(z, sz)).astype(BF16)
            dp = _dot(do, vh, NT)
            ds = p * (dp - jnp.sum(p * dp, axis=1, keepdims=True)) * scale
            dq_ref[:, sl] = _dot(ds, kh, NN).astype(BF16)
            dkv_ref[:, sl] += _dot(ds, q, TN)
            dkv_ref[:, vl] += _dot(p, do, TN)

    return _call(body, name="attn_bwd", grid=(S // tm,),
                 in_specs=[_row(tm, D), _row(tm, D, qcol), _row(tm, D, zcol), _par(M, 2 * D)],
                 out_specs=[_row(tm, D), _row(tm, D), _par(M, 2 * D)],
                 out_shape=[_sds((S, D), BF16), _sds((S, D), BF16), _sds((M, 2 * D), F32)],
                 compiler_params=_cp(("arbitrary",)))(d_cc, P, P, kv)


def _merge_fwd(yc, ym, yx, P, D, gcol):
    S = yc.shape[0]
    tm = min(ROW_TILE, S)

    def body(c_ref, m_ref, x_ref, gc_ref, gm_ref, gx_ref, o_ref):
        o_ref[...] = (_sg(gc_ref[...]) * c_ref[...] + _sg(gm_ref[...]) * m_ref[...]
                      + _sg(gx_ref[...]) * x_ref[...]).astype(BF16)

    return _call(body, name="merge_fwd", grid=(S // tm,),
                 in_specs=[_row(tm, D)] * 3 + [_row(tm, D, gcol), _row(tm, D, gcol + 1), _row(tm, D, gcol + 2)],
                 out_specs=_row(tm, D), out_shape=_sds((S, D), BF16), compiler_params=_cp(("parallel",)))(yc, ym, yx, P, P, P)


def _merge_bwd(dm, yc, ym, yx, P, D, gcol):
    S = yc.shape[0]
    tm = min(ROW_TILE, S)

    def body(d_ref, c_ref, m_ref, x_ref, gc_ref, gm_ref, gx_ref, dc_ref, dmm_ref, dx_ref, dgc_ref, dgm_ref, dgx_ref):
        d = d_ref[...]
        for y_ref, g_ref, dy_ref, dg_ref in ((c_ref, gc_ref, dc_ref, dgc_ref), (m_ref, gm_ref, dmm_ref, dgm_ref),
                                             (x_ref, gx_ref, dx_ref, dgx_ref)):
            s = _sg(g_ref[...])
            dy_ref[...] = (d * s).astype(BF16)
            dg_ref[...] = (d * y_ref[...] * s * (1.0 - s)).astype(BF16)

    return _call(body, name="merge_bwd", grid=(S // tm,),
                 in_specs=[_row(tm, D)] * 4 + [_row(tm, D, gcol), _row(tm, D, gcol + 1), _row(tm, D, gcol + 2)],
                 out_specs=[_row(tm, D)] * 6, out_shape=[_sds((S, D), BF16)] * 6,
                 compiler_params=_cp(("parallel",)))(dm, yc, ym, yx, P, P, P)


def _post(o, x, tgt, g):
    S, D = x.shape
    tm = min(ROW_TILE, S)

    def body(o_ref, x_ref, t_ref, g_ref, dy_ref, do_ref, dg_ref, l_ref):
        ov, gv = o_ref[...], g_ref[...]
        r = lax.rsqrt(_mean(ov * ov) + EPS)
        e = x_ref[...] + ov * r * gv - t_ref[...]
        dy = e / D
        dy_ref[...] = dy
        dxh = dy * gv
        do_ref[...] = (r * dxh - ov * (r * r * r) * _mean(dxh * ov)).astype(BF16)

        @pl.when(pl.program_id(0) == 0)
        def _():
            dg_ref[...] = jnp.zeros_like(dg_ref)
            l_ref[...] = jnp.zeros_like(l_ref)

        dg_ref[...] += jnp.sum(dy * ov * r, axis=0, keepdims=True)
        l_ref[...] += jnp.sum(e * e, axis=0, keepdims=True)

    return _call(body, name="post", grid=(S // tm,), in_specs=[_row(tm, D)] * 3 + [_par(1, D)],
                 out_specs=[_row(tm, D), _row(tm, D), _par(1, D), _par(1, D)],
                 out_shape=[_sds((S, D), F32), _sds((S, D), BF16), _sds((1, D), F32), _sds((1, D), F32)],
                 compiler_params=_cp(("arbitrary",)))(o, x, tgt, g)


def _chip_peers():
    x, y, c = lax.axis_index("x"), lax.axis_index("y"), lax.axis_index("c")
    return x, y, c, [(1 - x, y), (x, 1 - y), (1 - x, 1 - y)]


def _gather_chips(halved, whole):
    TH, TW = len(halved), len(whole)
    T = TH + TW

    def body(*refs):
        ins, outs = refs[:T], refs[T:2 * T]
        send, recv, fsend, frecv = refs[2 * T:]
        x, y, c, chips = _chip_peers()
        mine = 2 * x + y

        def over_ici(t, j, chip, slot):
            src = ins[t].at[c] if t < TH else ins[t]
            dst = outs[t].at[slot, c] if t < TH else outs[t].at[slot]
            return pltpu.make_async_remote_copy(src_ref=src, dst_ref=dst, send_sem=send.at[t, j], recv_sem=recv.at[t, j],
                                                device_id=(*chip, c), device_id_type=MESH)

        def to_sibling(t, j, slot, half):
            place = outs[t].at[slot, half]
            return pltpu.make_async_remote_copy(src_ref=place, dst_ref=place, send_sem=fsend.at[t, j], recv_sem=frecv.at[t, j],
                                                device_id=(x, y, 1 - c), device_id_type=MESH)

        sends = []
        for t in range(T):
            for j, chip in enumerate(chips):
                cp = over_ici(t, j, chip, mine)
                cp.start()
                sends.append(cp)
        for t in range(T):
            for j, (px, py) in enumerate(chips):
                over_ici(t, j, (px, py), 2 * px + py).wait_recv()
                if t < TH:
                    cp = to_sibling(t, j, 2 * px + py, c)
                    cp.start()
                    sends.append(cp)
        for t in range(TH):
            for j, (px, py) in enumerate(chips):
                to_sibling(t, j, 2 * px + py, 1 - c).wait_recv()
        for cp in sends:
            cp.wait_send()

    anyspec = pl.BlockSpec(memory_space=pl.ANY)
    dma = pltpu.SemaphoreType.DMA
    return _call(body, name="gather_weights", in_specs=[anyspec] * T, out_specs=[anyspec] * T,
                 out_shape=[_sds((4, *s.shape), s.dtype) for s in (*halved, *whole)],
                 scratch_shapes=[dma((T, 3)), dma((T, 3)), dma((TH, 3)), dma((TH, 3))],
                 compiler_params=pltpu.CompilerParams(has_side_effects=True))(*halved, *whole)


def _to_sibling(tensors, name):
    T = len(tensors)

    def body(*refs):
        ins, outs = refs[:T], refs[T:2 * T]
        send, recv = refs[2 * T:]
        x, y, c = lax.axis_index("x"), lax.axis_index("y"), lax.axis_index("c")
        cps = [pltpu.make_async_remote_copy(src_ref=ins[t], dst_ref=outs[t], send_sem=send.at[t], recv_sem=recv.at[t],
                                            device_id=(x, y, 1 - c), device_id_type=MESH) for t in range(T)]
        for cp in cps:
            cp.start()
        for cp in cps:
            cp.wait_recv()
        for cp in cps:
            cp.wait_send()

    anyspec = pl.BlockSpec(memory_space=pl.ANY)
    return _call(body, name=name, in_specs=[anyspec] * T, out_specs=[anyspec] * T,
                 out_shape=[_sds(a.shape, a.dtype) for a in tensors],
                 scratch_shapes=[pltpu.SemaphoreType.DMA((T,)), pltpu.SemaphoreType.DMA((T,))],
                 compiler_params=pltpu.CompilerParams(has_side_effects=True))(*tensors)


def _chip_scatter(tensors):
    T = len(tensors)

    def body(*refs):
        ins, outs = refs[:T], refs[T:2 * T]
        send, recv = refs[2 * T:]
        x, y, c, chips = _chip_peers()
        mine = 2 * x + y

        def over_ici(t, j, chip, src_slot, dst_slot):
            return pltpu.make_async_remote_copy(src_ref=ins[t].at[src_slot], dst_ref=outs[t].at[dst_slot], send_sem=send.at[t, j],
                                                recv_sem=recv.at[t, j], device_id=(*chip, c), device_id_type=MESH)

        sends = []
        for t in range(T):
            for j, (px, py) in enumerate(chips):
                cp = over_ici(t, j, (px, py), 2 * px + py, mine)
                cp.start()
                sends.append(cp)
        for t in range(T):
            for j, (px, py) in enumerate(chips):
                over_ici(t, j, (px, py), mine, 2 * px + py).wait_recv()
        for cp in sends:
            cp.wait_send()

    anyspec = pl.BlockSpec(memory_space=pl.ANY)
    dma = pltpu.SemaphoreType.DMA
    return _call(body, name="chip_scatter", in_specs=[anyspec] * T, out_specs=[anyspec] * T,
                 out_shape=[_sds(a.shape, a.dtype) for a in tensors], scratch_shapes=[dma((T, 3)), dma((T, 3))],
                 compiler_params=pltpu.CompilerParams(has_side_effects=True))(*tensors)


def _row_tile(R, C, slots):
    tr = max(16, ((8 << 20) // (C * 4 * slots)) // 16 * 16)
    if tr >= R:
        return R
    while R % tr:
        tr -= 16
    return tr


def _add_pair(a, b, name):
    lead, R, C = a.shape
    tr = _row_tile(R, C, 3)

    def body(a_ref, b_ref, o_ref):
        o_ref[...] = (a_ref[...].astype(F32) + b_ref[...].astype(F32)).astype(BF16)

    blk = pl.BlockSpec((None, tr, C), lambda k, i: (k, i, 0))
    return _call(body, name=name, grid=(lead, R // tr), in_specs=[blk, blk], out_specs=blk, out_shape=_sds(a.shape, BF16),
                 compiler_params=_cp(("parallel", "parallel")))(a, b)


def _sum_chips(land, name):
    _, R, C = land.shape
    tr = _row_tile(R, C, 5)

    def body(l_ref, o_ref):
        o_ref[...] = ((l_ref[0].astype(F32) + l_ref[1].astype(F32)) + l_ref[2].astype(F32)) + l_ref[3].astype(F32)

    return _call(body, name=name, grid=(R // tr,), in_specs=[pl.BlockSpec((4, tr, C), lambda i: (0, i, 0))],
                 out_specs=pl.BlockSpec((tr, C), lambda i: (i, 0)), out_shape=_sds((R, C), F32),
                 compiler_params=_cp(("parallel",)))(land)


def _scatter_grads(blocks, small):
    T = len(blocks)

    def body(*refs):
        ins, sm = refs[:T], refs[T]
        outs, smo = refs[T + 1:2 * T + 1], refs[2 * T + 1]
        send, recv, loc = refs[2 * T + 2:]
        x, y, c = lax.axis_index("x"), lax.axis_index("y"), lax.axis_index("c")
        me = 4 * x + 2 * y + c
        peers = [(x ^ ((k >> 2) & 1), y ^ ((k >> 1) & 1), c ^ (k & 1)) for k in range(1, N_DEV)]
        copies = []
        for t in range(T):
            lc = pltpu.make_async_copy(ins[t].at[2 * x + y], outs[t].at[me], loc.at[t])
            lc.start()
            copies.append(lc)
        lc = pltpu.make_async_copy(sm, smo.at[me], loc.at[T])
        lc.start()
        copies.append(lc)

        def remote(t, j, peer, sender):
            px, py, pc = peer
            src = sm if t == T else ins[t].at[2 * px + py]
            dst = (smo if t == T else outs[t]).at[sender]
            return pltpu.make_async_remote_copy(src_ref=src, dst_ref=dst, send_sem=send.at[t, j], recv_sem=recv.at[t, j],
                                                device_id=peer, device_id_type=MESH)

        sends = []
        for t in range(T + 1):
            for j, peer in enumerate(peers):
                cp = remote(t, j, peer, me)
                cp.start()
                sends.append(cp)
        for t in range(T + 1):
            for j, (px, py, pc) in enumerate(peers):
                remote(t, j, (x, y, c), 4 * px + 2 * py + pc).wait_recv()
        for cp in sends:
            cp.wait_send()
        for lc in copies:
            lc.wait()

    anyspec = pl.BlockSpec(memory_space=pl.ANY)
    return _call(body, name="scatter_grads", in_specs=[anyspec] * (T + 1), out_specs=[anyspec] * (T + 1),
                 out_shape=[_sds((N_DEV, *b.shape[1:]), b.dtype) for b in blocks] + [_sds((N_DEV, *small.shape), small.dtype)],
                 scratch_shapes=[pltpu.SemaphoreType.DMA((T + 1, N_DEV - 1)), pltpu.SemaphoreType.DMA((T + 1, N_DEV - 1)),
                                 pltpu.SemaphoreType.DMA((T + 1,))],
                 compiler_params=pltpu.CompilerParams(has_side_effects=True))(*blocks, small)


def _adam_update(g, w_ref, m_ref, v_ref, g_ref, d_ref, mo_ref, vo_ref):
    c1, c2 = 1.0 - ADAM_B1 ** ADAM_STEP, 1.0 - ADAM_B2 ** ADAM_STEP
    mn = ADAM_B1 * m_ref[...] + (1.0 - ADAM_B1) * g
    vn = ADAM_B2 * v_ref[...] + (1.0 - ADAM_B2) * (g * g)
    g_ref[...] = g
    mo_ref[...] = mn
    vo_ref[...] = vn
    d_ref[...] = -ADAM_LR * ((mn / c1) / (jnp.sqrt(vn / c2) + ADAM_EPS) + ADAM_WD * w_ref[...])


def _adam_rows(R, C):
    tr = R if R * C * 4 <= (1 << 20) else max(16, ((1 << 20) // (C * 4)) // 16 * 16)
    while R % tr:
        tr -= 16
    return tr


def _adamw(land, w, m, v, name):
    R, C = w.shape
    tr = _adam_rows(R, C)

    def body(l_ref, w_ref, m_ref, v_ref, *outs):
        g = l_ref[0].astype(F32)
        for k in range(1, N_DEV):
            g = g + l_ref[k].astype(F32)
        _adam_update(g, w_ref, m_ref, v_ref, *outs)

    blk = pl.BlockSpec((tr, C), lambda i: (i, 0))
    return _call(body, name=name, grid=(R // tr,), in_specs=[pl.BlockSpec((N_DEV, tr, C), lambda i: (0, i, 0)), blk, blk, blk],
                 out_specs=[blk] * 4, out_shape=[_sds((R, C), F32)] * 4, compiler_params=_cp(("parallel",)))(land, w, m, v)


def _adamw_halves(mine, theirs, w, m, v, name):
    r, C = mine.shape
    tr = _adam_rows(r, C)
    nt = r // tr

    def body(a_ref, b_ref, w_ref, m_ref, v_ref, *outs):
        g = jnp.where(pl.program_id(0) == lax.axis_index("c"), a_ref[...], b_ref[...])
        _adam_update(g, w_ref, m_ref, v_ref, *outs)

    half = pl.BlockSpec((tr, C), lambda hh, i: (i, 0))
    blk = pl.BlockSpec((tr, C), lambda hh, i: (hh * nt + i, 0))
    return _call(body, name=name, grid=(2, nt), in_specs=[half, half, blk, blk, blk], out_specs=[blk] * 4,
                 out_shape=[_sds((2 * r, C), F32)] * 4, compiler_params=_cp(("parallel", "parallel")))(mine, theirs, w, m, v)


def _pad_rows(a, rows):
    return jnp.pad(a, ((0, rows - a.shape[0]), (0, 0)))


def _local_step(x, mem, tgt, g_pre, w_main, w_if, b_if, wq, wk, w_dw, b_dw, g_ln, b_ln, w_conv_out, g_ml_head, w_ml_out,
                g_mem, w_mem_kv, w_xa_out, w_out, g_post):
    S, D = x.shape
    row = lambda a: a.reshape(1, -1)
    bif = jnp.pad(b_if, (0, 128 - b_if.shape[0])).reshape(1, 128)
    wdw, wq8, wk8 = _pad_rows(w_dw, HALO), _pad_rows(wq, QK_HALO), _pad_rows(wk, QK_HALO)

    h = _rms_fwd(x, row(g_pre), "rms_pre")
    P = _mm(h, w_main, "nn", F32, "proj_in")
    gif = _mm(h, w_if, "nn", F32, "proj_if")
    ca, u1 = _conv_fwd(P, wdw, row(b_dw), row(g_ln), row(b_ln), D)
    q, k, cq, ck = _qk_fwd(P, wq8, wk8, D, 3)
    gl = _gates_fwd(gif, bif)
    g8 = gl[:, :8]
    gt = g8.T
    hm, cs, ns, ms = _mlstm_fwd(q, k, P, g8, gt, D, 5)
    cb = _ml_post_fwd(hm, P, row(g_ml_head), D, 6, 7)
    mn = _rms_fwd(mem, row(g_mem), "rms_mem")
    kv = _mm(mn, w_mem_kv, "nn", BF16, "proj_kv")
    cc = _attn_fwd(P, kv, D, 8, 9)
    yc = _mm(ca, w_conv_out, "nn", F32, "out_conv")
    ym = _mm(cb, w_ml_out, "nn", F32, "out_ml")
    yx = _mm(cc, w_xa_out, "nn", F32, "out_xa")
    merged = _merge_fwd(yc, ym, yx, P, D, 10)
    o = _mm(merged, w_out, "nn", F32, "out_proj")
    dy, d_o, gg_post, sq = _post(o, x, tgt, row(g_post))

    d_merged = _mm(d_o, w_out, "nt", F32, "d_merged")
    gw_out = _mm(merged, d_o, "tn", F32, "gw_out")
    d_yc, d_ym, d_yx, d_gc, d_gm, d_gx = _merge_bwd(d_merged, yc, ym, yx, P, D, 10)
    d_ca = _mm(d_yc, w_conv_out, "nt", F32, "d_ca")
    gw_conv_out = _mm(ca, d_yc, "tn", F32, "gw_conv_out")
    d_cb = _mm(d_ym, w_ml_out, "nt", F32, "d_cb")
    gw_ml_out = _mm(cb, d_ym, "tn", F32, "gw_ml_out")
    d_cc = _mm(d_yx, w_xa_out, "nt", F32, "d_cc")
    gw_xa_out = _mm(cc, d_yx, "tn", F32, "gw_xa_out")

    d_u1, d_zc, gg_ln, gb_ln = _conv_bwd_local(d_ca, u1, P, row(g_ln), row(b_ln), D)
    d_a, d_b, gw_dw, gb_dw = _conv_bwd_taps(d_u1, P, wdw, D)

    d_qx, d_zx, d_kv = _attn_bwd(d_cc, P, kv, D, 8, 9)
    d_kvb = d_kv.astype(BF16)
    gw_mem_kv = _mm(mn, d_kvb, "tn", F32, "gw_mem_kv")
    d_mn = _mm(d_kvb, w_mem_kv, "nt", F32, "d_mn")
    _, gg_mem = _rms_bwd(mem, row(g_mem), [d_mn], None, "rms_mem_bwd")

    d_hm, d_om, d_zm, gg_ml = _ml_post_bwd(d_cb, hm, P, row(g_ml_head), D, 6, 7)
    dq, dk, d_v, dg = _mlstm_bwd(d_hm, q, k, P, g8, gt, cs, ns, ms, D, 5)
    d_qp, d_kp, gwq, gwk = _qk_bwd(dq, dk, cq, ck, P, wq8, wk8, D, 3)
    dgl = jnp.pad(jnp.concatenate([dg[:, 0, :].T, dg[:, 1, :].T], axis=1), ((0, 0), (0, 128 - 2 * N_HEADS)))
    d_gif, gb_if = _gates_bwd(dgl, gif, bif)

    dP = jnp.concatenate([d_a, d_b, d_zc, d_qp, d_kp, d_v, d_om, d_zm, d_qx, d_zx, d_gc, d_gm, d_gx], axis=1)
    d_h1 = _mm(dP, w_main, "nt", F32, "d_h_main")
    d_h2 = _mm(d_gif, w_if, "nt", F32, "d_h_if")
    gw_main = _mm(h, dP, "tn", F32, "gw_main")
    gw_if = _mm(h, d_gif, "tn", F32, "gw_if")
    grad_x, gg_pre = _rms_bwd(x, row(g_pre), [d_h1, d_h2], dy, "rms_pre_bwd")

    grads = dict(g_pre=gg_pre[0], w_main=gw_main, w_if=gw_if, b_if=gb_if[0, :2 * N_HEADS], wq=gwq[:QK_W], wk=gwk[:QK_W],
                 w_dw=gw_dw[:CONV_W], b_dw=gb_dw[0], g_ln=gg_ln[0], b_ln=gb_ln[0], w_conv_out=gw_conv_out,
                 g_ml_head=gg_ml[0], w_ml_out=gw_ml_out, g_mem=gg_mem[0], w_mem_kv=gw_mem_kv, w_xa_out=gw_xa_out,
                 w_out=gw_out, g_post=gg_post[0])
    return jnp.sum(sq), grad_x, grads


SMALL = ("g_pre", "b_if", "b_dw", "g_ln", "b_ln", "g_ml_head", "g_mem", "g_post")


def kernel(x, mem, g_pre, w_in, b_if, w_qk_conv, w_dw, b_dw, g_ln, b_ln, w_conv_out, g_ml_head, w_ml_out, g_mem, w_mem_kv, w_xa_out, w_out, g_post, loss_target, m_g_pre, m_w_in, m_b_if, m_w_qk_conv, m_w_dw, m_b_dw, m_g_ln, m_b_ln, m_w_conv_out, m_g_ml_head, m_w_ml_out, m_g_mem, m_w_mem_kv, m_w_xa_out, m_w_out, m_g_post, v_g_pre, v_w_in, v_b_if, v_w_qk_conv, v_w_dw, v_b_dw, v_g_ln, v_b_ln, v_w_conv_out, v_g_ml_head, v_w_ml_out, v_g_mem, v_w_mem_kv, v_w_xa_out, v_w_out, v_g_post):
    W = dict(g_pre=g_pre, w_in=w_in, b_if=b_if, w_qk_conv=w_qk_conv, w_dw=w_dw, b_dw=b_dw, g_ln=g_ln, b_ln=b_ln,
             w_conv_out=w_conv_out, g_ml_head=g_ml_head, w_ml_out=w_ml_out, g_mem=g_mem, w_mem_kv=w_mem_kv,
             w_xa_out=w_xa_out, w_out=w_out, g_post=g_post)
    Mo = dict(g_pre=m_g_pre, w_in=m_w_in, b_if=m_b_if, w_qk_conv=m_w_qk_conv, w_dw=m_w_dw, b_dw=m_b_dw, g_ln=m_g_ln,
              b_ln=m_b_ln, w_conv_out=m_w_conv_out, g_ml_head=m_g_ml_head, w_ml_out=m_w_ml_out, g_mem=m_g_mem,
              w_mem_kv=m_w_mem_kv, w_xa_out=m_w_xa_out, w_out=m_w_out, g_post=m_g_post)
    Vo = dict(g_pre=v_g_pre, w_in=v_w_in, b_if=v_b_if, w_qk_conv=v_w_qk_conv, w_dw=v_w_dw, b_dw=v_b_dw, g_ln=v_g_ln,
              b_ln=v_b_ln, w_conv_out=v_w_conv_out, g_ml_head=v_g_ml_head, w_ml_out=v_w_ml_out, g_mem=v_g_mem,
              w_mem_kv=v_w_mem_kv, w_xa_out=v_w_xa_out, w_out=v_w_out, g_post=v_g_post)
    D = x.shape[-1]
    n_in = 4 * w_in.shape[1]
    off_if = 8 * D

    big = ("w_in", "w_conv_out", "w_ml_out", "w_mem_kv", "w_xa_out", "w_out")
    halves = lambda a: a.reshape(2, a.shape[0] // 2, a.shape[1])
    own = [halves(W[n].astype(BF16)) for n in big] + [w_qk_conv, w_dw]
    my_chip = 2 * lax.axis_index("x") + lax.axis_index("y")
    got = [lax.dynamic_update_index_in_dim(g, o, my_chip, 0) for g, o in zip(_gather_chips(own[:len(big)], own[len(big):]), own)]
    got = [g.reshape(4, -1, g.shape[-1]) for g in got]
    cols = lambda a: jnp.transpose(a, (1, 0, 2)).reshape(a.shape[1], -1)
    rows = lambda a: a.reshape(-1, a.shape[2])
    w_in_f = cols(got[0])
    w_main = jnp.concatenate([w_in_f[:, :off_if], w_in_f[:, off_if + 2 * N_HEADS:]], axis=1)
    w_if = jnp.pad(w_in_f[:, off_if:off_if + 2 * N_HEADS], ((0, 0), (0, 128 - 2 * N_HEADS)))
    wqk_f = cols(got[6])

    sq, grad_x, G = _local_step(
        x[0], mem[0], loss_target[0], g_pre, w_main, w_if, b_if, wqk_f[:, :D], wqk_f[:, D:], cols(got[7]), b_dw, g_ln, b_ln,
        rows(got[1]), g_ml_head, rows(got[2]), g_mem, cols(got[3]), rows(got[4]), rows(got[5]), g_post)
    loss = lax.psum(0.5 * sq / D, ("x", "y", "c"))

    colblk = lambda a: jnp.transpose(a.reshape(a.shape[0], 4, -1), (1, 0, 2))
    colblk_h = lambda a: jnp.transpose(a.reshape(2, a.shape[0] // 2, 4, -1), (0, 2, 1, 3))
    rowblk_h = lambda a: jnp.transpose(a.reshape(4, 2, -1, a.shape[1]), (1, 0, 2, 3))
    gw_in = jnp.concatenate([G["w_main"][:, :off_if], G["w_if"][:, :2 * N_HEADS], G["w_main"][:, off_if:]], axis=1)
    assert gw_in.shape[1] == n_in
    gh = dict(w_in=colblk_h(gw_in), w_conv_out=rowblk_h(G["w_conv_out"]), w_ml_out=rowblk_h(G["w_ml_out"]),
              w_mem_kv=colblk_h(G["w_mem_kv"]), w_xa_out=rowblk_h(G["w_xa_out"]), w_out=rowblk_h(G["w_out"]))
    c = lax.axis_index("c")
    gh = {n: a.astype(BF16) for n, a in gh.items()}
    mine = [lax.dynamic_index_in_dim(gh[n], c, 0, keepdims=False) for n in big]
    theirs = _to_sibling([lax.dynamic_index_in_dim(gh[n], 1 - c, 0, keepdims=False) for n in big], "pair_swap")
    chip_sums = [_add_pair(a, b, "chip_sum_" + n) for n, a, b in zip(big, mine, theirs)]
    landed = [lax.dynamic_update_index_in_dim(l, lax.dynamic_index_in_dim(s, my_chip, 0), my_chip, 0)
              for l, s in zip(_chip_scatter(chip_sums), chip_sums)]
    halves_mine = [_sum_chips(a, "sum_chips_" + n) for n, a in zip(big, landed)]
    halves_theirs = _to_sibling(halves_mine, "share_halves")
    out = {}
    for n, a, b in zip(big, halves_mine, halves_theirs):
        out[n] = _adamw_halves(a, b, W[n], Mo[n], Vo[n], "adamw_" + n)

    pad = lambda a: jnp.pad(a, (0, D - a.shape[0]))
    small_g = jnp.stack([pad(G[n]) for n in SMALL])
    lands = _scatter_grads([colblk(jnp.concatenate([G["wq"], G["wk"]], axis=1)), colblk(G["w_dw"])], small_g)
    for n, land in zip(("w_qk_conv", "w_dw"), lands[:-1]):
        out[n] = _adamw(land, W[n], Mo[n], Vo[n], "adamw_" + n)
    res = _adamw(lands[-1], jnp.stack([pad(W[n]) for n in SMALL]), jnp.stack([pad(Mo[n]) for n in SMALL]),
                 jnp.stack([pad(Vo[n]) for n in SMALL]), "adamw_small")
    for i, n in enumerate(SMALL):
        out[n] = tuple(r[i, :W[n].shape[0]] for r in res)
    order = ("g_pre", "w_in", "b_if", "w_qk_conv", "w_dw", "b_dw", "g_ln", "b_ln", "w_conv_out", "g_ml_head", "w_ml_out",
             "g_mem", "w_mem_kv", "w_xa_out", "w_out", "g_post")
    return (loss, grad_x[None], *[out[n][0] for n in order], *[out[n][1] for n in order], *[out[n][2] for n in order],
            *[out[n][3] for n in order])
```

```python
import jax
import jax.numpy as jnp
from jax import lax
from jax.experimental import pallas as pl
from jax.experimental.pallas import tpu as pltpu

F32, BF16 = jnp.float32, jnp.bfloat16
EPS = 1e-6
N_HEADS = 4
CONV_W = 31
QK_W = 4
HALO = 32
QK_HALO = 8
ROW_TILE = 256
ML_CHUNK = 256
VMEM_LIMIT = 56 * 1024 * 1024
NEG = -1e30
ADAM_LR, ADAM_B1, ADAM_B2, ADAM_EPS, ADAM_WD, ADAM_STEP = 0.001, 0.9, 0.999, 1e-08, 0.01, 10
MESH = pl.DeviceIdType.MESH
N_DEV = 8

NN = (((1,), (0,)), ((), ()))
NT = (((1,), (1,)), ((), ()))
TN = (((0,), (0,)), ((), ()))


def _call(body, **kw):
    return pl.pallas_call(body, **kw)


def _cp(sem):
    return pltpu.CompilerParams(dimension_semantics=sem, vmem_limit_bytes=VMEM_LIMIT)


def _sds(shape, dt):
    return jax.ShapeDtypeStruct(shape, dt)


def _row(tm, d, col=0):
    return pl.BlockSpec((tm, d), lambda i: (i, col))


def _par(r, d):
    return pl.BlockSpec((r, d), lambda i: (0, 0))


def _prev(tm, hb, d, col=0):
    return pl.BlockSpec((hb, d), lambda i: (jnp.maximum(i * (tm // hb) - 1, 0), col))


def _next(tm, hb, d, nblk, col=0):
    return pl.BlockSpec((hb, d), lambda i: (jnp.minimum((i + 1) * (tm // hb), nblk - 1), col))


def _dot(a, b, dn):
    return lax.dot_general(a.astype(BF16), b.astype(BF16), dn, preferred_element_type=F32)


def _sg(x):
    return jax.nn.sigmoid(x)


def _dsilu(z, s):
    return s * (1.0 + z * (1.0 - s))


def _mean(x):
    return jnp.mean(x, axis=-1, keepdims=True)


def _pick(n, pref):
    if n <= pref:
        return n
    t = pref
    while n % t:
        t -= 128
    return t


def _mm(a, b, mode, out_dtype, name, tm=1024, tn=1024, tk=2048):
    if mode == "nn":
        (M, K), N = a.shape, b.shape[1]
    elif mode == "nt":
        (M, K), N = a.shape, b.shape[0]
    else:
        (K, M), N = a.shape, b.shape[1]
    tm, tn, tk = _pick(M, tm), _pick(N, tn), _pick(K, tk)
    nk = K // tk
    if mode == "nn":
        sa, sb, dn = pl.BlockSpec((tm, tk), lambda i, j, k: (i, k)), pl.BlockSpec((tk, tn), lambda i, j, k: (k, j)), NN
    elif mode == "nt":
        sa, sb, dn = pl.BlockSpec((tm, tk), lambda i, j, k: (i, k)), pl.BlockSpec((tn, tk), lambda i, j, k: (j, k)), NT
    else:
        sa, sb, dn = pl.BlockSpec((tk, tm), lambda i, j, k: (k, i)), pl.BlockSpec((tk, tn), lambda i, j, k: (k, j)), TN

    def body(a_ref, b_ref, o_ref, *acc):
        p = _dot(a_ref[...], b_ref[...], dn)
        if nk == 1:
            o_ref[...] = p.astype(out_dtype)
        else:
            acc_ref, k = acc[0], pl.program_id(2)

            @pl.when(k == 0)
            def _():
                acc_ref[...] = p

            @pl.when(k > 0)
            def _():
                acc_ref[...] += p

            @pl.when(k == nk - 1)
            def _():
                o_ref[...] = acc_ref[...].astype(out_dtype)

    return _call(body, name=name, grid=(M // tm, N // tn, nk), in_specs=[sa, sb],
                 out_specs=pl.BlockSpec((tm, tn), lambda i, j, k: (i, j)), out_shape=_sds((M, N), out_dtype),
                 scratch_shapes=[pltpu.VMEM((tm, tn), F32)] if nk > 1 else [],
                 compiler_params=_cp(("parallel", "parallel", "arbitrary")))(a, b)


def _rms_fwd(x, g, name):
    S, D = x.shape
    tm = min(ROW_TILE, S)

    def body(x_ref, g_ref, h_ref):
        xv = x_ref[...]
        r = lax.rsqrt(_mean(xv * xv) + EPS)
        h_ref[...] = (xv * r * g_ref[...]).astype(BF16)

    return _call(body, name=name, grid=(S // tm,), in_specs=[_row(tm, D), _par(1, D)], out_specs=_row(tm, D),
                 out_shape=_sds((S, D), BF16), compiler_params=_cp(("parallel",)))(x, g)


def _rms_bwd(x, g, ds, resid, name):
    S, D = x.shape
    tm = min(ROW_TILE, S)
    nd = len(ds)

    def body(*refs):
        x_ref, g_ref = refs[:2]
        d_refs = refs[2:2 + nd]
        r_ref = refs[2 + nd] if resid is not None else None
        dx_ref, dg_ref = refs[-2:]
        xv = x_ref[...]
        d = d_refs[0][...]
        for dr in d_refs[1:]:
            d = d + dr[...]
        r = lax.rsqrt(_mean(xv * xv) + EPS)
        dxh = d * g_ref[...]
        dx = r * dxh - xv * (r * r * r) * _mean(dxh * xv)
        if r_ref is not None:
            dx = dx + r_ref[...]
        dx_ref[...] = dx

        @pl.when(pl.program_id(0) == 0)
        def _():
            dg_ref[...] = jnp.zeros_like(dg_ref)

        dg_ref[...] += jnp.sum(d * xv * r, axis=0, keepdims=True)

    ins = [x, g, *ds] + ([resid] if resid is not None else [])
    specs = [_row(tm, D), _par(1, D)] + [_row(tm, D)] * (len(ins) - 2)
    return _call(body, name=name, grid=(S // tm,), in_specs=specs, out_specs=[_row(tm, D), _par(1, D)],
                 out_shape=[_sds((S, D), F32), _sds((1, D), F32)], compiler_params=_cp(("arbitrary",)))(*ins)


def _conv_fwd(P, w_dw, b_dw, g_ln, b_ln, D):
    S = P.shape[0]
    tm = min(ROW_TILE, S)

    def body(a_ref, b_ref, ap_ref, bp_ref, z_ref, w_ref, bd_ref, g_ref, bl_ref, ca_ref, u1_ref, buf):
        first = pl.program_id(0) == 0
        hp = ap_ref[...] * _sg(bp_ref[...])
        buf[0:HALO, :] = jnp.where(first, 0.0, hp)
        buf[HALO:HALO + tm, :] = a_ref[...] * _sg(b_ref[...])
        acc = jnp.zeros((tm, D), F32)
        for j in range(CONV_W):
            acc = acc + w_ref[j:j + 1, :] * buf[HALO - CONV_W + 1 + j:HALO - CONV_W + 1 + j + tm, :]
        u1 = acc + bd_ref[...]
        u1_ref[...] = u1
        xc = u1 - _mean(u1)
        r = lax.rsqrt(_mean(xc * xc) + EPS)
        u2 = xc * r * g_ref[...] + bl_ref[...]
        z = z_ref[...]
        ca_ref[...] = (u2 * _sg(u2) * (z * _sg(z))).astype(BF16)

    return _call(body, name="conv_fwd", grid=(S // tm,),
                 in_specs=[_row(tm, D, 0), _row(tm, D, 1), _prev(tm, HALO, D, 0), _prev(tm, HALO, D, 1), _row(tm, D, 2),
                           _par(HALO, D), _par(1, D), _par(1, D), _par(1, D)],
                 out_specs=[_row(tm, D), _row(tm, D)], out_shape=[_sds((S, D), BF16), _sds((S, D), F32)],
                 scratch_shapes=[pltpu.VMEM((tm + HALO, D), F32)], compiler_params=_cp(("parallel",)))(
        P, P, P, P, P, w_dw, b_dw, g_ln, b_ln)


def _conv_bwd_local(d_ca, u1, P, g_ln, b_ln, D):
    S = P.shape[0]
    tm = min(ROW_TILE, S)

    def body(dca_ref, u1_ref, z_ref, g_ref, bl_ref, du1_ref, dz_ref, dg_ref, db_ref):
        u1, z, dca, g = u1_ref[...], z_ref[...], dca_ref[...], g_ref[...]
        xc = u1 - _mean(u1)
        r = lax.rsqrt(_mean(xc * xc) + EPS)
        xh = xc * r
        u2 = xh * g + bl_ref[...]
        s2, sz = _sg(u2), _sg(z)
        d_u3 = dca * (z * sz)
        dz_ref[...] = (dca * (u2 * s2) * _dsilu(z, sz)).astype(BF16)
        d_u2 = d_u3 * _dsilu(u2, s2)
        dxh = d_u2 * g
        du1_ref[...] = r * (dxh - _mean(dxh) - xh * _mean(dxh * xh))

        @pl.when(pl.program_id(0) == 0)
        def _():
            dg_ref[...] = jnp.zeros_like(dg_ref)
            db_ref[...] = jnp.zeros_like(db_ref)

        dg_ref[...] += jnp.sum(d_u2 * xh, axis=0, keepdims=True)
        db_ref[...] += jnp.sum(d_u2, axis=0, keepdims=True)

    return _call(body, name="conv_bwd_local", grid=(S // tm,),
                 in_specs=[_row(tm, D), _row(tm, D), _row(tm, D, 2), _par(1, D), _par(1, D)],
                 out_specs=[_row(tm, D), _row(tm, D), _par(1, D), _par(1, D)],
                 out_shape=[_sds((S, D), F32), _sds((S, D), BF16), _sds((1, D), F32), _sds((1, D), F32)],
                 compiler_params=_cp(("arbitrary",)))(d_ca, u1, P, g_ln, b_ln)


def _conv_bwd_taps(d_u1, P, w_dw, D):
    S = P.shape[0]
    tm = min(ROW_TILE, S)
    nblk = S // HALO

    def body(d_ref, dn_ref, a_ref, b_ref, ap_ref, bp_ref, w_ref, da_ref, db_ref, gw_ref, gb_ref, dbuf, ubuf):
        i = pl.program_id(0)
        a, sb = a_ref[...], _sg(b_ref[...])
        d = d_ref[...]
        dbuf[0:tm, :] = d
        dbuf[tm:tm + HALO, :] = jnp.where(i == pl.num_programs(0) - 1, 0.0, dn_ref[...])
        ubuf[0:HALO, :] = jnp.where(i == 0, 0.0, ap_ref[...] * _sg(bp_ref[...]))
        ubuf[HALO:HALO + tm, :] = a * sb

        @pl.when(i == 0)
        def _():
            gw_ref[...] = jnp.zeros_like(gw_ref)
            gb_ref[...] = jnp.zeros_like(gb_ref)

        acc = jnp.zeros((tm, D), F32)
        for j in range(CONV_W):
            acc = acc + w_ref[j:j + 1, :] * dbuf[CONV_W - 1 - j:CONV_W - 1 - j + tm, :]
            sh = HALO - CONV_W + 1 + j
            gw_ref[j:j + 1, :] += jnp.sum(d * ubuf[sh:sh + tm, :], axis=0, keepdims=True)
        gb_ref[...] += jnp.sum(d, axis=0, keepdims=True)
        da_ref[...] = (acc * sb).astype(BF16)
        db_ref[...] = (acc * a * sb * (1.0 - sb)).astype(BF16)

    return _call(body, name="conv_bwd_taps", grid=(S // tm,),
                 in_specs=[_row(tm, D), _next(tm, HALO, D, nblk), _row(tm, D, 0), _row(tm, D, 1),
                           _prev(tm, HALO, D, 0), _prev(tm, HALO, D, 1), _par(HALO, D)],
                 out_specs=[_row(tm, D), _row(tm, D), _par(HALO, D), _par(1, D)],
                 out_shape=[_sds((S, D), BF16), _sds((S, D), BF16), _sds((HALO, D), F32), _sds((1, D), F32)],
                 scratch_shapes=[pltpu.VMEM((tm + HALO, D), F32), pltpu.VMEM((tm + HALO, D), F32)],
                 compiler_params=_cp(("arbitrary",)))(d_u1, d_u1, P, P, P, P, w_dw)


def _qk_fwd(P, wq, wk, D, col):
    S = P.shape[0]
    tm = min(ROW_TILE, S)
    scale = float(D // N_HEADS) ** -0.5

    def body(q_ref, k_ref, qp_ref, kp_ref, wq_ref, wk_ref, qo_ref, ko_ref, cq_ref, ck_ref, buf):
        first = pl.program_id(0) == 0
        for x_ref, p_ref, w_ref, o_ref, c_ref, sc in ((q_ref, qp_ref, wq_ref, qo_ref, cq_ref, 1.0),
                                                      (k_ref, kp_ref, wk_ref, ko_ref, ck_ref, scale)):
            buf[0:QK_HALO, :] = jnp.where(first, 0.0, p_ref[...])
            buf[QK_HALO:QK_HALO + tm, :] = x_ref[...]
            acc = jnp.zeros((tm, D), F32)
            for j in range(QK_W):
                sh = QK_HALO - QK_W + 1 + j
                acc = acc + w_ref[j:j + 1, :] * buf[sh:sh + tm, :]
            c_ref[...] = acc
            o_ref[...] = acc * _sg(acc) * sc

    return _call(body, name="qk_fwd", grid=(S // tm,),
                 in_specs=[_row(tm, D, col), _row(tm, D, col + 1), _prev(tm, QK_HALO, D, col), _prev(tm, QK_HALO, D, col + 1),
                           _par(QK_HALO, D), _par(QK_HALO, D)],
                 out_specs=[_row(tm, D)] * 4, out_shape=[_sds((S, D), F32)] * 4,
                 scratch_shapes=[pltpu.VMEM((tm + QK_HALO, D), F32)], compiler_params=_cp(("parallel",)))(P, P, P, P, wq, wk)


def _qk_bwd(dq, dk, cq, ck, P, wq, wk, D, col):
    S = P.shape[0]
    tm = min(ROW_TILE, S)
    nblk = S // QK_HALO
    scale = float(D // N_HEADS) ** -0.5

    def body(dq_ref, dqn_ref, cq_ref, cqn_ref, xq_ref, xqp_ref, wq_ref,
             dk_ref, dkn_ref, ck_ref, ckn_ref, xk_ref, xkp_ref, wk_ref,
             oq_ref, ok_ref, gq_ref, gk_ref, dbuf, xbuf):
        i = pl.program_id(0)
        last = i == pl.num_programs(0) - 1

        @pl.when(i == 0)
        def _():
            gq_ref[...] = jnp.zeros_like(gq_ref)
            gk_ref[...] = jnp.zeros_like(gk_ref)

        for d_ref, dn_ref, c_ref, cn_ref, x_ref, xp_ref, w_ref, o_ref, g_ref, sc in (
                (dq_ref, dqn_ref, cq_ref, cqn_ref, xq_ref, xqp_ref, wq_ref, oq_ref, gq_ref, 1.0),
                (dk_ref, dkn_ref, ck_ref, ckn_ref, xk_ref, xkp_ref, wk_ref, ok_ref, gk_ref, scale)):
            c, cn = c_ref[...], cn_ref[...]
            dc = d_ref[...] * sc * _dsilu(c, _sg(c))
            dcn = dn_ref[...] * sc * _dsilu(cn, _sg(cn))
            dbuf[0:tm, :] = dc
            dbuf[tm:tm + QK_HALO, :] = jnp.where(last, 0.0, dcn)
            xbuf[0:QK_HALO, :] = jnp.where(i == 0, 0.0, xp_ref[...])
            xbuf[QK_HALO:QK_HALO + tm, :] = x_ref[...]
            acc = jnp.zeros((tm, D), F32)
            for j in range(QK_W):
                acc = acc + w_ref[j:j + 1, :] * dbuf[QK_W - 1 - j:QK_W - 1 - j + tm, :]
                sh = QK_HALO - QK_W + 1 + j
                g_ref[j:j + 1, :] += jnp.sum(dc * xbuf[sh:sh + tm, :], axis=0, keepdims=True)
            o_ref[...] = acc.astype(BF16)

    one = [_row(tm, D), _next(tm, QK_HALO, D, nblk)]
    specs = (one + one + [_row(tm, D, col), _prev(tm, QK_HALO, D, col), _par(QK_HALO, D)]
             + one + one + [_row(tm, D, col + 1), _prev(tm, QK_HALO, D, col + 1), _par(QK_HALO, D)])
    return _call(body, name="qk_bwd", grid=(S // tm,), in_specs=specs,
                 out_specs=[_row(tm, D), _row(tm, D), _par(QK_HALO, D), _par(QK_HALO, D)],
                 out_shape=[_sds((S, D), BF16), _sds((S, D), BF16), _sds((QK_HALO, D), F32), _sds((QK_HALO, D), F32)],
                 scratch_shapes=[pltpu.VMEM((tm + QK_HALO, D), F32), pltpu.VMEM((tm + QK_HALO, D), F32)],
                 compiler_params=_cp(("arbitrary",)))(dq, dq, cq, cq, P, P, wq, dk, dk, ck, ck, P, P, wk)


def _gates_fwd(gif, b_if):
    S = gif.shape[0]
    tm = min(ROW_TILE, S)

    def body(g_ref, b_ref, o_ref):
        z = g_ref[...] + b_ref[...]
        lane = lax.broadcasted_iota(jnp.int32, z.shape, 1)
        ls = jnp.minimum(z, 0.0) - jnp.log(1.0 + jnp.exp(-jnp.abs(z)))
        o_ref[...] = jnp.where(lane < N_HEADS, z, jnp.where(lane < 2 * N_HEADS, ls, 0.0))

    return _call(body, name="gates_fwd", grid=(S // tm,), in_specs=[_row(tm, 128), _par(1, 128)], out_specs=_row(tm, 128),
                 out_shape=_sds((S, 128), F32), compiler_params=_cp(("parallel",)))(gif, b_if)


def _gates_bwd(dgl, gif, b_if):
    S = gif.shape[0]
    tm = min(ROW_TILE, S)

    def body(d_ref, g_ref, b_ref, o_ref, gb_ref):
        z = g_ref[...] + b_ref[...]
        lane = lax.broadcasted_iota(jnp.int32, z.shape, 1)
        d = d_ref[...]
        dz = jnp.where(lane < N_HEADS, d, jnp.where(lane < 2 * N_HEADS, d * _sg(-z), 0.0))
        o_ref[...] = dz.astype(BF16)

        @pl.when(pl.program_id(0) == 0)
        def _():
            gb_ref[...] = jnp.zeros_like(gb_ref)

        gb_ref[...] += jnp.sum(dz, axis=0, keepdims=True)

    return _call(body, name="gates_bwd", grid=(S // tm,), in_specs=[_row(tm, 128), _row(tm, 128), _par(1, 128)],
                 out_specs=[_row(tm, 128), _par(1, 128)], out_shape=[_sds((S, 128), BF16), _sds((1, 128), F32)],
                 compiler_params=_cp(("arbitrary",)))(dgl, gif, b_if)


def _chunk_gates(g8_ref, gt_ref, h, L):
    g8, gt = g8_ref[...], gt_ref[...]
    lane = lax.broadcasted_iota(jnp.int32, g8.shape, 1)
    sub = lax.broadcasted_iota(jnp.int32, gt.shape, 0)
    li_col = jnp.sum(jnp.where(lane == h, g8, 0.0), axis=1, keepdims=True)
    lf_col = jnp.sum(jnp.where(lane == h + N_HEADS, g8, 0.0), axis=1, keepdims=True)
    li_row = jnp.sum(jnp.where(sub == h, gt, 0.0), axis=0, keepdims=True)
    lf_row = jnp.sum(jnp.where(sub == h + N_HEADS, gt, 0.0), axis=0, keepdims=True)
    t = lax.broadcasted_iota(jnp.int32, (L, L), 0)
    s = lax.broadcasted_iota(jnp.int32, (L, L), 1)
    causal = s <= t
    b_col = jnp.sum(jnp.where(causal, lf_row, 0.0), axis=1, keepdims=True)
    b_row = jnp.sum(jnp.where(t <= s, lf_col, 0.0), axis=0, keepdims=True)
    return li_col, li_row, b_col, b_row, causal


def _chunk_fwd(q, k, v, C, n, m, li_row, b_col, b_row, causal):
    d = jnp.where(causal, b_col - b_row + li_row, NEG)
    inter = b_col + m
    m_row = jnp.maximum(inter, jnp.max(d, axis=1, keepdims=True))
    wi = jnp.exp(d - m_row)
    wn = jnp.exp(inter - m_row)
    s = _dot(q, k, NT) * wi
    num = _dot(s, v, NN) + wn * _dot(q, C, NN)
    den = jnp.sum(s, axis=1, keepdims=True) + wn * jnp.sum(q * n, axis=1, keepdims=True)
    e = jnp.exp(-m_row)
    inv = 1.0 / jnp.maximum(jnp.abs(den), e)
    return wi, wn, s, num * inv, den, e, inv


def _chunk_state(li_col, li_row, b_col, b_row, m, L):
    blast = b_col[L - 1:L, :]
    g_col = blast - b_col + li_col
    g_row = blast - b_row + li_row
    m_new = jnp.maximum(blast + m, jnp.max(g_row, axis=1, keepdims=True))
    decay = jnp.exp(blast + m - m_new)
    wk_col = jnp.exp(g_col - m_new)
    return m_new, decay, wk_col


def _mlstm_fwd(q, k, P, g8, gt, D, vcol):
    S = q.shape[0]
    H, dh = N_HEADS, D // N_HEADS
    L = min(ML_CHUNK, S)
    nc = S // L

    def body(q_ref, k_ref, v_ref, g8_ref, gt_ref, h_ref, cs_ref, ns_ref, ms_ref, C, n, m):
        h, j = pl.program_id(0), pl.program_id(1)

        @pl.when(j == 0)
        def _():
            C[...] = jnp.zeros_like(C)
            n[...] = jnp.zeros_like(n)
            m[...] = jnp.zeros_like(m)

        qv, kv, vv = q_ref[...], k_ref[...], v_ref[...]
        Cv, nv, mv = C[...], n[...], m[:, 0:1]
        cs_ref[...] = Cv.astype(BF16)
        ns_ref[...] = nv
        ms_ref[...] = m[...]
        li_col, li_row, b_col, b_row, causal = _chunk_gates(g8_ref, gt_ref, h, L)
        _, _, _, hv, _, _, _ = _chunk_fwd(qv, kv, vv, Cv, nv, mv, li_row, b_col, b_row, causal)
        h_ref[...] = hv
        m_new, decay, wk_col = _chunk_state(li_col, li_row, b_col, b_row, mv, L)
        kw = kv * wk_col
        C[...] = decay * Cv + _dot(kw, vv, TN)
        n[...] = decay * nv + jnp.sum(kw, axis=0, keepdims=True)
        m[...] = jnp.broadcast_to(m_new, m.shape)

    blk = lambda c0: pl.BlockSpec((L, dh), lambda h, j: (j, c0 + h))
    return _call(body, name="mlstm_fwd", grid=(H, nc),
                 in_specs=[blk(0), blk(0), blk(vcol * H), pl.BlockSpec((L, 8), lambda h, j: (j, 0)),
                           pl.BlockSpec((8, L), lambda h, j: (0, j))],
                 out_specs=[blk(0), pl.BlockSpec((None, None, dh, dh), lambda h, j: (h, j, 0, 0)),
                            pl.BlockSpec((None, None, 1, dh), lambda h, j: (h, j, 0, 0)),
                            pl.BlockSpec((None, None, 1, 128), lambda h, j: (h, j, 0, 0))],
                 out_shape=[_sds((S, D), F32), _sds((H, nc, dh, dh), BF16), _sds((H, nc, 1, dh), F32),
                            _sds((H, nc, 1, 128), F32)],
                 scratch_shapes=[pltpu.VMEM((dh, dh), F32), pltpu.VMEM((1, dh), F32), pltpu.VMEM((1, 128), F32)],
                 compiler_params=_cp(("arbitrary", "arbitrary")))(q, k, P, g8, gt)


def _mlstm_bwd(dhm, q, k, P, g8, gt, cs, ns, ms, D, vcol):
    S = q.shape[0]
    H, dh = N_HEADS, D // N_HEADS
    L = min(ML_CHUNK, S)
    nc = S // L

    def body(dh_ref, q_ref, k_ref, v_ref, g8_ref, gt_ref, cs_ref, ns_ref, ms_ref,
             dq_ref, dk_ref, dv_ref, dg_ref, dC, dnv):
        h, j = pl.program_id(0), pl.program_id(1)

        @pl.when(j == 0)
        def _():
            dC[...] = jnp.zeros_like(dC)
            dnv[...] = jnp.zeros_like(dnv)

        qv, kv, vv, dhv = q_ref[...], k_ref[...], v_ref[...], dh_ref[...]
        Cv, nv, mv = cs_ref[...], ns_ref[...], ms_ref[:, 0:1]
        li_col, li_row, b_col, b_row, causal = _chunk_gates(g8_ref, gt_ref, h, L)
        wi, wn, s, hv, den, e, inv = _chunk_fwd(qv, kv, vv, Cv, nv, mv, li_row, b_col, b_row, causal)
        dn = dhv * inv
        dd = -jnp.sum(dhv * hv, axis=1, keepdims=True) * inv * jnp.where(jnp.abs(den) > e, jnp.sign(den), 0.0)
        dS = _dot(dn, vv, NT) + dd
        dqk = dS * wi
        dq_i = wn * (_dot(dn, Cv, NT) + dd * nv)
        dq = _dot(dqk, kv, NN) + dq_i
        dv = _dot(s, dn, TN)
        m_new, decay, wk_col = _chunk_state(li_col, li_row, b_col, b_row, mv, L)
        dCv, dnvv = dC[...], dnv[...]
        dk_i = wk_col * (_dot(vv, dCv, NT) + dnvv)
        dk = _dot(dqk, qv, TN) + dk_i
        dv = dv + _dot(kv * wk_col, dCv, NN)
        qw = qv * wn
        dC[...] = decay * dCv + _dot(qw, dn, TN)
        dnv[...] = decay * dnvv + jnp.sum(qw * dd, axis=0, keepdims=True)
        dq_ref[...] = dq
        dk_ref[...] = dk
        dv_ref[...] = dv.astype(BF16)
        G = dS * s
        t = lax.broadcasted_iota(jnp.int32, (L, L), 0)
        r = lax.broadcasted_iota(jnp.int32, (L, L), 1)
        eye = t == r
        col_row = jnp.sum(G, axis=0, keepdims=True)
        col_g = jnp.sum(jnp.where(eye, col_row, 0.0), axis=1, keepdims=True)
        row_i = jnp.sum(qv * dq_i, axis=1, keepdims=True)
        col_i = jnp.sum(kv * dk_i, axis=1, keepdims=True)
        dF = jnp.sum(G, axis=1, keepdims=True) - col_g + row_i
        across = decay * (jnp.sum(jnp.sum(dCv * Cv.astype(F32), axis=0, keepdims=True), axis=1, keepdims=True)
                          + jnp.sum(dnvv * nv, axis=1, keepdims=True))
        dlf = jnp.sum(jnp.where(t >= r, dF, 0.0) + jnp.where(t < r, col_i, 0.0), axis=0, keepdims=True) + across
        dli = col_row + jnp.sum(jnp.where(eye, col_i, 0.0), axis=0, keepdims=True)
        sub = lax.broadcasted_iota(jnp.int32, (8, L), 0)
        dg_ref[...] = jnp.where(sub == 0, dli, jnp.where(sub == 1, dlf, 0.0))

    blk = lambda c0: pl.BlockSpec((L, dh), lambda h, j: (nc - 1 - j, c0 + h))
    st = lambda r, c: pl.BlockSpec((None, None, r, c), lambda h, j: (h, nc - 1 - j, 0, 0))
    return _call(body, name="mlstm_bwd", grid=(H, nc),
                 in_specs=[blk(0), blk(0), blk(0), blk(vcol * H), pl.BlockSpec((L, 8), lambda h, j: (nc - 1 - j, 0)),
                           pl.BlockSpec((8, L), lambda h, j: (0, nc - 1 - j)), st(dh, dh), st(1, dh), st(1, 128)],
                 out_specs=[blk(0), blk(0), blk(0), pl.BlockSpec((None, 8, L), lambda h, j: (h, 0, nc - 1 - j))],
                 out_shape=[_sds((S, D), F32), _sds((S, D), F32), _sds((S, D), BF16), _sds((H, 8, S), F32)],
                 scratch_shapes=[pltpu.VMEM((dh, dh), F32), pltpu.VMEM((1, dh), F32)],
                 compiler_params=_cp(("arbitrary", "arbitrary")))(dhm, q, k, P, g8, gt, cs, ns, ms)


def _ml_post_fwd(hm, P, g, D, ocol, zcol):
    S = hm.shape[0]
    tm = min(ROW_TILE, S)
    dh = D // N_HEADS

    def body(h_ref, o_ref, z_ref, g_ref, cb_ref):
        for hd in range(N_HEADS):
            sl = slice(hd * dh, (hd + 1) * dh)
            x = _sg(o_ref[:, sl]) * h_ref[:, sl]
            r = lax.rsqrt(_mean(x * x) + EPS)
            z = z_ref[:, sl]
            cb_ref[:, sl] = (x * r * g_ref[:, sl] * (z * _sg(z))).astype(BF16)

    return _call(body, name="ml_post_fwd", grid=(S // tm,),
                 in_specs=[_row(tm, D), _row(tm, D, ocol), _row(tm, D, zcol), _par(1, D)], out_specs=_row(tm, D),
                 out_shape=_sds((S, D), BF16), compiler_params=_cp(("parallel",)))(hm, P, P, g)


def _ml_post_bwd(d_cb, hm, P, g, D, ocol, zcol):
    S = hm.shape[0]
    tm = min(ROW_TILE, S)
    dh = D // N_HEADS

    def body(d_ref, h_ref, o_ref, z_ref, g_ref, dh_ref, do_ref, dz_ref, dg_ref):
        @pl.when(pl.program_id(0) == 0)
        def _():
            dg_ref[...] = jnp.zeros_like(dg_ref)

        for hd in range(N_HEADS):
            sl = slice(hd * dh, (hd + 1) * dh)
            hv, z, gv, d = h_ref[:, sl], z_ref[:, sl], g_ref[:, sl], d_ref[:, sl]
            so, sz = _sg(o_ref[:, sl]), _sg(z)
            x = so * hv
            r = lax.rsqrt(_mean(x * x) + EPS)
            xh = x * r
            d3 = d * (z * sz)
            dz_ref[:, sl] = (d * xh * gv * _dsilu(z, sz)).astype(BF16)
            dg_ref[:, sl] += jnp.sum(d3 * xh, axis=0, keepdims=True)
            dxh = d3 * gv
            d2 = r * (dxh - xh * _mean(dxh * xh))
            do_ref[:, sl] = (d2 * hv * so * (1.0 - so)).astype(BF16)
            dh_ref[:, sl] = d2 * so

    return _call(body, name="ml_post_bwd", grid=(S // tm,),
                 in_specs=[_row(tm, D), _row(tm, D), _row(tm, D, ocol), _row(tm, D, zcol), _par(1, D)],
                 out_specs=[_row(tm, D), _row(tm, D), _row(tm, D), _par(1, D)],
                 out_shape=[_sds((S, D), F32), _sds((S, D), BF16), _sds((S, D), BF16), _sds((1, D), F32)],
                 compiler_params=_cp(("arbitrary",)))(d_cb, hm, P, P, g)


def _attn_scores(q, kh, scale):
    sc = _dot(q, kh, NT) * scale
    ex = jnp.exp(sc - jnp.max(sc, axis=1, keepdims=True))
    return ex / jnp.sum(ex, axis=1, keepdims=True)


def _attn_fwd(P, kv, D, qcol, zcol):
    S, M = P.shape[0], kv.shape[0]
    tm = min(ROW_TILE, S)
    dh = D // N_HEADS
    scale = float(dh) ** -0.5

    def body(q_ref, z_ref, kv_ref, cc_ref):
        for hd in range(N_HEADS):
            sl = slice(hd * dh, (hd + 1) * dh)
            p = _attn_scores(q_ref[:, sl], kv_ref[:, sl], scale)
            o = _dot(p, kv_ref[:, D + hd * dh:D + (hd + 1) * dh], NN)
            z = z_ref[:, sl]
            cc_ref[:, sl] = (o * (z * _sg(z))).astype(BF16)

    return _call(body, name="attn_fwd", grid=(S // tm,),
                 in_specs=[_row(tm, D, qcol), _row(tm, D, zcol), _par(M, 2 * D)], out_specs=_row(tm, D),
                 out_shape=_sds((S, D), BF16), compiler_params=_cp(("parallel",)))(P, P, kv)


def _attn_bwd(d_cc, P, kv, D, qcol, zcol):
    S, M = P.shape[0], kv.shape[0]
    tm = min(ROW_TILE, S)
    dh = D // N_HEADS
    scale = float(dh) ** -0.5

    def body(d_ref, q_ref, z_ref, kv_ref, dq_ref, dz_ref, dkv_ref):
        @pl.when(pl.program_id(0) == 0)
        def _():
            dkv_ref[...] = jnp.zeros_like(dkv_ref)

        for hd in range(N_HEADS):
            sl = slice(hd * dh, (hd + 1) * dh)
            vl = slice(D + hd * dh, D + (hd + 1) * dh)
            q, kh, vh = q_ref[:, sl], kv_ref[:, sl], kv_ref[:, vl]
            p = _attn_scores(q, kh, scale)
            o = _dot(p, vh, NN)
            z, d = z_ref[:, sl], d_ref[:, sl]
            sz = _sg(z)
            do = d * (z * sz)
            dz_ref[:, sl] = (d * o * _dsilu(z, sz)).astype(BF16)
            dp = _dot(do, vh, NT)
            ds = p * (dp - jnp.sum(p * dp, axis=1, keepdims=True)) * scale
            dq_ref[:, sl] = _dot(ds, kh, NN).astype(BF16)
            dkv_ref[:, sl] += _dot(ds, q, TN)
            dkv_ref[:, vl] += _dot(p, do, TN)

    return _call(body, name="attn_bwd", grid=(S // tm,),
                 in_specs=[_row(tm, D), _row(tm, D, qcol), _row(tm, D, zcol), _par(M, 2 * D)],
                 out_specs=[_row(tm, D), _row(tm, D), _par(M, 2 * D)],
                 out_shape=[_sds((S, D), BF16), _sds((S, D), BF16), _sds((M, 2 * D), F32)],
                 compiler_params=_cp(("arbitrary",)))(d_cc, P, P, kv)


def _merge_fwd(yc, ym, yx, P, D, gcol):
    S = yc.shape[0]
    tm = min(ROW_TILE, S)

    def body(c_ref, m_ref, x_ref, gc_ref, gm_ref, gx_ref, o_ref):
        o_ref[...] = (_sg(gc_ref[...]) * c_ref[...] + _sg(gm_ref[...]) * m_ref[...]
                      + _sg(gx_ref[...]) * x_ref[...]).astype(BF16)

    return _call(body, name="merge_fwd", grid=(S // tm,),
                 in_specs=[_row(tm, D)] * 3 + [_row(tm, D, gcol), _row(tm, D, gcol + 1), _row(tm, D, gcol + 2)],
                 out_specs=_row(tm, D), out_shape=_sds((S, D), BF16), compiler_params=_cp(("parallel",)))(yc, ym, yx, P, P, P)


def _merge_bwd(dm, yc, ym, yx, P, D, gcol):
    S = yc.shape[0]
    tm = min(ROW_TILE, S)

    def body(d_ref, c_ref, m_ref, x_ref, gc_ref, gm_ref, gx_ref, dc_ref, dmm_ref, dx_ref, dgc_ref, dgm_ref, dgx_ref):
        d = d_ref[...]
        for y_ref, g_ref, dy_ref, dg_ref in ((c_ref, gc_ref, dc_ref, dgc_ref), (m_ref, gm_ref, dmm_ref, dgm_ref),
                                             (x_ref, gx_ref, dx_ref, dgx_ref)):
            s = _sg(g_ref[...])
            dy_ref[...] = (d * s).astype(BF16)
            dg_ref[...] = (d * y_ref[...] * s * (1.0 - s)).astype(BF16)

    return _call(body, name="merge_bwd", grid=(S // tm,),
                 in_specs=[_row(tm, D)] * 4 + [_row(tm, D, gcol), _row(tm, D, gcol + 1), _row(tm, D, gcol + 2)],
                 out_specs=[_row(tm, D)] * 6, out_shape=[_sds((S, D), BF16)] * 6,
                 compiler_params=_cp(("parallel",)))(dm, yc, ym, yx, P, P, P)


def _post(o, x, tgt, g):
    S, D = x.shape
    tm = min(ROW_TILE, S)

    def body(o_ref, x_ref, t_ref, g_ref, dy_ref, do_ref, dg_ref, l_ref):
        ov, gv = o_ref[...], g_ref[...]
        r = lax.rsqrt(_mean(ov * ov) + EPS)
        e = x_ref[...] + ov * r * gv - t_ref[...]
        dy = e / D
        dy_ref[...] = dy
        dxh = dy * gv
        do_ref[...] = (r * dxh - ov * (r * r * r) * _mean(dxh * ov)).astype(BF16)

        @pl.when(pl.program_id(0) == 0)
        def _():
            dg_ref[...] = jnp.zeros_like(dg_ref)
            l_ref[...] = jnp.zeros_like(l_ref)

        dg_ref[...] += jnp.sum(dy * ov * r, axis=0, keepdims=True)
        l_ref[...] += jnp.sum(e * e, axis=0, keepdims=True)

    return _call(body, name="post", grid=(S // tm,), in_specs=[_row(tm, D)] * 3 + [_par(1, D)],
                 out_specs=[_row(tm, D), _row(tm, D), _par(1, D), _par(1, D)],
                 out_shape=[_sds((S, D), F32), _sds((S, D), BF16), _sds((1, D), F32), _sds((1, D), F32)],
                 compiler_params=_cp(("arbitrary",)))(o, x, tgt, g)


def _chip_peers():
    x, y, c = lax.axis_index("x"), lax.axis_index("y"), lax.axis_index("c")
    return x, y, c, [(1 - x, y), (x, 1 - y), (1 - x, 1 - y)]


def _gather_chips(halved, whole):
    TH, TW = len(halved), len(whole)
    T = TH + TW

    def body(*refs):
        ins, outs = refs[:T], refs[T:2 * T]
        send, recv, fsend, frecv = refs[2 * T:]
        x, y, c, chips = _chip_peers()
        mine = 2 * x + y

        def over_ici(t, j, chip, slot):
            src = ins[t].at[c] if t < TH else ins[t]
            dst = outs[t].at[slot, c] if t < TH else outs[t].at[slot]
            return pltpu.make_async_remote_copy(src_ref=src, dst_ref=dst, send_sem=send.at[t, j], recv_sem=recv.at[t, j],
                                                device_id=(*chip, c), device_id_type=MESH)

        def to_sibling(t, j, slot, half):
            place = outs[t].at[slot, half]
            return pltpu.make_async_remote_copy(src_ref=place, dst_ref=place, send_sem=fsend.at[t, j], recv_sem=frecv.at[t, j],
                                                device_id=(x, y, 1 - c), device_id_type=MESH)

        sends = []
        for t in range(T):
            for j, chip in enumerate(chips):
                cp = over_ici(t, j, chip, mine)
                cp.start()
                sends.append(cp)
        for t in range(T):
            for j, (px, py) in enumerate(chips):
                over_ici(t, j, (px, py), 2 * px + py).wait_recv()
                if t < TH:
                    cp = to_sibling(t, j, 2 * px + py, c)
                    cp.start()
                    sends.append(cp)
        for t in range(TH):
            for j, (px, py) in enumerate(chips):
                to_sibling(t, j, 2 * px + py, 1 - c).wait_recv()
        for cp in sends:
            cp.wait_send()

    anyspec = pl.BlockSpec(memory_space=pl.ANY)
    dma = pltpu.SemaphoreType.DMA
    return _call(body, name="gather_weights", in_specs=[anyspec] * T, out_specs=[anyspec] * T,
                 out_shape=[_sds((4, *s.shape), s.dtype) for s in (*halved, *whole)],
                 scratch_shapes=[dma((T, 3)), dma((T, 3)), dma((TH, 3)), dma((TH, 3))],
                 compiler_params=pltpu.CompilerParams(has_side_effects=True))(*halved, *whole)


def _to_sibling(tensors, name):
    T = len(tensors)

    def body(*refs):
        ins, outs = refs[:T], refs[T:2 * T]
        send, recv = refs[2 * T:]
        x, y, c = lax.axis_index("x"), lax.axis_index("y"), lax.axis_index("c")
        cps = [pltpu.make_async_remote_copy(src_ref=ins[t], dst_ref=outs[t], send_sem=send.at[t], recv_sem=recv.at[t],
                                            device_id=(x, y, 1 - c), device_id_type=MESH) for t in range(T)]
        for cp in cps:
            cp.start()
        for cp in cps:
            cp.wait_recv()
        for cp in cps:
            cp.wait_send()

    anyspec = pl.BlockSpec(memory_space=pl.ANY)
    return _call(body, name=name, in_specs=[anyspec] * T, out_specs=[anyspec] * T,
                 out_shape=[_sds(a.shape, a.dtype) for a in tensors],
                 scratch_shapes=[pltpu.SemaphoreType.DMA((T,)), pltpu.SemaphoreType.DMA((T,))],
                 compiler_params=pltpu.CompilerParams(has_side_effects=True))(*tensors)


def _chip_scatter(tensors):
    T = len(tensors)

    def body(*refs):
        ins, outs = refs[:T], refs[T:2 * T]
        send, recv = refs[2 * T:]
        x, y, c, chips = _chip_peers()
        mine = 2 * x + y

        def over_ici(t, j, chip, src_slot, dst_slot):
            return pltpu.make_async_remote_copy(src_ref=ins[t].at[src_slot], dst_ref=outs[t].at[dst_slot], send_sem=send.at[t, j],
                                                recv_sem=recv.at[t, j], device_id=(*chip, c), device_id_type=MESH)

        sends = []
        for t in range(T):
            for j, (px, py) in enumerate(chips):
                cp = over_ici(t, j, (px, py), 2 * px + py, mine)
                cp.start()
                sends.append(cp)
        for t in range(T):
            for j, (px, py) in enumerate(chips):
                over_ici(t, j, (px, py), mine, 2 * px + py).wait_recv()
        for cp in sends:
            cp.wait_send()

    anyspec = pl.BlockSpec(memory_space=pl.ANY)
    dma = pltpu.SemaphoreType.DMA
    return _call(body, name="chip_scatter", in_specs=[anyspec] * T, out_specs=[anyspec] * T,
                 out_shape=[_sds(a.shape, a.dtype) for a in tensors], scratch_shapes=[dma((T, 3)), dma((T, 3))],
                 compiler_params=pltpu.CompilerParams(has_side_effects=True))(*tensors)


def _row_tile(R, C, slots):
    tr = max(16, ((8 << 20) // (C * 4 * slots)) // 16 * 16)
    if tr >= R:
        return R
    while R % tr:
        tr -= 16
    return tr


def _add_pair(a, b, name):
    lead, R, C = a.shape
    tr = _row_tile(R, C, 3)

    def body(a_ref, b_ref, o_ref):
        o_ref[...] = (a_ref[...].astype(F32) + b_ref[...].astype(F32)).astype(BF16)

    blk = pl.BlockSpec((None, tr, C), lambda k, i: (k, i, 0))
    return _call(body, name=name, grid=(lead, R // tr), in_specs=[blk, blk], out_specs=blk, out_shape=_sds(a.shape, BF16),
                 compiler_params=_cp(("parallel", "parallel")))(a, b)


def _sum_chips(land, name):
    _, R, C = land.shape
    tr = _row_tile(R, C, 5)

    def body(l_ref, o_ref):
        o_ref[...] = ((l_ref[0].astype(F32) + l_ref[1].astype(F32)) + l_ref[2].astype(F32)) + l_ref[3].astype(F32)

    return _call(body, name=name, grid=(R // tr,), in_specs=[pl.BlockSpec((4, tr, C), lambda i: (0, i, 0))],
                 out_specs=pl.BlockSpec((tr, C), lambda i: (i, 0)), out_shape=_sds((R, C), F32),
                 compiler_params=_cp(("parallel",)))(land)


def _scatter_grads(blocks, small):
    T = len(blocks)

    def body(*refs):
        ins, sm = refs[:T], refs[T]
        outs, smo = refs[T + 1:2 * T + 1], refs[2 * T + 1]
        send, recv, loc = refs[2 * T + 2:]
        x, y, c = lax.axis_index("x"), lax.axis_index("y"), lax.axis_index("c")
        me = 4 * x + 2 * y + c
        peers = [(x ^ ((k >> 2) & 1), y ^ ((k >> 1) & 1), c ^ (k & 1)) for k in range(1, N_DEV)]
        copies = []
        for t in range(T):
            lc = pltpu.make_async_copy(ins[t].at[2 * x + y], outs[t].at[me], loc.at[t])
            lc.start()
            copies.append(lc)
        lc = pltpu.make_async_copy(sm, smo.at[me], loc.at[T])
        lc.start()
        copies.append(lc)

        def remote(t, j, peer, sender):
            px, py, pc = peer
            src = sm if t == T else ins[t].at[2 * px + py]
            dst = (smo if t == T else outs[t]).at[sender]
            return pltpu.make_async_remote_copy(src_ref=src, dst_ref=dst, send_sem=send.at[t, j], recv_sem=recv.at[t, j],
                                                device_id=peer, device_id_type=MESH)

        sends = []
        for t in range(T + 1):
            for j, peer in enumerate(peers):
                cp = remote(t, j, peer, me)
                cp.start()
                sends.append(cp)
        for t in range(T + 1):
            for j, (px, py, pc) in enumerate(peers):
                remote(t, j, (x, y, c), 4 * px + 2 * py + pc).wait_recv()
        for cp in sends:
            cp.wait_send()
        for lc in copies:
            lc.wait()

    anyspec = pl.BlockSpec(memory_space=pl.ANY)
    return _call(body, name="scatter_grads", in_specs=[anyspec] * (T + 1), out_specs=[anyspec] * (T + 1),
                 out_shape=[_sds((N_DEV, *b.shape[1:]), b.dtype) for b in blocks] + [_sds((N_DEV, *small.shape), small.dtype)],
                 scratch_shapes=[pltpu.SemaphoreType.DMA((T + 1, N_DEV - 1)), pltpu.SemaphoreType.DMA((T + 1, N_DEV - 1)),
                                 pltpu.SemaphoreType.DMA((T + 1,))],
                 compiler_params=pltpu.CompilerParams(has_side_effects=True))(*blocks, small)


def _adam_update(g, w_ref, m_ref, v_ref, g_ref, d_ref, mo_ref, vo_ref):
    c1, c2 = 1.0 - ADAM_B1 ** ADAM_STEP, 1.0 - ADAM_B2 ** ADAM_STEP
    mn = ADAM_B1 * m_ref[...] + (1.0 - ADAM_B1) * g
    vn = ADAM_B2 * v_ref[...] + (1.0 - ADAM_B2) * (g * g)
    g_ref[...] = g
    mo_ref[...] = mn
    vo_ref[...] = vn
    d_ref[...] = -ADAM_LR * ((mn / c1) / (jnp.sqrt(vn / c2) + ADAM_EPS) + ADAM_WD * w_ref[...])


def _adam_rows(R, C):
    tr = R if R * C * 4 <= (1 << 20) else max(16, ((1 << 20) // (C * 4)) // 16 * 16)
    while R % tr:
        tr -= 16
    return tr


def _adamw(land, w, m, v, name):
    R, C = w.shape
    tr = _adam_rows(R, C)

    def body(l_ref, w_ref, m_ref, v_ref, *outs):
        g = l_ref[0].astype(F32)
        for k in range(1, N_DEV):
            g = g + l_ref[k].astype(F32)
        _adam_update(g, w_ref, m_ref, v_ref, *outs)

    blk = pl.BlockSpec((tr, C), lambda i: (i, 0))
    return _call(body, name=name, grid=(R // tr,), in_specs=[pl.BlockSpec((N_DEV, tr, C), lambda i: (0, i, 0)), blk, blk, blk],
                 out_specs=[blk] * 4, out_shape=[_sds((R, C), F32)] * 4, compiler_params=_cp(("parallel",)))(land, w, m, v)


def _adamw_halves(mine, theirs, w, m, v, name):
    r, C = mine.shape
    tr = _adam_rows(r, C)
    nt = r // tr

    def body(a_ref, b_ref, w_ref, m_ref, v_ref, *outs):
        g = jnp.where(pl.program_id(0) == lax.axis_index("c"), a_ref[...], b_ref[...])
        _adam_update(g, w_ref, m_ref, v_ref, *outs)

    half = pl.BlockSpec((tr, C), lambda hh, i: (i, 0))
    blk = pl.BlockSpec((tr, C), lambda hh, i: (hh * nt + i, 0))
    return _call(body, name=name, grid=(2, nt), in_specs=[half, half, blk, blk, blk], out_specs=[blk] * 4,
                 out_shape=[_sds((2 * r, C), F32)] * 4, compiler_params=_cp(("parallel", "parallel")))(mine, theirs, w, m, v)


def _pad_rows(a, rows):
    return jnp.pad(a, ((0, rows - a.shape[0]), (0, 0)))


def _local_step(x, mem, tgt, g_pre, w_main, w_if, b_if, wq, wk, w_dw, b_dw, g_ln, b_ln, w_conv_out, g_ml_head, w_ml_out,
                g_mem, w_mem_kv, w_xa_out, w_out, g_post):
    S, D = x.shape
    row = lambda a: a.reshape(1, -1)
    bif = jnp.pad(b_if, (0, 128 - b_if.shape[0])).reshape(1, 128)
    wdw, wq8, wk8 = _pad_rows(w_dw, HALO), _pad_rows(wq, QK_HALO), _pad_rows(wk, QK_HALO)

    h = _rms_fwd(x, row(g_pre), "rms_pre")
    P = _mm(h, w_main, "nn", F32, "proj_in")
    gif = _mm(h, w_if, "nn", F32, "proj_if")
    ca, u1 = _conv_fwd(P, wdw, row(b_dw), row(g_ln), row(b_ln), D)
    q, k, cq, ck = _qk_fwd(P, wq8, wk8, D, 3)
    gl = _gates_fwd(gif, bif)
    g8 = gl[:, :8]
    gt = g8.T
    hm, cs, ns, ms = _mlstm_fwd(q, k, P, g8, gt, D, 5)
    cb = _ml_post_fwd(hm, P, row(g_ml_head), D, 6, 7)
    mn = _rms_fwd(mem, row(g_mem), "rms_mem")
    kv = _mm(mn, w_mem_kv, "nn", BF16, "proj_kv")
    cc = _attn_fwd(P, kv, D, 8, 9)
    yc = _mm(ca, w_conv_out, "nn", F32, "out_conv")
    ym = _mm(cb, w_ml_out, "nn", F32, "out_ml")
    yx = _mm(cc, w_xa_out, "nn", F32, "out_xa")
    merged = _merge_fwd(yc, ym, yx, P, D, 10)
    o = _mm(merged, w_out, "nn", F32, "out_proj")
    dy, d_o, gg_post, sq = _post(o, x, tgt, row(g_post))

    d_merged = _mm(d_o, w_out, "nt", F32, "d_merged")
    gw_out = _mm(merged, d_o, "tn", BF16, "gw_out")
    d_yc, d_ym, d_yx, d_gc, d_gm, d_gx = _merge_bwd(d_merged, yc, ym, yx, P, D, 10)
    d_ca = _mm(d_yc, w_conv_out, "nt", F32, "d_ca")
    gw_conv_out = _mm(ca, d_yc, "tn", BF16, "gw_conv_out")
    d_cb = _mm(d_ym, w_ml_out, "nt", F32, "d_cb")
    gw_ml_out = _mm(cb, d_ym, "tn", BF16, "gw_ml_out")
    d_cc = _mm(d_yx, w_xa_out, "nt", F32, "d_cc")
    gw_xa_out = _mm(cc, d_yx, "tn", BF16, "gw_xa_out")

    d_u1, d_zc, gg_ln, gb_ln = _conv_bwd_local(d_ca, u1, P, row(g_ln), row(b_ln), D)
    d_a, d_b, gw_dw, gb_dw = _conv_bwd_taps(d_u1, P, wdw, D)

    d_qx, d_zx, d_kv = _attn_bwd(d_cc, P, kv, D, 8, 9)
    d_kvb = d_kv.astype(BF16)
    gw_mem_kv = _mm(mn, d_kvb, "tn", BF16, "gw_mem_kv")
    d_mn = _mm(d_kvb, w_mem_kv, "nt", F32, "d_mn")
    _, gg_mem = _rms_bwd(mem, row(g_mem), [d_mn], None, "rms_mem_bwd")

    d_hm, d_om, d_zm, gg_ml = _ml_post_bwd(d_cb, hm, P, row(g_ml_head), D, 6, 7)
    dq, dk, d_v, dg = _mlstm_bwd(d_hm, q, k, P, g8, gt, cs, ns, ms, D, 5)
    d_qp, d_kp, gwq, gwk = _qk_bwd(dq, dk, cq, ck, P, wq8, wk8, D, 3)
    dgl = jnp.pad(jnp.concatenate([dg[:, 0, :].T, dg[:, 1, :].T], axis=1), ((0, 0), (0, 128 - 2 * N_HEADS)))
    d_gif, gb_if = _gates_bwd(dgl, gif, bif)

    dP = jnp.concatenate([d_a, d_b, d_zc, d_qp, d_kp, d_v, d_om, d_zm, d_qx, d_zx, d_gc, d_gm, d_gx], axis=1)
    d_h1 = _mm(dP, w_main, "nt", F32, "d_h_main")
    d_h2 = _mm(d_gif, w_if, "nt", F32, "d_h_if")
    gw_main = _mm(h, dP, "tn", BF16, "gw_main")
    gw_if = _mm(h, d_gif, "tn", BF16, "gw_if")
    grad_x, gg_pre = _rms_bwd(x, row(g_pre), [d_h1, d_h2], dy, "rms_pre_bwd")

    grads = dict(g_pre=gg_pre[0], w_main=gw_main, w_if=gw_if, b_if=gb_if[0, :2 * N_HEADS], wq=gwq[:QK_W], wk=gwk[:QK_W],
                 w_dw=gw_dw[:CONV_W], b_dw=gb_dw[0], g_ln=gg_ln[0], b_ln=gb_ln[0], w_conv_out=gw_conv_out,
                 g_ml_head=gg_ml[0], w_ml_out=gw_ml_out, g_mem=gg_mem[0], w_mem_kv=gw_mem_kv, w_xa_out=gw_xa_out,
                 w_out=gw_out, g_post=gg_post[0])
    return jnp.sum(sq), grad_x, grads


SMALL = ("g_pre", "b_if", "b_dw", "g_ln", "b_ln", "g_ml_head", "g_mem", "g_post")


def kernel(x, mem, g_pre, w_in, b_if, w_qk_conv, w_dw, b_dw, g_ln, b_ln, w_conv_out, g_ml_head, w_ml_out, g_mem, w_mem_kv, w_xa_out, w_out, g_post, loss_target, m_g_pre, m_w_in, m_b_if, m_w_qk_conv, m_w_dw, m_b_dw, m_g_ln, m_b_ln, m_w_conv_out, m_g_ml_head, m_w_ml_out, m_g_mem, m_w_mem_kv, m_w_xa_out, m_w_out, m_g_post, v_g_pre, v_w_in, v_b_if, v_w_qk_conv, v_w_dw, v_b_dw, v_g_ln, v_b_ln, v_w_conv_out, v_g_ml_head, v_w_ml_out, v_g_mem, v_w_mem_kv, v_w_xa_out, v_w_out, v_g_post):
    W = dict(g_pre=g_pre, w_in=w_in, b_if=b_if, w_qk_conv=w_qk_conv, w_dw=w_dw, b_dw=b_dw, g_ln=g_ln, b_ln=b_ln,
             w_conv_out=w_conv_out, g_ml_head=g_ml_head, w_ml_out=w_ml_out, g_mem=g_mem, w_mem_kv=w_mem_kv,
             w_xa_out=w_xa_out, w_out=w_out, g_post=g_post)
    Mo = dict(g_pre=m_g_pre, w_in=m_w_in, b_if=m_b_if, w_qk_conv=m_w_qk_conv, w_dw=m_w_dw, b_dw=m_b_dw, g_ln=m_g_ln,
              b_ln=m_b_ln, w_conv_out=m_w_conv_out, g_ml_head=m_g_ml_head, w_ml_out=m_w_ml_out, g_mem=m_g_mem,
              w_mem_kv=m_w_mem_kv, w_xa_out=m_w_xa_out, w_out=m_w_out, g_post=m_g_post)
    Vo = dict(g_pre=v_g_pre, w_in=v_w_in, b_if=v_b_if, w_qk_conv=v_w_qk_conv, w_dw=v_w_dw, b_dw=v_b_dw, g_ln=v_g_ln,
              b_ln=v_b_ln, w_conv_out=v_w_conv_out, g_ml_head=v_g_ml_head, w_ml_out=v_w_ml_out, g_mem=v_g_mem,
              w_mem_kv=v_w_mem_kv, w_xa_out=v_w_xa_out, w_out=v_w_out, g_post=v_g_post)
    D = x.shape[-1]
    n_in = 4 * w_in.shape[1]
    off_if = 8 * D

    big = ("w_in", "w_conv_out", "w_ml_out", "w_mem_kv", "w_xa_out", "w_out")
    halves = lambda a: a.reshape(2, a.shape[0] // 2, a.shape[1])
    own = [halves(W[n].astype(BF16)) for n in big] + [w_qk_conv, w_dw]
    my_chip = 2 * lax.axis_index("x") + lax.axis_index("y")
    got = [lax.dynamic_update_index_in_dim(g, o, my_chip, 0) for g, o in zip(_gather_chips(own[:len(big)], own[len(big):]), own)]
    got = [g.reshape(4, -1, g.shape[-1]) for g in got]
    cols = lambda a: jnp.transpose(a, (1, 0, 2)).reshape(a.shape[1], -1)
    rows = lambda a: a.reshape(-1, a.shape[2])
    w_in_f = cols(got[0])
    w_main = jnp.concatenate([w_in_f[:, :off_if], w_in_f[:, off_if + 2 * N_HEADS:]], axis=1)
    w_if = jnp.pad(w_in_f[:, off_if:off_if + 2 * N_HEADS], ((0, 0), (0, 128 - 2 * N_HEADS)))
    wqk_f = cols(got[6])

    sq, grad_x, G = _local_step(
        x[0], mem[0], loss_target[0], g_pre, w_main, w_if, b_if, wqk_f[:, :D], wqk_f[:, D:], cols(got[7]), b_dw, g_ln, b_ln,
        rows(got[1]), g_ml_head, rows(got[2]), g_mem, cols(got[3]), rows(got[4]), rows(got[5]), g_post)
    loss = lax.psum(0.5 * sq / D, ("x", "y", "c"))

    colblk = lambda a: jnp.transpose(a.reshape(a.shape[0], 4, -1), (1, 0, 2))
    colblk_h = lambda a: jnp.transpose(a.reshape(2, a.shape[0] // 2, 4, -1), (0, 2, 1, 3))
    rowblk_h = lambda a: jnp.transpose(a.reshape(4, 2, -1, a.shape[1]), (1, 0, 2, 3))
    gw_in = jnp.concatenate([G["w_main"][:, :off_if], G["w_if"][:, :2 * N_HEADS], G["w_main"][:, off_if:]], axis=1)
    assert gw_in.shape[1] == n_in
    gh = dict(w_in=colblk_h(gw_in), w_conv_out=rowblk_h(G["w_conv_out"]), w_ml_out=rowblk_h(G["w_ml_out"]),
              w_mem_kv=colblk_h(G["w_mem_kv"]), w_xa_out=rowblk_h(G["w_xa_out"]), w_out=rowblk_h(G["w_out"]))
    c = lax.axis_index("c")
    gh = {n: a.astype(BF16) for n, a in gh.items()}
    mine = [lax.dynamic_index_in_dim(gh[n], c, 0, keepdims=False) for n in big]
    theirs = _to_sibling([lax.dynamic_index_in_dim(gh[n], 1 - c, 0, keepdims=False) for n in big], "pair_swap")
    chip_sums = [_add_pair(a, b, "chip_sum_" + n) for n, a, b in zip(big, mine, theirs)]
    landed = [lax.dynamic_update_index_in_dim(l, lax.dynamic_index_in_dim(s, my_chip, 0), my_chip, 0)
              for l, s in zip(_chip_scatter(chip_sums), chip_sums)]
    halves_mine = [_sum_chips(a, "sum_chips_" + n) for n, a in zip(big, landed)]
    halves_theirs = _to_sibling(halves_mine, "share_halves")
    out = {}
    for n, a, b in zip(big, halves_mine, halves_theirs):
        out[n] = _adamw_halves(a, b, W[n], Mo[n], Vo[n], "adamw_" + n)

    pad = lambda a: jnp.pad(a, (0, D - a.shape[0]))
    small_g = jnp.stack([pad(G[n]) for n in SMALL])
    lands = _scatter_grads([colblk(jnp.concatenate([G["wq"], G["wk"]], axis=1)), colblk(G["w_dw"])], small_g)
    for n, land in zip(("w_qk_conv", "w_dw"), lands[:-1]):
        out[n] = _adamw(land, W[n], Mo[n], Vo[n], "adamw_" + n)
    res = _adamw(lands[-1], jnp.stack([pad(W[n]) for n in SMALL]), jnp.stack([pad(Mo[n]) for n in SMALL]),
                 jnp.stack([pad(Vo[n]) for n in SMALL]), "adamw_small")
    for i, n in enumerate(SMALL):
        out[n] = tuple(r[i, :W[n].shape[0]] for r in res)
    order = ("g_pre", "w_in", "b_if", "w_qk_conv", "w_dw", "b_dw", "g_ln", "b_ln", "w_conv_out", "g_ml_head", "w_ml_out",
             "g_mem", "w_mem_kv", "w_xa_out", "w_out", "g_post")
    return (loss, grad_x[None], *[out[n][0] for n in order], *[out[n][1] for n in order], *[out[n][2] for n in order],
            *[out[n][3] for n in order])
```

```python
import jax
import jax.numpy as jnp
from jax import lax
from jax.experimental import pallas as pl
from jax.experimental.pallas import tpu as pltpu

F32, BF16 = jnp.float32, jnp.bfloat16
EPS = 1e-6
N_HEADS = 4
CONV_W = 31
QK_W = 4
HALO = 32
QK_HALO = 8
ROW_TILE = 256
ML_CHUNK = 256
VMEM_LIMIT = 56 * 1024 * 1024
NEG = -1e30
ADAM_LR, ADAM_B1, ADAM_B2, ADAM_EPS, ADAM_WD, ADAM_STEP = 0.001, 0.9, 0.999, 1e-08, 0.01, 10
MESH = pl.DeviceIdType.MESH
N_DEV = 8

NN = (((1,), (0,)), ((), ()))
NT = (((1,), (1,)), ((), ()))
TN = (((0,), (0,)), ((), ()))


def _call(body, **kw):
    return pl.pallas_call(body, **kw)


def _cp(sem):
    return pltpu.CompilerParams(dimension_semantics=sem, vmem_limit_bytes=VMEM_LIMIT)


def _sds(shape, dt):
    return jax.ShapeDtypeStruct(shape, dt)


def _row(tm, d, col=0):
    return pl.BlockSpec((tm, d), lambda i: (i, col))


def _par(r, d):
    return pl.BlockSpec((r, d), lambda i: (0, 0))


def _prev(tm, hb, d, col=0):
    return pl.BlockSpec((hb, d), lambda i: (jnp.maximum(i * (tm // hb) - 1, 0), col))


def _next(tm, hb, d, nblk, col=0):
    return pl.BlockSpec((hb, d), lambda i: (jnp.minimum((i + 1) * (tm // hb), nblk - 1), col))


def _dot(a, b, dn):
    return lax.dot_general(a.astype(BF16), b.astype(BF16), dn, preferred_element_type=F32)


def _sg(x):
    return jax.nn.sigmoid(x)


def _dsilu(z, s):
    return s * (1.0 + z * (1.0 - s))


def _mean(x):
    return jnp.mean(x, axis=-1, keepdims=True)


def _pick(n, pref):
    if n <= pref:
        return n
    t = pref
    while n % t:
        t -= 128
    return t


def _mm(a, b, mode, out_dtype, name, tm=1024, tn=1024, tk=2048):
    if mode == "nn":
        (M, K), N = a.shape, b.shape[1]
    elif mode == "nt":
        (M, K), N = a.shape, b.shape[0]
    else:
        (K, M), N = a.shape, b.shape[1]
    tm, tn, tk = _pick(M, tm), _pick(N, tn), _pick(K, tk)
    nk = K // tk
    if mode == "nn":
        sa, sb, dn = pl.BlockSpec((tm, tk), lambda i, j, k: (i, k)), pl.BlockSpec((tk, tn), lambda i, j, k: (k, j)), NN
    elif mode == "nt":
        sa, sb, dn = pl.BlockSpec((tm, tk), lambda i, j, k: (i, k)), pl.BlockSpec((tn, tk), lambda i, j, k: (j, k)), NT
    else:
        sa, sb, dn = pl.BlockSpec((tk, tm), lambda i, j, k: (k, i)), pl.BlockSpec((tk, tn), lambda i, j, k: (k, j)), TN

    def body(a_ref, b_ref, o_ref, *acc):
        p = _dot(a_ref[...], b_ref[...], dn)
        if nk == 1:
            o_ref[...] = p.astype(out_dtype)
        else:
            acc_ref, k = acc[0], pl.program_id(2)

            @pl.when(k == 0)
            def _():
                acc_ref[...] = p

            @pl.when(k > 0)
            def _():
                acc_ref[...] += p

            @pl.when(k == nk - 1)
            def _():
                o_ref[...] = acc_ref[...].astype(out_dtype)

    return _call(body, name=name, grid=(M // tm, N // tn, nk), in_specs=[sa, sb],
                 out_specs=pl.BlockSpec((tm, tn), lambda i, j, k: (i, j)), out_shape=_sds((M, N), out_dtype),
                 scratch_shapes=[pltpu.VMEM((tm, tn), F32)] if nk > 1 else [],
                 compiler_params=_cp(("parallel", "parallel", "arbitrary")))(a, b)


def _rms_fwd(x, g, name):
    S, D = x.shape
    tm = min(ROW_TILE, S)

    def body(x_ref, g_ref, h_ref):
        xv = x_ref[...]
        r = lax.rsqrt(_mean(xv * xv) + EPS)
        h_ref[...] = (xv * r * g_ref[...]).astype(BF16)

    return _call(body, name=name, grid=(S // tm,), in_specs=[_row(tm, D), _par(1, D)], out_specs=_row(tm, D),
                 out_shape=_sds((S, D), BF16), compiler_params=_cp(("parallel",)))(x, g)


def _rms_bwd(x, g, ds, resid, name):
    S, D = x.shape
    tm = min(ROW_TILE, S)
    nd = len(ds)

    def body(*refs):
        x_ref, g_ref = refs[:2]
        d_refs = refs[2:2 + nd]
        r_ref = refs[2 + nd] if resid is not None else None
        dx_ref, dg_ref = refs[-2:]
        xv = x_ref[...]
        d = d_refs[0][...]
        for dr in d_refs[1:]:
            d = d + dr[...]
        r = lax.rsqrt(_mean(xv * xv) + EPS)
        dxh = d * g_ref[...]
        dx = r * dxh - xv * (r * r * r) * _mean(dxh * xv)
        if r_ref is not None:
            dx = dx + r_ref[...]
        dx_ref[...] = dx

        @pl.when(pl.program_id(0) == 0)
        def _():
            dg_ref[...] = jnp.zeros_like(dg_ref)

        dg_ref[...] += jnp.sum(d * xv * r, axis=0, keepdims=True)

    ins = [x, g, *ds] + ([resid] if resid is not None else [])
    specs = [_row(tm, D), _par(1, D)] + [_row(tm, D)] * (len(ins) - 2)
    return _call(body, name=name, grid=(S // tm,), in_specs=specs, out_specs=[_row(tm, D), _par(1, D)],
                 out_shape=[_sds((S, D), F32), _sds((1, D), F32)], compiler_params=_cp(("arbitrary",)))(*ins)


def _shifted_views(buf, sh, shifts, tm):
    for r in range(8):
        group = [(i, s) for i, s in enumerate(shifts) if s % 8 == r]
        if not group:
            continue
        if r:
            top = max(s for _, s in group) - r + tm
            sh[0:top, :] = buf[r:r + top, :]
        src = sh if r else buf
        for i, s in group:
            yield i, src[s - r:s - r + tm, :]


def _conv_fwd(P, w_dw, b_dw, g_ln, b_ln, D):
    S = P.shape[0]
    tm = min(ROW_TILE, S)

    def body(a_ref, b_ref, ap_ref, bp_ref, z_ref, w_ref, bd_ref, g_ref, bl_ref, ca_ref, u1_ref, buf, sh):
        first = pl.program_id(0) == 0
        hp = ap_ref[...] * _sg(bp_ref[...])
        buf[0:HALO, :] = jnp.where(first, 0.0, hp)
        buf[HALO:HALO + tm, :] = a_ref[...] * _sg(b_ref[...])
        acc = jnp.zeros((tm, D), F32)
        for j, view in _shifted_views(buf, sh, [HALO - CONV_W + 1 + j for j in range(CONV_W)], tm):
            acc = acc + w_ref[j:j + 1, :] * view
        u1 = acc + bd_ref[...]
        u1_ref[...] = u1
        xc = u1 - _mean(u1)
        r = lax.rsqrt(_mean(xc * xc) + EPS)
        u2 = xc * r * g_ref[...] + bl_ref[...]
        z = z_ref[...]
        ca_ref[...] = (u2 * _sg(u2) * (z * _sg(z))).astype(BF16)

    return _call(body, name="conv_fwd", grid=(S // tm,),
                 in_specs=[_row(tm, D, 0), _row(tm, D, 1), _prev(tm, HALO, D, 0), _prev(tm, HALO, D, 1), _row(tm, D, 2),
                           _par(HALO, D), _par(1, D), _par(1, D), _par(1, D)],
                 out_specs=[_row(tm, D), _row(tm, D)], out_shape=[_sds((S, D), BF16), _sds((S, D), F32)],
                 scratch_shapes=[pltpu.VMEM((tm + HALO, D), F32)] * 2, compiler_params=_cp(("parallel",)))(
        P, P, P, P, P, w_dw, b_dw, g_ln, b_ln)


def _conv_bwd_local(d_ca, u1, P, g_ln, b_ln, D):
    S = P.shape[0]
    tm = min(ROW_TILE, S)

    def body(dca_ref, u1_ref, z_ref, g_ref, bl_ref, du1_ref, dz_ref, dg_ref, db_ref):
        u1, z, dca, g = u1_ref[...], z_ref[...], dca_ref[...], g_ref[...]
        xc = u1 - _mean(u1)
        r = lax.rsqrt(_mean(xc * xc) + EPS)
        xh = xc * r
        u2 = xh * g + bl_ref[...]
        s2, sz = _sg(u2), _sg(z)
        d_u3 = dca * (z * sz)
        dz_ref[...] = (dca * (u2 * s2) * _dsilu(z, sz)).astype(BF16)
        d_u2 = d_u3 * _dsilu(u2, s2)
        dxh = d_u2 * g
        du1_ref[...] = r * (dxh - _mean(dxh) - xh * _mean(dxh * xh))

        @pl.when(pl.program_id(0) == 0)
        def _():
            dg_ref[...] = jnp.zeros_like(dg_ref)
            db_ref[...] = jnp.zeros_like(db_ref)

        dg_ref[...] += jnp.sum(d_u2 * xh, axis=0, keepdims=True)
        db_ref[...] += jnp.sum(d_u2, axis=0, keepdims=True)

    return _call(body, name="conv_bwd_local", grid=(S // tm,),
                 in_specs=[_row(tm, D), _row(tm, D), _row(tm, D, 2), _par(1, D), _par(1, D)],
                 out_specs=[_row(tm, D), _row(tm, D), _par(1, D), _par(1, D)],
                 out_shape=[_sds((S, D), F32), _sds((S, D), BF16), _sds((1, D), F32), _sds((1, D), F32)],
                 compiler_params=_cp(("arbitrary",)))(d_ca, u1, P, g_ln, b_ln)


def _conv_bwd_taps(d_u1, P, w_dw, D):
    S = P.shape[0]
    tm = min(ROW_TILE, S)
    nblk = S // HALO

    def body(d_ref, dn_ref, a_ref, b_ref, ap_ref, bp_ref, w_ref, da_ref, db_ref, gw_ref, gb_ref, dbuf, ubuf, sh):
        i = pl.program_id(0)
        a, sb = a_ref[...], _sg(b_ref[...])
        d = d_ref[...]
        dbuf[0:tm, :] = d
        dbuf[tm:tm + HALO, :] = jnp.where(i == pl.num_programs(0) - 1, 0.0, dn_ref[...])
        ubuf[0:HALO, :] = jnp.where(i == 0, 0.0, ap_ref[...] * _sg(bp_ref[...]))
        ubuf[HALO:HALO + tm, :] = a * sb

        @pl.when(i == 0)
        def _():
            gw_ref[...] = jnp.zeros_like(gw_ref)
            gb_ref[...] = jnp.zeros_like(gb_ref)

        acc = jnp.zeros((tm, D), F32)
        for j, view in _shifted_views(dbuf, sh, [CONV_W - 1 - j for j in range(CONV_W)], tm):
            acc = acc + w_ref[j:j + 1, :] * view
        for j, view in _shifted_views(ubuf, sh, [HALO - CONV_W + 1 + j for j in range(CONV_W)], tm):
            gw_ref[j:j + 1, :] += jnp.sum(d * view, axis=0, keepdims=True)
        gb_ref[...] += jnp.sum(d, axis=0, keepdims=True)
        da_ref[...] = (acc * sb).astype(BF16)
        db_ref[...] = (acc * a * sb * (1.0 - sb)).astype(BF16)

    return _call(body, name="conv_bwd_taps", grid=(S // tm,),
                 in_specs=[_row(tm, D), _next(tm, HALO, D, nblk), _row(tm, D, 0), _row(tm, D, 1),
                           _prev(tm, HALO, D, 0), _prev(tm, HALO, D, 1), _par(HALO, D)],
                 out_specs=[_row(tm, D), _row(tm, D), _par(HALO, D), _par(1, D)],
                 out_shape=[_sds((S, D), BF16), _sds((S, D), BF16), _sds((HALO, D), F32), _sds((1, D), F32)],
                 scratch_shapes=[pltpu.VMEM((tm + HALO, D), F32)] * 3,
                 compiler_params=_cp(("arbitrary",)))(d_u1, d_u1, P, P, P, P, w_dw)


def _qk_fwd(P, wq, wk, D, col):
    S = P.shape[0]
    tm = min(ROW_TILE, S)
    scale = float(D // N_HEADS) ** -0.5

    def body(q_ref, k_ref, qp_ref, kp_ref, wq_ref, wk_ref, qo_ref, ko_ref, cq_ref, ck_ref, buf):
        first = pl.program_id(0) == 0
        for x_ref, p_ref, w_ref, o_ref, c_ref, sc in ((q_ref, qp_ref, wq_ref, qo_ref, cq_ref, 1.0),
                                                      (k_ref, kp_ref, wk_ref, ko_ref, ck_ref, scale)):
            buf[0:QK_HALO, :] = jnp.where(first, 0.0, p_ref[...])
            buf[QK_HALO:QK_HALO + tm, :] = x_ref[...]
            acc = jnp.zeros((tm, D), F32)
            for j in range(QK_W):
                sh = QK_HALO - QK_W + 1 + j
                acc = acc + w_ref[j:j + 1, :] * buf[sh:sh + tm, :]
            c_ref[...] = acc
            o_ref[...] = acc * _sg(acc) * sc

    return _call(body, name="qk_fwd", grid=(S // tm,),
                 in_specs=[_row(tm, D, col), _row(tm, D, col + 1), _prev(tm, QK_HALO, D, col), _prev(tm, QK_HALO, D, col + 1),
                           _par(QK_HALO, D), _par(QK_HALO, D)],
                 out_specs=[_row(tm, D)] * 4, out_shape=[_sds((S, D), F32)] * 4,
                 scratch_shapes=[pltpu.VMEM((tm + QK_HALO, D), F32)], compiler_params=_cp(("parallel",)))(P, P, P, P, wq, wk)


def _qk_bwd(dq, dk, cq, ck, P, wq, wk, D, col):
    S = P.shape[0]
    tm = min(ROW_TILE, S)
    nblk = S // QK_HALO
    scale = float(D // N_HEADS) ** -0.5

    def body(dq_ref, dqn_ref, cq_ref, cqn_ref, xq_ref, xqp_ref, wq_ref,
             dk_ref, dkn_ref, ck_ref, ckn_ref, xk_ref, xkp_ref, wk_ref,
             oq_ref, ok_ref, gq_ref, gk_ref, dbuf, xbuf):
        i = pl.program_id(0)
        last = i == pl.num_programs(0) - 1

        @pl.when(i == 0)
        def _():
            gq_ref[...] = jnp.zeros_like(gq_ref)
            gk_ref[...] = jnp.zeros_like(gk_ref)

        for d_ref, dn_ref, c_ref, cn_ref, x_ref, xp_ref, w_ref, o_ref, g_ref, sc in (
                (dq_ref, dqn_ref, cq_ref, cqn_ref, xq_ref, xqp_ref, wq_ref, oq_ref, gq_ref, 1.0),
                (dk_ref, dkn_ref, ck_ref, ckn_ref, xk_ref, xkp_ref, wk_ref, ok_ref, gk_ref, scale)):
            c, cn = c_ref[...], cn_ref[...]
            dc = d_ref[...] * sc * _dsilu(c, _sg(c))
            dcn = dn_ref[...] * sc * _dsilu(cn, _sg(cn))
            dbuf[0:tm, :] = dc
            dbuf[tm:tm + QK_HALO, :] = jnp.where(last, 0.0, dcn)
            xbuf[0:QK_HALO, :] = jnp.where(i == 0, 0.0, xp_ref[...])
            xbuf[QK_HALO:QK_HALO + tm, :] = x_ref[...]
            acc = jnp.zeros((tm, D), F32)
            for j in range(QK_W):
                acc = acc + w_ref[j:j + 1, :] * dbuf[QK_W - 1 - j:QK_W - 1 - j + tm, :]
                sh = QK_HALO - QK_W + 1 + j
                g_ref[j:j + 1, :] += jnp.sum(dc * xbuf[sh:sh + tm, :], axis=0, keepdims=True)
            o_ref[...] = acc.astype(BF16)

    one = [_row(tm, D), _next(tm, QK_HALO, D, nblk)]
    specs = (one + one + [_row(tm, D, col), _prev(tm, QK_HALO, D, col), _par(QK_HALO, D)]
             + one + one + [_row(tm, D, col + 1), _prev(tm, QK_HALO, D, col + 1), _par(QK_HALO, D)])
    return _call(body, name="qk_bwd", grid=(S // tm,), in_specs=specs,
                 out_specs=[_row(tm, D), _row(tm, D), _par(QK_HALO, D), _par(QK_HALO, D)],
                 out_shape=[_sds((S, D), BF16), _sds((S, D), BF16), _sds((QK_HALO, D), F32), _sds((QK_HALO, D), F32)],
                 scratch_shapes=[pltpu.VMEM((tm + QK_HALO, D), F32), pltpu.VMEM((tm + QK_HALO, D), F32)],
                 compiler_params=_cp(("arbitrary",)))(dq, dq, cq, cq, P, P, wq, dk, dk, ck, ck, P, P, wk)


def _gates_fwd(gif, b_if):
    S = gif.shape[0]
    tm = min(ROW_TILE, S)

    def body(g_ref, b_ref, o_ref):
        z = g_ref[...] + b_ref[...]
        lane = lax.broadcasted_iota(jnp.int32, z.shape, 1)
        ls = jnp.minimum(z, 0.0) - jnp.log(1.0 + jnp.exp(-jnp.abs(z)))
        o_ref[...] = jnp.where(lane < N_HEADS, z, jnp.where(lane < 2 * N_HEADS, ls, 0.0))

    return _call(body, name="gates_fwd", grid=(S // tm,), in_specs=[_row(tm, 128), _par(1, 128)], out_specs=_row(tm, 128),
                 out_shape=_sds((S, 128), F32), compiler_params=_cp(("parallel",)))(gif, b_if)


def _gates_bwd(dgl, gif, b_if):
    S = gif.shape[0]
    tm = min(ROW_TILE, S)

    def body(d_ref, g_ref, b_ref, o_ref, gb_ref):
        z = g_ref[...] + b_ref[...]
        lane = lax.broadcasted_iota(jnp.int32, z.shape, 1)
        d = d_ref[...]
        dz = jnp.where(lane < N_HEADS, d, jnp.where(lane < 2 * N_HEADS, d * _sg(-z), 0.0))
        o_ref[...] = dz.astype(BF16)

        @pl.when(pl.program_id(0) == 0)
        def _():
            gb_ref[...] = jnp.zeros_like(gb_ref)

        gb_ref[...] += jnp.sum(dz, axis=0, keepdims=True)

    return _call(body, name="gates_bwd", grid=(S // tm,), in_specs=[_row(tm, 128), _row(tm, 128), _par(1, 128)],
                 out_specs=[_row(tm, 128), _par(1, 128)], out_shape=[_sds((S, 128), BF16), _sds((1, 128), F32)],
                 compiler_params=_cp(("arbitrary",)))(dgl, gif, b_if)


def _chunk_gates(g8_ref, gt_ref, h, L):
    g8, gt = g8_ref[...], gt_ref[...]
    lane = lax.broadcasted_iota(jnp.int32, g8.shape, 1)
    sub = lax.broadcasted_iota(jnp.int32, gt.shape, 0)
    li_col = jnp.sum(jnp.where(lane == h, g8, 0.0), axis=1, keepdims=True)
    lf_col = jnp.sum(jnp.where(lane == h + N_HEADS, g8, 0.0), axis=1, keepdims=True)
    li_row = jnp.sum(jnp.where(sub == h, gt, 0.0), axis=0, keepdims=True)
    lf_row = jnp.sum(jnp.where(sub == h + N_HEADS, gt, 0.0), axis=0, keepdims=True)
    t = lax.broadcasted_iota(jnp.int32, (L, L), 0)
    s = lax.broadcasted_iota(jnp.int32, (L, L), 1)
    causal = s <= t
    b_col = jnp.sum(jnp.where(causal, lf_row, 0.0), axis=1, keepdims=True)
    b_row = jnp.sum(jnp.where(t <= s, lf_col, 0.0), axis=0, keepdims=True)
    return li_col, li_row, b_col, b_row, causal


def _chunk_fwd(q, k, v, C, n, m, li_row, b_col, b_row, causal):
    d = jnp.where(causal, b_col - b_row + li_row, NEG)
    inter = b_col + m
    m_row = jnp.maximum(inter, jnp.max(d, axis=1, keepdims=True))
    wi = jnp.exp(d - m_row)
    wn = jnp.exp(inter - m_row)
    s = _dot(q, k, NT) * wi
    num = _dot(s, v, NN) + wn * _dot(q, C, NN)
    den = jnp.sum(s, axis=1, keepdims=True) + wn * jnp.sum(q * n, axis=1, keepdims=True)
    e = jnp.exp(-m_row)
    inv = 1.0 / jnp.maximum(jnp.abs(den), e)
    return wi, wn, s, num * inv, den, e, inv


def _chunk_state(li_col, li_row, b_col, b_row, m, L):
    blast = b_col[L - 1:L, :]
    g_col = blast - b_col + li_col
    g_row = blast - b_row + li_row
    m_new = jnp.maximum(blast + m, jnp.max(g_row, axis=1, keepdims=True))
    decay = jnp.exp(blast + m - m_new)
    wk_col = jnp.exp(g_col - m_new)
    return m_new, decay, wk_col


def _mlstm_fwd(q, k, P, g8, gt, D, vcol):
    S = q.shape[0]
    H, dh = N_HEADS, D // N_HEADS
    L = min(ML_CHUNK, S)
    nc = S // L

    def body(q_ref, k_ref, v_ref, g8_ref, gt_ref, h_ref, cs_ref, ns_ref, ms_ref, C, n, m):
        h, j = pl.program_id(0), pl.program_id(1)

        @pl.when(j == 0)
        def _():
            C[...] = jnp.zeros_like(C)
            n[...] = jnp.zeros_like(n)
            m[...] = jnp.zeros_like(m)

        qv, kv, vv = q_ref[...], k_ref[...], v_ref[...]
        Cv, nv, mv = C[...], n[...], m[:, 0:1]
        cs_ref[...] = Cv.astype(BF16)
        ns_ref[...] = nv
        ms_ref[...] = m[...]
        li_col, li_row, b_col, b_row, causal = _chunk_gates(g8_ref, gt_ref, h, L)
        _, _, _, hv, _, _, _ = _chunk_fwd(qv, kv, vv, Cv, nv, mv, li_row, b_col, b_row, causal)
        h_ref[...] = hv
        m_new, decay, wk_col = _chunk_state(li_col, li_row, b_col, b_row, mv, L)
        kw = kv * wk_col
        C[...] = decay * Cv + _dot(kw, vv, TN)
        n[...] = decay * nv + jnp.sum(kw, axis=0, keepdims=True)
        m[...] = jnp.broadcast_to(m_new, m.shape)

    blk = lambda c0: pl.BlockSpec((L, dh), lambda h, j: (j, c0 + h))
    return _call(body, name="mlstm_fwd", grid=(H, nc),
                 in_specs=[blk(0), blk(0), blk(vcol * H), pl.BlockSpec((L, 8), lambda h, j: (j, 0)),
                           pl.BlockSpec((8, L), lambda h, j: (0, j))],
                 out_specs=[blk(0), pl.BlockSpec((None, None, dh, dh), lambda h, j: (h, j, 0, 0)),
                            pl.BlockSpec((None, None, 1, dh), lambda h, j: (h, j, 0, 0)),
                            pl.BlockSpec((None, None, 1, 128), lambda h, j: (h, j, 0, 0))],
                 out_shape=[_sds((S, D), F32), _sds((H, nc, dh, dh), BF16), _sds((H, nc, 1, dh), F32),
                            _sds((H, nc, 1, 128), F32)],
                 scratch_shapes=[pltpu.VMEM((dh, dh), F32), pltpu.VMEM((1, dh), F32), pltpu.VMEM((1, 128), F32)],
                 compiler_params=_cp(("arbitrary", "arbitrary")))(q, k, P, g8, gt)


def _mlstm_bwd(dhm, q, k, P, g8, gt, cs, ns, ms, D, vcol):
    S = q.shape[0]
    H, dh = N_HEADS, D // N_HEADS
    L = min(ML_CHUNK, S)
    nc = S // L

    def body(dh_ref, q_ref, k_ref, v_ref, g8_ref, gt_ref, cs_ref, ns_ref, ms_ref,
             dq_ref, dk_ref, dv_ref, dg_ref, dC, dnv):
        h, j = pl.program_id(0), pl.program_id(1)

        @pl.when(j == 0)
        def _():
            dC[...] = jnp.zeros_like(dC)
            dnv[...] = jnp.zeros_like(dnv)

        qv, kv, vv, dhv = q_ref[...], k_ref[...], v_ref[...], dh_ref[...]
        Cv, nv, mv = cs_ref[...], ns_ref[...], ms_ref[:, 0:1]
        li_col, li_row, b_col, b_row, causal = _chunk_gates(g8_ref, gt_ref, h, L)
        wi, wn, s, hv, den, e, inv = _chunk_fwd(qv, kv, vv, Cv, nv, mv, li_row, b_col, b_row, causal)
        dn = dhv * inv
        dd = -jnp.sum(dhv * hv, axis=1, keepdims=True) * inv * jnp.where(jnp.abs(den) > e, jnp.sign(den), 0.0)
        dS = _dot(dn, vv, NT) + dd
        dqk = dS * wi
        dq_i = wn * (_dot(dn, Cv, NT) + dd * nv)
        dq = _dot(dqk, kv, NN) + dq_i
        dv = _dot(s, dn, TN)
        m_new, decay, wk_col = _chunk_state(li_col, li_row, b_col, b_row, mv, L)
        dCv, dnvv = dC[...], dnv[...]
        dk_i = wk_col * (_dot(vv, dCv, NT) + dnvv)
        dk = _dot(dqk, qv, TN) + dk_i
        dv = dv + _dot(kv * wk_col, dCv, NN)
        qw = qv * wn
        dC[...] = decay * dCv + _dot(qw, dn, TN)
        dnv[...] = decay * dnvv + jnp.sum(qw * dd, axis=0, keepdims=True)
        dq_ref[...] = dq
        dk_ref[...] = dk
        dv_ref[...] = dv.astype(BF16)
        G = dS * s
        t = lax.broadcasted_iota(jnp.int32, (L, L), 0)
        r = lax.broadcasted_iota(jnp.int32, (L, L), 1)
        eye = t == r
        col_row = jnp.sum(G, axis=0, keepdims=True)
        col_g = jnp.sum(jnp.where(eye, col_row, 0.0), axis=1, keepdims=True)
        row_i = jnp.sum(qv * dq_i, axis=1, keepdims=True)
        col_i = jnp.sum(kv * dk_i, axis=1, keepdims=True)
        dF = jnp.sum(G, axis=1, keepdims=True) - col_g + row_i
        across = decay * (jnp.sum(jnp.sum(dCv * Cv.astype(F32), axis=0, keepdims=True), axis=1, keepdims=True)
                          + jnp.sum(dnvv * nv, axis=1, keepdims=True))
        dlf = jnp.sum(jnp.where(t >= r, dF, 0.0) + jnp.where(t < r, col_i, 0.0), axis=0, keepdims=True) + across
        dli = col_row + jnp.sum(jnp.where(eye, col_i, 0.0), axis=0, keepdims=True)
        sub = lax.broadcasted_iota(jnp.int32, (8, L), 0)
        dg_ref[...] = jnp.where(sub == 0, dli, jnp.where(sub == 1, dlf, 0.0))

    blk = lambda c0: pl.BlockSpec((L, dh), lambda h, j: (nc - 1 - j, c0 + h))
    st = lambda r, c: pl.BlockSpec((None, None, r, c), lambda h, j: (h, nc - 1 - j, 0, 0))
    return _call(body, name="mlstm_bwd", grid=(H, nc),
                 in_specs=[blk(0), blk(0), blk(0), blk(vcol * H), pl.BlockSpec((L, 8), lambda h, j: (nc - 1 - j, 0)),
                           pl.BlockSpec((8, L), lambda h, j: (0, nc - 1 - j)), st(dh, dh), st(1, dh), st(1, 128)],
                 out_specs=[blk(0), blk(0), blk(0), pl.BlockSpec((None, 8, L), lambda h, j: (h, 0, nc - 1 - j))],
                 out_shape=[_sds((S, D), F32), _sds((S, D), F32), _sds((S, D), BF16), _sds((H, 8, S), F32)],
                 scratch_shapes=[pltpu.VMEM((dh, dh), F32), pltpu.VMEM((1, dh), F32)],
                 compiler_params=_cp(("arbitrary", "arbitrary")))(dhm, q, k, P, g8, gt, cs, ns, ms)


def _ml_post_fwd(hm, P, g, D, ocol, zcol):
    S = hm.shape[0]
    tm = min(ROW_TILE, S)
    dh = D // N_HEADS

    def body(h_ref, o_ref, z_ref, g_ref, cb_ref):
        for hd in range(N_HEADS):
            sl = slice(hd * dh, (hd + 1) * dh)
            x = _sg(o_ref[:, sl]) * h_ref[:, sl]
            r = lax.rsqrt(_mean(x * x) + EPS)
            z = z_ref[:, sl]
            cb_ref[:, sl] = (x * r * g_ref[:, sl] * (z * _sg(z))).astype(BF16)

    return _call(body, name="ml_post_fwd", grid=(S // tm,),
                 in_specs=[_row(tm, D), _row(tm, D, ocol), _row(tm, D, zcol), _par(1, D)], out_specs=_row(tm, D),
                 out_shape=_sds((S, D), BF16), compiler_params=_cp(("parallel",)))(hm, P, P, g)


def _ml_post_bwd(d_cb, hm, P, g, D, ocol, zcol):
    S = hm.shape[0]
    tm = min(ROW_TILE, S)
    dh = D // N_HEADS

    def body(d_ref, h_ref, o_ref, z_ref, g_ref, dh_ref, do_ref, dz_ref, dg_ref):
        @pl.when(pl.program_id(0) == 0)
        def _():
            dg_ref[...] = jnp.zeros_like(dg_ref)

        for hd in range(N_HEADS):
            sl = slice(hd * dh, (hd + 1) * dh)
            hv, z, gv, d = h_ref[:, sl], z_ref[:, sl], g_ref[:, sl], d_ref[:, sl]
            so, sz = _sg(o_ref[:, sl]), _sg(z)
            x = so * hv
            r = lax.rsqrt(_mean(x * x) + EPS)
            xh = x * r
            d3 = d * (z * sz)
            dz_ref[:, sl] = (d * xh * gv * _dsilu(z, sz)).astype(BF16)
            dg_ref[:, sl] += jnp.sum(d3 * xh, axis=0, keepdims=True)
            dxh = d3 * gv
            d2 = r * (dxh - xh * _mean(dxh * xh))
            do_ref[:, sl] = (d2 * hv * so * (1.0 - so)).astype(BF16)
            dh_ref[:, sl] = d2 * so

    return _call(body, name="ml_post_bwd", grid=(S // tm,),
                 in_specs=[_row(tm, D), _row(tm, D), _row(tm, D, ocol), _row(tm, D, zcol), _par(1, D)],
                 out_specs=[_row(tm, D), _row(tm, D), _row(tm, D), _par(1, D)],
                 out_shape=[_sds((S, D), F32), _sds((S, D), BF16), _sds((S, D), BF16), _sds((1, D), F32)],
                 compiler_params=_cp(("arbitrary",)))(d_cb, hm, P, P, g)


def _attn_scores(q, kh, scale):
    sc = _dot(q, kh, NT) * scale
    ex = jnp.exp(sc - jnp.max(sc, axis=1, keepdims=True))
    return ex / jnp.sum(ex, axis=1, keepdims=True)


def _attn_fwd(P, kv, D, qcol, zcol):
    S, M = P.shape[0], kv.shape[0]
    tm = min(ROW_TILE, S)
    dh = D // N_HEADS
    scale = float(dh) ** -0.5

    def body(q_ref, z_ref, kv_ref, cc_ref):
        for hd in range(N_HEADS):
            sl = slice(hd * dh, (hd + 1) * dh)
            p = _attn_scores(q_ref[:, sl], kv_ref[:, sl], scale)
            o = _dot(p, kv_ref[:, D + hd * dh:D + (hd + 1) * dh], NN)
            z = z_ref[:, sl]
            cc_ref[:, sl] = (o * (z * _sg(z))).astype(BF16)

    return _call(body, name="attn_fwd", grid=(S // tm,),
                 in_specs=[_row(tm, D, qcol), _row(tm, D, zcol), _par(M, 2 * D)], out_specs=_row(tm, D),
                 out_shape=_sds((S, D), BF16), compiler_params=_cp(("parallel",)))(P, P, kv)


def _attn_bwd(d_cc, P, kv, D, qcol, zcol):
    S, M = P.shape[0], kv.shape[0]
    tm = min(ROW_TILE, S)
    dh = D // N_HEADS
    scale = float(dh) ** -0.5

    def body(d_ref, q_ref, z_ref, kv_ref, dq_ref, dz_ref, dkv_ref):
        @pl.when(pl.program_id(0) == 0)
        def _():
            dkv_ref[...] = jnp.zeros_like(dkv_ref)

        for hd in range(N_HEADS):
            sl = slice(hd * dh, (hd + 1) * dh)
            vl = slice(D + hd * dh, D + (hd + 1) * dh)
            q, kh, vh = q_ref[:, sl], kv_ref[:, sl], kv_ref[:, vl]
            p = _attn_scores(q, kh, scale)
            o = _dot(p, vh, NN)
            z, d = z_ref[:, sl], d_ref[:, sl]
            sz = _sg(z)
            do = d * (z * sz)
            dz_ref[:, sl] = (d * o * _dsilu(z, sz)).astype(BF16)
            dp = _dot(do, vh, NT)
            ds = p * (dp - jnp.sum(p * dp, axis=1, keepdims=True)) * scale
            dq_ref[:, sl] = _dot(ds, kh, NN).astype(BF16)
            dkv_ref[:, sl] += _dot(ds, q, TN)
            dkv_ref[:, vl] += _dot(p, do, TN)

    return _call(body, name="attn_bwd", grid=(S // tm,),
                 in_specs=[_row(tm, D), _row(tm, D, qcol), _row(tm, D, zcol), _par(M, 2 * D)],
                 out_specs=[_row(tm, D), _row(tm, D), _par(M, 2 * D)],
                 out_shape=[_sds((S, D), BF16), _sds((S, D), BF16), _sds((M, 2 * D), F32)],
                 compiler_params=_cp(("arbitrary",)))(d_cc, P, P, kv)


def _merge_fwd(yc, ym, yx, P, D, gcol):
    S = yc.shape[0]
    tm = min(ROW_TILE, S)

    def body(c_ref, m_ref, x_ref, gc_ref, gm_ref, gx_ref, o_ref):
        o_ref[...] = (_sg(gc_ref[...]) * c_ref[...] + _sg(gm_ref[...]) * m_ref[...]
                      + _sg(gx_ref[...]) * x_ref[...]).astype(BF16)

    return _call(body, name="merge_fwd", grid=(S // tm,),
                 in_specs=[_row(tm, D)] * 3 + [_row(tm, D, gcol), _row(tm, D, gcol + 1), _row(tm, D, gcol + 2)],
                 out_specs=_row(tm, D), out_shape=_sds((S, D), BF16), compiler_params=_cp(("parallel",)))(yc, ym, yx, P, P, P)


def _merge_bwd(dm, yc, ym, yx, P, D, gcol):
    S = yc.shape[0]
    tm = min(ROW_TILE, S)

    def body(d_ref, c_ref, m_ref, x_ref, gc_ref, gm_ref, gx_ref, dc_ref, dmm_ref, dx_ref, dgc_ref, dgm_ref, dgx_ref):
        d = d_ref[...]
        for y_ref, g_ref, dy_ref, dg_ref in ((c_ref, gc_ref, dc_ref, dgc_ref), (m_ref, gm_ref, dmm_ref, dgm_ref),
                                             (x_ref, gx_ref, dx_ref, dgx_ref)):
            s = _sg(g_ref[...])
            dy_ref[...] = (d * s).astype(BF16)
            dg_ref[...] = (d * y_ref[...] * s * (1.0 - s)).astype(BF16)

    return _call(body, name="merge_bwd", grid=(S // tm,),
                 in_specs=[_row(tm, D)] * 4 + [_row(tm, D, gcol), _row(tm, D, gcol + 1), _row(tm, D, gcol + 2)],
                 out_specs=[_row(tm, D)] * 6, out_shape=[_sds((S, D), BF16)] * 6,
                 compiler_params=_cp(("parallel",)))(dm, yc, ym, yx, P, P, P)


def _post(o, x, tgt, g):
    S, D = x.shape
    tm = min(ROW_TILE, S)

    def body(o_ref, x_ref, t_ref, g_ref, dy_ref, do_ref, dg_ref, l_ref):
        ov, gv = o_ref[...], g_ref[...]
        r = lax.rsqrt(_mean(ov * ov) + EPS)
        e = x_ref[...] + ov * r * gv - t_ref[...]
        dy = e / D
        dy_ref[...] = dy
        dxh = dy * gv
        do_ref[...] = (r * dxh - ov * (r * r * r) * _mean(dxh * ov)).astype(BF16)

        @pl.when(pl.program_id(0) == 0)
        def _():
            dg_ref[...] = jnp.zeros_like(dg_ref)
            l_ref[...] = jnp.zeros_like(l_ref)

        dg_ref[...] += jnp.sum(dy * ov * r, axis=0, keepdims=True)
        l_ref[...] += jnp.sum(e * e, axis=0, keepdims=True)

    return _call(body, name="post", grid=(S // tm,), in_specs=[_row(tm, D)] * 3 + [_par(1, D)],
                 out_specs=[_row(tm, D), _row(tm, D), _par(1, D), _par(1, D)],
                 out_shape=[_sds((S, D), F32), _sds((S, D), BF16), _sds((1, D), F32), _sds((1, D), F32)],
                 compiler_params=_cp(("arbitrary",)))(o, x, tgt, g)


def _chip_peers():
    x, y, c = lax.axis_index("x"), lax.axis_index("y"), lax.axis_index("c")
    return x, y, c, [(1 - x, y), (x, 1 - y), (1 - x, 1 - y)]


def _gather_chips(halved, whole):
    TH, TW = len(halved), len(whole)
    T = TH + TW

    def body(*refs):
        ins, outs = refs[:T], refs[T:2 * T]
        send, recv, fsend, frecv = refs[2 * T:]
        x, y, c, chips = _chip_peers()
        mine = 2 * x + y

        def over_ici(t, j, chip, slot):
            src = ins[t].at[c] if t < TH else ins[t]
            dst = outs[t].at[slot, c] if t < TH else outs[t].at[slot]
            return pltpu.make_async_remote_copy(src_ref=src, dst_ref=dst, send_sem=send.at[t, j], recv_sem=recv.at[t, j],
                                                device_id=(*chip, c), device_id_type=MESH)

        def to_sibling(t, j, slot, half):
            place = outs[t].at[slot, half]
            return pltpu.make_async_remote_copy(src_ref=place, dst_ref=place, send_sem=fsend.at[t, j], recv_sem=frecv.at[t, j],
                                                device_id=(x, y, 1 - c), device_id_type=MESH)

        sends = []
        for t in range(T):
            for j, chip in enumerate(chips):
                cp = over_ici(t, j, chip, mine)
                cp.start()
                sends.append(cp)
        for t in range(T):
            for j, (px, py) in enumerate(chips):
                over_ici(t, j, (px, py), 2 * px + py).wait_recv()
                if t < TH:
                    cp = to_sibling(t, j, 2 * px + py, c)
                    cp.start()
                    sends.append(cp)
        for t in range(TH):
            for j, (px, py) in enumerate(chips):
                to_sibling(t, j, 2 * px + py, 1 - c).wait_recv()
        for cp in sends:
            cp.wait_send()

    anyspec = pl.BlockSpec(memory_space=pl.ANY)
    dma = pltpu.SemaphoreType.DMA
    return _call(body, name="gather_weights", in_specs=[anyspec] * T, out_specs=[anyspec] * T,
                 out_shape=[_sds((4, *s.shape), s.dtype) for s in (*halved, *whole)],
                 scratch_shapes=[dma((T, 3)), dma((T, 3)), dma((TH, 3)), dma((TH, 3))],
                 compiler_params=pltpu.CompilerParams(has_side_effects=True))(*halved, *whole)


def _to_sibling(tensors, name):
    T = len(tensors)

    def body(*refs):
        ins, outs = refs[:T], refs[T:2 * T]
        send, recv = refs[2 * T:]
        x, y, c = lax.axis_index("x"), lax.axis_index("y"), lax.axis_index("c")
        cps = [pltpu.make_async_remote_copy(src_ref=ins[t], dst_ref=outs[t], send_sem=send.at[t], recv_sem=recv.at[t],
                                            device_id=(x, y, 1 - c), device_id_type=MESH) for t in range(T)]
        for cp in cps:
            cp.start()
        for cp in cps:
            cp.wait_recv()
        for cp in cps:
            cp.wait_send()

    anyspec = pl.BlockSpec(memory_space=pl.ANY)
    return _call(body, name=name, in_specs=[anyspec] * T, out_specs=[anyspec] * T,
                 out_shape=[_sds(a.shape, a.dtype) for a in tensors],
                 scratch_shapes=[pltpu.SemaphoreType.DMA((T,)), pltpu.SemaphoreType.DMA((T,))],
                 compiler_params=pltpu.CompilerParams(has_side_effects=True))(*tensors)


def _chip_scatter(tensors):
    T = len(tensors)

    def body(*refs):
        ins, outs = refs[:T], refs[T:2 * T]
        send, recv = refs[2 * T:]
        x, y, c, chips = _chip_peers()
        mine = 2 * x + y

        def over_ici(t, j, chip, src_slot, dst_slot):
            return pltpu.make_async_remote_copy(src_ref=ins[t].at[src_slot], dst_ref=outs[t].at[dst_slot], send_sem=send.at[t, j],
                                                recv_sem=recv.at[t, j], device_id=(*chip, c), device_id_type=MESH)

        sends = []
        for t in range(T):
            for j, (px, py) in enumerate(chips):
                cp = over_ici(t, j, (px, py), 2 * px + py, mine)
                cp.start()
                sends.append(cp)
        for t in range(T):
            for j, (px, py) in enumerate(chips):
                over_ici(t, j, (px, py), mine, 2 * px + py).wait_recv()
        for cp in sends:
            cp.wait_send()

    anyspec = pl.BlockSpec(memory_space=pl.ANY)
    dma = pltpu.SemaphoreType.DMA
    return _call(body, name="chip_scatter", in_specs=[anyspec] * T, out_specs=[anyspec] * T,
                 out_shape=[_sds(a.shape, a.dtype) for a in tensors], scratch_shapes=[dma((T, 3)), dma((T, 3))],
                 compiler_params=pltpu.CompilerParams(has_side_effects=True))(*tensors)


def _row_tile(R, C, slots):
    tr = max(16, ((8 << 20) // (C * 4 * slots)) // 16 * 16)
    if tr >= R:
        return R
    while R % tr:
        tr -= 16
    return tr


def _add_pair(a, b, name):
    lead, R, C = a.shape
    tr = _row_tile(R, C, 3)

    def body(a_ref, b_ref, o_ref):
        o_ref[...] = (a_ref[...].astype(F32) + b_ref[...].astype(F32)).astype(BF16)

    blk = pl.BlockSpec((None, tr, C), lambda k, i: (k, i, 0))
    return _call(body, name=name, grid=(lead, R // tr), in_specs=[blk, blk], out_specs=blk, out_shape=_sds(a.shape, BF16),
                 compiler_params=_cp(("parallel", "parallel")))(a, b)


def _sum_chips(land, name):
    _, R, C = land.shape
    tr = _row_tile(R, C, 5)

    def body(l_ref, o_ref):
        o_ref[...] = ((l_ref[0].astype(F32) + l_ref[1].astype(F32)) + l_ref[2].astype(F32)) + l_ref[3].astype(F32)

    return _call(body, name=name, grid=(R // tr,), in_specs=[pl.BlockSpec((4, tr, C), lambda i: (0, i, 0))],
                 out_specs=pl.BlockSpec((tr, C), lambda i: (i, 0)), out_shape=_sds((R, C), F32),
                 compiler_params=_cp(("parallel",)))(land)


def _scatter_grads(blocks, small):
    T = len(blocks)

    def body(*refs):
        ins, sm = refs[:T], refs[T]
        outs, smo = refs[T + 1:2 * T + 1], refs[2 * T + 1]
        send, recv, loc = refs[2 * T + 2:]
        x, y, c = lax.axis_index("x"), lax.axis_index("y"), lax.axis_index("c")
        me = 4 * x + 2 * y + c
        peers = [(x ^ ((k >> 2) & 1), y ^ ((k >> 1) & 1), c ^ (k & 1)) for k in range(1, N_DEV)]
        copies = []
        for t in range(T):
            lc = pltpu.make_async_copy(ins[t].at[2 * x + y], outs[t].at[me], loc.at[t])
            lc.start()
            copies.append(lc)
        lc = pltpu.make_async_copy(sm, smo.at[me], loc.at[T])
        lc.start()
        copies.append(lc)

        def remote(t, j, peer, sender):
            px, py, pc = peer
            src = sm if t == T else ins[t].at[2 * px + py]
            dst = (smo if t == T else outs[t]).at[sender]
            return pltpu.make_async_remote_copy(src_ref=src, dst_ref=dst, send_sem=send.at[t, j], recv_sem=recv.at[t, j],
                                                device_id=peer, device_id_type=MESH)

        sends = []
        for t in range(T + 1):
            for j, peer in enumerate(peers):
                cp = remote(t, j, peer, me)
                cp.start()
                sends.append(cp)
        for t in range(T + 1):
            for j, (px, py, pc) in enumerate(peers):
                remote(t, j, (x, y, c), 4 * px + 2 * py + pc).wait_recv()
        for cp in sends:
            cp.wait_send()
        for lc in copies:
            lc.wait()

    anyspec = pl.BlockSpec(memory_space=pl.ANY)
    return _call(body, name="scatter_grads", in_specs=[anyspec] * (T + 1), out_specs=[anyspec] * (T + 1),
                 out_shape=[_sds((N_DEV, *b.shape[1:]), b.dtype) for b in blocks] + [_sds((N_DEV, *small.shape), small.dtype)],
                 scratch_shapes=[pltpu.SemaphoreType.DMA((T + 1, N_DEV - 1)), pltpu.SemaphoreType.DMA((T + 1, N_DEV - 1)),
                                 pltpu.SemaphoreType.DMA((T + 1,))],
                 compiler_params=pltpu.CompilerParams(has_side_effects=True))(*blocks, small)


def _adam_update(g, w_ref, m_ref, v_ref, g_ref, d_ref, mo_ref, vo_ref):
    c1, c2 = 1.0 - ADAM_B1 ** ADAM_STEP, 1.0 - ADAM_B2 ** ADAM_STEP
    mn = ADAM_B1 * m_ref[...] + (1.0 - ADAM_B1) * g
    vn = ADAM_B2 * v_ref[...] + (1.0 - ADAM_B2) * (g * g)
    g_ref[...] = g
    mo_ref[...] = mn
    vo_ref[...] = vn
    d_ref[...] = -ADAM_LR * ((mn / c1) / (jnp.sqrt(vn / c2) + ADAM_EPS) + ADAM_WD * w_ref[...])


def _adam_rows(R, C):
    tr = R if R * C * 4 <= (1 << 20) else max(16, ((1 << 20) // (C * 4)) // 16 * 16)
    while R % tr:
        tr -= 16
    return tr


def _adamw(land, w, m, v, name):
    R, C = w.shape
    tr = _adam_rows(R, C)

    def body(l_ref, w_ref, m_ref, v_ref, *outs):
        g = l_ref[0].astype(F32)
        for k in range(1, N_DEV):
            g = g + l_ref[k].astype(F32)
        _adam_update(g, w_ref, m_ref, v_ref, *outs)

    blk = pl.BlockSpec((tr, C), lambda i: (i, 0))
    return _call(body, name=name, grid=(R // tr,), in_specs=[pl.BlockSpec((N_DEV, tr, C), lambda i: (0, i, 0)), blk, blk, blk],
                 out_specs=[blk] * 4, out_shape=[_sds((R, C), F32)] * 4, compiler_params=_cp(("parallel",)))(land, w, m, v)


def _adamw_halves(mine, theirs, w, m, v, name):
    r, C = mine.shape
    tr = _adam_rows(r, C)
    nt = r // tr

    def body(a_ref, b_ref, w_ref, m_ref, v_ref, *outs):
        g = jnp.where(pl.program_id(0) == lax.axis_index("c"), a_ref[...], b_ref[...])
        _adam_update(g, w_ref, m_ref, v_ref, *outs)

    half = pl.BlockSpec((tr, C), lambda hh, i: (i, 0))
    blk = pl.BlockSpec((tr, C), lambda hh, i: (hh * nt + i, 0))
    return _call(body, name=name, grid=(2, nt), in_specs=[half, half, blk, blk, blk], out_specs=[blk] * 4,
                 out_shape=[_sds((2 * r, C), F32)] * 4, compiler_params=_cp(("parallel", "parallel")))(mine, theirs, w, m, v)


def _pad_rows(a, rows):
    return jnp.pad(a, ((0, rows - a.shape[0]), (0, 0)))


def _local_step(x, mem, tgt, g_pre, w_main, w_if, b_if, wq, wk, w_dw, b_dw, g_ln, b_ln, w_conv_out, g_ml_head, w_ml_out,
                g_mem, w_mem_kv, w_xa_out, w_out, g_post):
    S, D = x.shape
    row = lambda a: a.reshape(1, -1)
    bif = jnp.pad(b_if, (0, 128 - b_if.shape[0])).reshape(1, 128)
    wdw, wq8, wk8 = _pad_rows(w_dw, HALO), _pad_rows(wq, QK_HALO), _pad_rows(wk, QK_HALO)

    h = _rms_fwd(x, row(g_pre), "rms_pre")
    P = _mm(h, w_main, "nn", F32, "proj_in")
    gif = _mm(h, w_if, "nn", F32, "proj_if")
    ca, u1 = _conv_fwd(P, wdw, row(b_dw), row(g_ln), row(b_ln), D)
    q, k, cq, ck = _qk_fwd(P, wq8, wk8, D, 3)
    gl = _gates_fwd(gif, bif)
    g8 = gl[:, :8]
    gt = g8.T
    hm, cs, ns, ms = _mlstm_fwd(q, k, P, g8, gt, D, 5)
    cb = _ml_post_fwd(hm, P, row(g_ml_head), D, 6, 7)
    mn = _rms_fwd(mem, row(g_mem), "rms_mem")
    kv = _mm(mn, w_mem_kv, "nn", BF16, "proj_kv")
    cc = _attn_fwd(P, kv, D, 8, 9)
    yc = _mm(ca, w_conv_out, "nn", F32, "out_conv")
    ym = _mm(cb, w_ml_out, "nn", F32, "out_ml")
    yx = _mm(cc, w_xa_out, "nn", F32, "out_xa")
    merged = _merge_fwd(yc, ym, yx, P, D, 10)
    o = _mm(merged, w_out, "nn", F32, "out_proj")
    dy, d_o, gg_post, sq = _post(o, x, tgt, row(g_post))

    d_merged = _mm(d_o, w_out, "nt", F32, "d_merged")
    gw_out = _mm(merged, d_o, "tn", BF16, "gw_out")
    d_yc, d_ym, d_yx, d_gc, d_gm, d_gx = _merge_bwd(d_merged, yc, ym, yx, P, D, 10)
    d_ca = _mm(d_yc, w_conv_out, "nt", F32, "d_ca")
    gw_conv_out = _mm(ca, d_yc, "tn", BF16, "gw_conv_out")
    d_cb = _mm(d_ym, w_ml_out, "nt", F32, "d_cb")
    gw_ml_out = _mm(cb, d_ym, "tn", BF16, "gw_ml_out")
    d_cc = _mm(d_yx, w_xa_out, "nt", F32, "d_cc")
    gw_xa_out = _mm(cc, d_yx, "tn", BF16, "gw_xa_out")

    d_u1, d_zc, gg_ln, gb_ln = _conv_bwd_local(d_ca, u1, P, row(g_ln), row(b_ln), D)
    d_a, d_b, gw_dw, gb_dw = _conv_bwd_taps(d_u1, P, wdw, D)

    d_qx, d_zx, d_kv = _attn_bwd(d_cc, P, kv, D, 8, 9)
    d_kvb = d_kv.astype(BF16)
    gw_mem_kv = _mm(mn, d_kvb, "tn", BF16, "gw_mem_kv")
    d_mn = _mm(d_kvb, w_mem_kv, "nt", F32, "d_mn")
    _, gg_mem = _rms_bwd(mem, row(g_mem), [d_mn], None, "rms_mem_bwd")

    d_hm, d_om, d_zm, gg_ml = _ml_post_bwd(d_cb, hm, P, row(g_ml_head), D, 6, 7)
    dq, dk, d_v, dg = _mlstm_bwd(d_hm, q, k, P, g8, gt, cs, ns, ms, D, 5)
    d_qp, d_kp, gwq, gwk = _qk_bwd(dq, dk, cq, ck, P, wq8, wk8, D, 3)
    dgl = jnp.pad(jnp.concatenate([dg[:, 0, :].T, dg[:, 1, :].T], axis=1), ((0, 0), (0, 128 - 2 * N_HEADS)))
    d_gif, gb_if = _gates_bwd(dgl, gif, bif)

    dP = jnp.concatenate([d_a, d_b, d_zc, d_qp, d_kp, d_v, d_om, d_zm, d_qx, d_zx, d_gc, d_gm, d_gx], axis=1)
    d_h1 = _mm(dP, w_main, "nt", F32, "d_h_main")
    d_h2 = _mm(d_gif, w_if, "nt", F32, "d_h_if")
    gw_main = _mm(h, dP, "tn", BF16, "gw_main")
    gw_if = _mm(h, d_gif, "tn", BF16, "gw_if")
    grad_x, gg_pre = _rms_bwd(x, row(g_pre), [d_h1, d_h2], dy, "rms_pre_bwd")

    grads = dict(g_pre=gg_pre[0], w_main=gw_main, w_if=gw_if, b_if=gb_if[0, :2 * N_HEADS], wq=gwq[:QK_W], wk=gwk[:QK_W],
                 w_dw=gw_dw[:CONV_W], b_dw=gb_dw[0], g_ln=gg_ln[0], b_ln=gb_ln[0], w_conv_out=gw_conv_out,
                 g_ml_head=gg_ml[0], w_ml_out=gw_ml_out, g_mem=gg_mem[0], w_mem_kv=gw_mem_kv, w_xa_out=gw_xa_out,
                 w_out=gw_out, g_post=gg_post[0])
    return jnp.sum(sq), grad_x, grads


SMALL = ("g_pre", "b_if", "b_dw", "g_ln", "b_ln", "g_ml_head", "g_mem", "g_post")


def kernel(x, mem, g_pre, w_in, b_if, w_qk_conv, w_dw, b_dw, g_ln, b_ln, w_conv_out, g_ml_head, w_ml_out, g_mem, w_mem_kv, w_xa_out, w_out, g_post, loss_target, m_g_pre, m_w_in, m_b_if, m_w_qk_conv, m_w_dw, m_b_dw, m_g_ln, m_b_ln, m_w_conv_out, m_g_ml_head, m_w_ml_out, m_g_mem, m_w_mem_kv, m_w_xa_out, m_w_out, m_g_post, v_g_pre, v_w_in, v_b_if, v_w_qk_conv, v_w_dw, v_b_dw, v_g_ln, v_b_ln, v_w_conv_out, v_g_ml_head, v_w_ml_out, v_g_mem, v_w_mem_kv, v_w_xa_out, v_w_out, v_g_post):
    W = dict(g_pre=g_pre, w_in=w_in, b_if=b_if, w_qk_conv=w_qk_conv, w_dw=w_dw, b_dw=b_dw, g_ln=g_ln, b_ln=b_ln,
             w_conv_out=w_conv_out, g_ml_head=g_ml_head, w_ml_out=w_ml_out, g_mem=g_mem, w_mem_kv=w_mem_kv,
             w_xa_out=w_xa_out, w_out=w_out, g_post=g_post)
    Mo = dict(g_pre=m_g_pre, w_in=m_w_in, b_if=m_b_if, w_qk_conv=m_w_qk_conv, w_dw=m_w_dw, b_dw=m_b_dw, g_ln=m_g_ln,
              b_ln=m_b_ln, w_conv_out=m_w_conv_out, g_ml_head=m_g_ml_head, w_ml_out=m_w_ml_out, g_mem=m_g_mem,
              w_mem_kv=m_w_mem_kv, w_xa_out=m_w_xa_out, w_out=m_w_out, g_post=m_g_post)
    Vo = dict(g_pre=v_g_pre, w_in=v_w_in, b_if=v_b_if, w_qk_conv=v_w_qk_conv, w_dw=v_w_dw, b_dw=v_b_dw, g_ln=v_g_ln,
              b_ln=v_b_ln, w_conv_out=v_w_conv_out, g_ml_head=v_g_ml_head, w_ml_out=v_w_ml_out, g_mem=v_g_mem,
              w_mem_kv=v_w_mem_kv, w_xa_out=v_w_xa_out, w_out=v_w_out, g_post=v_g_post)
    D = x.shape[-1]
    n_in = 4 * w_in.shape[1]
    off_if = 8 * D

    big = ("w_in", "w_conv_out", "w_ml_out", "w_mem_kv", "w_xa_out", "w_out")
    halves = lambda a: a.reshape(2, a.shape[0] // 2, a.shape[1])
    own = [halves(W[n].astype(BF16)) for n in big] + [w_qk_conv, w_dw]
    my_chip = 2 * lax.axis_index("x") + lax.axis_index("y")
    got = [lax.dynamic_update_index_in_dim(g, o, my_chip, 0) for g, o in zip(_gather_chips(own[:len(big)], own[len(big):]), own)]
    got = [g.reshape(4, -1, g.shape[-1]) for g in got]
    cols = lambda a: jnp.transpose(a, (1, 0, 2)).reshape(a.shape[1], -1)
    rows = lambda a: a.reshape(-1, a.shape[2])
    w_in_f = cols(got[0])
    w_main = jnp.concatenate([w_in_f[:, :off_if], w_in_f[:, off_if + 2 * N_HEADS:]], axis=1)
    w_if = jnp.pad(w_in_f[:, off_if:off_if + 2 * N_HEADS], ((0, 0), (0, 128 - 2 * N_HEADS)))
    wqk_f = cols(got[6])

    sq, grad_x, G = _local_step(
        x[0], mem[0], loss_target[0], g_pre, w_main, w_if, b_if, wqk_f[:, :D], wqk_f[:, D:], cols(got[7]), b_dw, g_ln, b_ln,
        rows(got[1]), g_ml_head, rows(got[2]), g_mem, cols(got[3]), rows(got[4]), rows(got[5]), g_post)
    loss = lax.psum(0.5 * sq / D, ("x", "y", "c"))

    colblk = lambda a: jnp.transpose(a.reshape(a.shape[0], 4, -1), (1, 0, 2))
    colblk_h = lambda a: jnp.transpose(a.reshape(2, a.shape[0] // 2, 4, -1), (0, 2, 1, 3))
    rowblk_h = lambda a: jnp.transpose(a.reshape(4, 2, -1, a.shape[1]), (1, 0, 2, 3))
    gw_in = jnp.concatenate([G["w_main"][:, :off_if], G["w_if"][:, :2 * N_HEADS], G["w_main"][:, off_if:]], axis=1)
    assert gw_in.shape[1] == n_in
    gh = dict(w_in=colblk_h(gw_in), w_conv_out=rowblk_h(G["w_conv_out"]), w_ml_out=rowblk_h(G["w_ml_out"]),
              w_mem_kv=colblk_h(G["w_mem_kv"]), w_xa_out=rowblk_h(G["w_xa_out"]), w_out=rowblk_h(G["w_out"]))
    c = lax.axis_index("c")
    gh = {n: a.astype(BF16) for n, a in gh.items()}
    mine = [lax.dynamic_index_in_dim(gh[n], c, 0, keepdims=False) for n in big]
    theirs = _to_sibling([lax.dynamic_index_in_dim(gh[n], 1 - c, 0, keepdims=False) for n in big], "pair_swap")
    chip_sums = [_add_pair(a, b, "chip_sum_" + n) for n, a, b in zip(big, mine, theirs)]
    landed = [lax.dynamic_update_index_in_dim(l, lax.dynamic_index_in_dim(s, my_chip, 0), my_chip, 0)
              for l, s in zip(_chip_scatter(chip_sums), chip_sums)]
    halves_mine = [_sum_chips(a, "sum_chips_" + n) for n, a in zip(big, landed)]
    halves_theirs = _to_sibling(halves_mine, "share_halves")
    out = {}
    for n, a, b in zip(big, halves_mine, halves_theirs):
        out[n] = _adamw_halves(a, b, W[n], Mo[n], Vo[n], "adamw_" + n)

    pad = lambda a: jnp.pad(a, (0, D - a.shape[0]))
    small_g = jnp.stack([pad(G[n]) for n in SMALL])
    lands = _scatter_grads([colblk(jnp.concatenate([G["wq"], G["wk"]], axis=1)), colblk(G["w_dw"])], small_g)
    for n, land in zip(("w_qk_conv", "w_dw"), lands[:-1]):
        out[n] = _adamw(land, W[n], Mo[n], Vo[n], "adamw_" + n)
    res = _adamw(lands[-1], jnp.stack([pad(W[n]) for n in SMALL]), jnp.stack([pad(Mo[n]) for n in SMALL]),
                 jnp.stack([pad(Vo[n]) for n in SMALL]), "adamw_small")
    for i, n in enumerate(SMALL):
        out[n] = tuple(r[i, :W[n].shape[0]] for r in res)
    order = ("g_pre", "w_in", "b_if", "w_qk_conv", "w_dw", "b_dw", "g_ln", "b_ln", "w_conv_out", "g_ml_head", "w_ml_out",
             "g_mem", "w_mem_kv", "w_xa_out", "w_out", "g_post")
    return (loss, grad_x[None], *[out[n][0] for n in order], *[out[n][1] for n in order], *[out[n][2] for n in order],
            *[out[n][3] for n in order])
```

```python
import jax
import jax.numpy as jnp
from jax import lax
from jax.experimental import pallas as pl
from jax.experimental.pallas import tpu as pltpu

F32, BF16 = jnp.float32, jnp.bfloat16
EPS = 1e-6
N_HEADS = 4
CONV_W = 31
QK_W = 4
HALO = 32
QK_HALO = 8
ROW_TILE = 256
ML_CHUNK = 256
VMEM_LIMIT = 56 * 1024 * 1024
NEG = -1e30
ADAM_LR, ADAM_B1, ADAM_B2, ADAM_EPS, ADAM_WD, ADAM_STEP = 0.001, 0.9, 0.999, 1e-08, 0.01, 10
MESH = pl.DeviceIdType.MESH
N_DEV = 8

NN = (((1,), (0,)), ((), ()))
NT = (((1,), (1,)), ((), ()))
TN = (((0,), (0,)), ((), ()))


def _call(body, **kw):
    return pl.pallas_call(body, **kw)


def _cp(sem):
    return pltpu.CompilerParams(dimension_semantics=sem, vmem_limit_bytes=VMEM_LIMIT)


def _sds(shape, dt):
    return jax.ShapeDtypeStruct(shape, dt)


def _row(tm, d, col=0):
    return pl.BlockSpec((tm, d), lambda i: (i, col))


def _par(r, d):
    return pl.BlockSpec((r, d), lambda i: (0, 0))


def _prev(tm, hb, d, col=0):
    return pl.BlockSpec((hb, d), lambda i: (jnp.maximum(i * (tm // hb) - 1, 0), col))


def _next(tm, hb, d, nblk, col=0):
    return pl.BlockSpec((hb, d), lambda i: (jnp.minimum((i + 1) * (tm // hb), nblk - 1), col))


def _dot(a, b, dn):
    return lax.dot_general(a.astype(BF16), b.astype(BF16), dn, preferred_element_type=F32)


def _sg(x):
    return jax.nn.sigmoid(x)


def _dsilu(z, s):
    return s * (1.0 + z * (1.0 - s))


def _mean(x):
    return jnp.mean(x, axis=-1, keepdims=True)


def _pick(n, pref):
    if n <= pref:
        return n
    t = pref
    while n % t:
        t -= 128
    return t


def _mm(a, b, mode, out_dtype, name, tm=1024, tn=1024, tk=2048):
    if mode == "nn":
        (M, K), N = a.shape, b.shape[1]
    elif mode == "nt":
        (M, K), N = a.shape, b.shape[0]
    else:
        (K, M), N = a.shape, b.shape[1]
    tm, tn, tk = _pick(M, tm), _pick(N, tn), _pick(K, tk)
    nk = K // tk
    if mode == "nn":
        sa, sb, dn = pl.BlockSpec((tm, tk), lambda i, j, k: (i, k)), pl.BlockSpec((tk, tn), lambda i, j, k: (k, j)), NN
    elif mode == "nt":
        sa, sb, dn = pl.BlockSpec((tm, tk), lambda i, j, k: (i, k)), pl.BlockSpec((tn, tk), lambda i, j, k: (j, k)), NT
    else:
        sa, sb, dn = pl.BlockSpec((tk, tm), lambda i, j, k: (k, i)), pl.BlockSpec((tk, tn), lambda i, j, k: (k, j)), TN

    def body(a_ref, b_ref, o_ref, *acc):
        p = _dot(a_ref[...], b_ref[...], dn)
        if nk == 1:
            o_ref[...] = p.astype(out_dtype)
        else:
            acc_ref, k = acc[0], pl.program_id(2)

            @pl.when(k == 0)
            def _():
                acc_ref[...] = p

            @pl.when(k > 0)
            def _():
                acc_ref[...] += p

            @pl.when(k == nk - 1)
            def _():
                o_ref[...] = acc_ref[...].astype(out_dtype)

    return _call(body, name=name, grid=(M // tm, N // tn, nk), in_specs=[sa, sb],
                 out_specs=pl.BlockSpec((tm, tn), lambda i, j, k: (i, j)), out_shape=_sds((M, N), out_dtype),
                 scratch_shapes=[pltpu.VMEM((tm, tn), F32)] if nk > 1 else [],
                 compiler_params=_cp(("parallel", "parallel", "arbitrary")))(a, b)


def _rms_fwd(x, g, name):
    S, D = x.shape
    tm = min(ROW_TILE, S)

    def body(x_ref, g_ref, h_ref):
        xv = x_ref[...]
        r = lax.rsqrt(_mean(xv * xv) + EPS)
        h_ref[...] = (xv * r * g_ref[...]).astype(BF16)

    return _call(body, name=name, grid=(S // tm,), in_specs=[_row(tm, D), _par(1, D)], out_specs=_row(tm, D),
                 out_shape=_sds((S, D), BF16), compiler_params=_cp(("parallel",)))(x, g)


def _rms_bwd(x, g, ds, resid, name):
    S, D = x.shape
    tm = min(ROW_TILE, S)
    nd = len(ds)

    def body(*refs):
        x_ref, g_ref = refs[:2]
        d_refs = refs[2:2 + nd]
        r_ref = refs[2 + nd] if resid is not None else None
        dx_ref, dg_ref = refs[-2:]
        xv = x_ref[...]
        d = d_refs[0][...]
        for dr in d_refs[1:]:
            d = d + dr[...]
        r = lax.rsqrt(_mean(xv * xv) + EPS)
        dxh = d * g_ref[...]
        dx = r * dxh - xv * (r * r * r) * _mean(dxh * xv)
        if r_ref is not None:
            dx = dx + r_ref[...]
        dx_ref[...] = dx

        @pl.when(pl.program_id(0) == 0)
        def _():
            dg_ref[...] = jnp.zeros_like(dg_ref)

        dg_ref[...] += jnp.sum(d * xv * r, axis=0, keepdims=True)

    ins = [x, g, *ds] + ([resid] if resid is not None else [])
    specs = [_row(tm, D), _par(1, D)] + [_row(tm, D)] * (len(ins) - 2)
    return _call(body, name=name, grid=(S // tm,), in_specs=specs, out_specs=[_row(tm, D), _par(1, D)],
                 out_shape=[_sds((S, D), F32), _sds((1, D), F32)], compiler_params=_cp(("arbitrary",)))(*ins)


def _shifted_views(buf, sh, shifts, tm):
    for r in range(8):
        group = [(i, s) for i, s in enumerate(shifts) if s % 8 == r]
        if not group:
            continue
        if r:
            top = max(s for _, s in group) - r + tm
            sh[0:top, :] = buf[r:r + top, :]
        src = sh if r else buf
        for i, s in group:
            yield i, src[s - r:s - r + tm, :]


def _conv_fwd(P, w_dw, b_dw, g_ln, b_ln, D):
    S = P.shape[0]
    tm = min(ROW_TILE, S)

    def body(a_ref, b_ref, ap_ref, bp_ref, z_ref, w_ref, bd_ref, g_ref, bl_ref, ca_ref, u1_ref, buf, sh):
        first = pl.program_id(0) == 0
        hp = ap_ref[...] * _sg(bp_ref[...])
        buf[0:HALO, :] = jnp.where(first, 0.0, hp)
        buf[HALO:HALO + tm, :] = a_ref[...] * _sg(b_ref[...])
        acc = jnp.zeros((tm, D), F32)
        for j, view in _shifted_views(buf, sh, [HALO - CONV_W + 1 + j for j in range(CONV_W)], tm):
            acc = acc + w_ref[j:j + 1, :] * view
        u1 = acc + bd_ref[...]
        u1_ref[...] = u1
        xc = u1 - _mean(u1)
        r = lax.rsqrt(_mean(xc * xc) + EPS)
        u2 = xc * r * g_ref[...] + bl_ref[...]
        z = z_ref[...]
        ca_ref[...] = (u2 * _sg(u2) * (z * _sg(z))).astype(BF16)

    return _call(body, name="conv_fwd", grid=(S // tm,),
                 in_specs=[_row(tm, D, 0), _row(tm, D, 1), _prev(tm, HALO, D, 0), _prev(tm, HALO, D, 1), _row(tm, D, 2),
                           _par(HALO, D), _par(1, D), _par(1, D), _par(1, D)],
                 out_specs=[_row(tm, D), _row(tm, D)], out_shape=[_sds((S, D), BF16), _sds((S, D), F32)],
                 scratch_shapes=[pltpu.VMEM((tm + HALO, D), F32)] * 2, compiler_params=_cp(("parallel",)))(
        P, P, P, P, P, w_dw, b_dw, g_ln, b_ln)


def _conv_bwd_local(d_ca, u1, P, g_ln, b_ln, D):
    S = P.shape[0]
    tm = min(ROW_TILE, S)

    def body(dca_ref, u1_ref, z_ref, g_ref, bl_ref, du1_ref, dz_ref, dg_ref, db_ref):
        u1, z, dca, g = u1_ref[...], z_ref[...], dca_ref[...], g_ref[...]
        xc = u1 - _mean(u1)
        r = lax.rsqrt(_mean(xc * xc) + EPS)
        xh = xc * r
        u2 = xh * g + bl_ref[...]
        s2, sz = _sg(u2), _sg(z)
        d_u3 = dca * (z * sz)
        dz_ref[...] = (dca * (u2 * s2) * _dsilu(z, sz)).astype(BF16)
        d_u2 = d_u3 * _dsilu(u2, s2)
        dxh = d_u2 * g
        du1_ref[...] = r * (dxh - _mean(dxh) - xh * _mean(dxh * xh))

        @pl.when(pl.program_id(0) == 0)
        def _():
            dg_ref[...] = jnp.zeros_like(dg_ref)
            db_ref[...] = jnp.zeros_like(db_ref)

        dg_ref[...] += jnp.sum(d_u2 * xh, axis=0, keepdims=True)
        db_ref[...] += jnp.sum(d_u2, axis=0, keepdims=True)

    return _call(body, name="conv_bwd_local", grid=(S // tm,),
                 in_specs=[_row(tm, D), _row(tm, D), _row(tm, D, 2), _par(1, D), _par(1, D)],
                 out_specs=[_row(tm, D), _row(tm, D), _par(1, D), _par(1, D)],
                 out_shape=[_sds((S, D), F32), _sds((S, D), BF16), _sds((1, D), F32), _sds((1, D), F32)],
                 compiler_params=_cp(("arbitrary",)))(d_ca, u1, P, g_ln, b_ln)


def _conv_bwd_taps(d_u1, P, w_dw, D):
    S = P.shape[0]
    tm = min(ROW_TILE, S)
    nblk = S // HALO

    def body(d_ref, dn_ref, a_ref, b_ref, ap_ref, bp_ref, w_ref, da_ref, db_ref, gw_ref, gb_ref, dbuf, ubuf, sh):
        i = pl.program_id(0)
        a, sb = a_ref[...], _sg(b_ref[...])
        d = d_ref[...]
        dbuf[0:tm, :] = d
        dbuf[tm:tm + HALO, :] = jnp.where(i == pl.num_programs(0) - 1, 0.0, dn_ref[...])
        ubuf[0:HALO, :] = jnp.where(i == 0, 0.0, ap_ref[...] * _sg(bp_ref[...]))
        ubuf[HALO:HALO + tm, :] = a * sb

        @pl.when(i == 0)
        def _():
            gw_ref[...] = jnp.zeros_like(gw_ref)
            gb_ref[...] = jnp.zeros_like(gb_ref)

        acc = jnp.zeros((tm, D), F32)
        for j, view in _shifted_views(dbuf, sh, [CONV_W - 1 - j for j in range(CONV_W)], tm):
            acc = acc + w_ref[j:j + 1, :] * view
        for j, view in _shifted_views(ubuf, sh, [HALO - CONV_W + 1 + j for j in range(CONV_W)], tm):
            gw_ref[j:j + 1, :] += jnp.sum(d * view, axis=0, keepdims=True)
        gb_ref[...] += jnp.sum(d, axis=0, keepdims=True)
        da_ref[...] = (acc * sb).astype(BF16)
        db_ref[...] = (acc * a * sb * (1.0 - sb)).astype(BF16)

    return _call(body, name="conv_bwd_taps", grid=(S // tm,),
                 in_specs=[_row(tm, D), _next(tm, HALO, D, nblk), _row(tm, D, 0), _row(tm, D, 1),
                           _prev(tm, HALO, D, 0), _prev(tm, HALO, D, 1), _par(HALO, D)],
                 out_specs=[_row(tm, D), _row(tm, D), _par(HALO, D), _par(1, D)],
                 out_shape=[_sds((S, D), BF16), _sds((S, D), BF16), _sds((HALO, D), F32), _sds((1, D), F32)],
                 scratch_shapes=[pltpu.VMEM((tm + HALO, D), F32)] * 3,
                 compiler_params=_cp(("arbitrary",)))(d_u1, d_u1, P, P, P, P, w_dw)


def _qk_fwd(P, wq, wk, D, col):
    S = P.shape[0]
    tm = min(ROW_TILE, S)
    scale = float(D // N_HEADS) ** -0.5

    def body(q_ref, k_ref, qp_ref, kp_ref, wq_ref, wk_ref, qo_ref, ko_ref, cq_ref, ck_ref, buf):
        first = pl.program_id(0) == 0
        for x_ref, p_ref, w_ref, o_ref, c_ref, sc in ((q_ref, qp_ref, wq_ref, qo_ref, cq_ref, 1.0),
                                                      (k_ref, kp_ref, wk_ref, ko_ref, ck_ref, scale)):
            buf[0:QK_HALO, :] = jnp.where(first, 0.0, p_ref[...])
            buf[QK_HALO:QK_HALO + tm, :] = x_ref[...]
            acc = jnp.zeros((tm, D), F32)
            for j in range(QK_W):
                sh = QK_HALO - QK_W + 1 + j
                acc = acc + w_ref[j:j + 1, :] * buf[sh:sh + tm, :]
            c_ref[...] = acc
            o_ref[...] = acc * _sg(acc) * sc

    return _call(body, name="qk_fwd", grid=(S // tm,),
                 in_specs=[_row(tm, D, col), _row(tm, D, col + 1), _prev(tm, QK_HALO, D, col), _prev(tm, QK_HALO, D, col + 1),
                           _par(QK_HALO, D), _par(QK_HALO, D)],
                 out_specs=[_row(tm, D)] * 4, out_shape=[_sds((S, D), F32)] * 4,
                 scratch_shapes=[pltpu.VMEM((tm + QK_HALO, D), F32)], compiler_params=_cp(("parallel",)))(P, P, P, P, wq, wk)


def _qk_bwd(dq, dk, cq, ck, P, wq, wk, D, col):
    S = P.shape[0]
    tm = min(ROW_TILE, S)
    nblk = S // QK_HALO
    scale = float(D // N_HEADS) ** -0.5

    def body(dq_ref, dqn_ref, cq_ref, cqn_ref, xq_ref, xqp_ref, wq_ref,
             dk_ref, dkn_ref, ck_ref, ckn_ref, xk_ref, xkp_ref, wk_ref,
             oq_ref, ok_ref, gq_ref, gk_ref, dbuf, xbuf):
        i = pl.program_id(0)
        last = i == pl.num_programs(0) - 1

        @pl.when(i == 0)
        def _():
            gq_ref[...] = jnp.zeros_like(gq_ref)
            gk_ref[...] = jnp.zeros_like(gk_ref)

        for d_ref, dn_ref, c_ref, cn_ref, x_ref, xp_ref, w_ref, o_ref, g_ref, sc in (
                (dq_ref, dqn_ref, cq_ref, cqn_ref, xq_ref, xqp_ref, wq_ref, oq_ref, gq_ref, 1.0),
                (dk_ref, dkn_ref, ck_ref, ckn_ref, xk_ref, xkp_ref, wk_ref, ok_ref, gk_ref, scale)):
            c, cn = c_ref[...], cn_ref[...]
            dc = d_ref[...] * sc * _dsilu(c, _sg(c))
            dcn = dn_ref[...] * sc * _dsilu(cn, _sg(cn))
            dbuf[0:tm, :] = dc
            dbuf[tm:tm + QK_HALO, :] = jnp.where(last, 0.0, dcn)
            xbuf[0:QK_HALO, :] = jnp.where(i == 0, 0.0, xp_ref[...])
            xbuf[QK_HALO:QK_HALO + tm, :] = x_ref[...]
            acc = jnp.zeros((tm, D), F32)
            for j in range(QK_W):
                acc = acc + w_ref[j:j + 1, :] * dbuf[QK_W - 1 - j:QK_W - 1 - j + tm, :]
                sh = QK_HALO - QK_W + 1 + j
                g_ref[j:j + 1, :] += jnp.sum(dc * xbuf[sh:sh + tm, :], axis=0, keepdims=True)
            o_ref[...] = acc.astype(BF16)

    one = [_row(tm, D), _next(tm, QK_HALO, D, nblk)]
    specs = (one + one + [_row(tm, D, col), _prev(tm, QK_HALO, D, col), _par(QK_HALO, D)]
             + one + one + [_row(tm, D, col + 1), _prev(tm, QK_HALO, D, col + 1), _par(QK_HALO, D)])
    return _call(body, name="qk_bwd", grid=(S // tm,), in_specs=specs,
                 out_specs=[_row(tm, D), _row(tm, D), _par(QK_HALO, D), _par(QK_HALO, D)],
                 out_shape=[_sds((S, D), BF16), _sds((S, D), BF16), _sds((QK_HALO, D), F32), _sds((QK_HALO, D), F32)],
                 scratch_shapes=[pltpu.VMEM((tm + QK_HALO, D), F32), pltpu.VMEM((tm + QK_HALO, D), F32)],
                 compiler_params=_cp(("arbitrary",)))(dq, dq, cq, cq, P, P, wq, dk, dk, ck, ck, P, P, wk)


def _gates_fwd(gif, b_if):
    S = gif.shape[0]
    tm = min(ROW_TILE, S)

    def body(g_ref, b_ref, o_ref):
        z = g_ref[...] + b_ref[...]
        lane = lax.broadcasted_iota(jnp.int32, z.shape, 1)
        ls = jnp.minimum(z, 0.0) - jnp.log(1.0 + jnp.exp(-jnp.abs(z)))
        o_ref[...] = jnp.where(lane < N_HEADS, z, jnp.where(lane < 2 * N_HEADS, ls, 0.0))

    return _call(body, name="gates_fwd", grid=(S // tm,), in_specs=[_row(tm, 128), _par(1, 128)], out_specs=_row(tm, 128),
                 out_shape=_sds((S, 128), F32), compiler_params=_cp(("parallel",)))(gif, b_if)


def _gates_bwd(dgl, gif, b_if):
    S = gif.shape[0]
    tm = min(ROW_TILE, S)

    def body(d_ref, g_ref, b_ref, o_ref, gb_ref):
        z = g_ref[...] + b_ref[...]
        lane = lax.broadcasted_iota(jnp.int32, z.shape, 1)
        d = d_ref[...]
        dz = jnp.where(lane < N_HEADS, d, jnp.where(lane < 2 * N_HEADS, d * _sg(-z), 0.0))
        o_ref[...] = dz.astype(BF16)

        @pl.when(pl.program_id(0) == 0)
        def _():
            gb_ref[...] = jnp.zeros_like(gb_ref)

        gb_ref[...] += jnp.sum(dz, axis=0, keepdims=True)

    return _call(body, name="gates_bwd", grid=(S // tm,), in_specs=[_row(tm, 128), _row(tm, 128), _par(1, 128)],
                 out_specs=[_row(tm, 128), _par(1, 128)], out_shape=[_sds((S, 128), BF16), _sds((1, 128), F32)],
                 compiler_params=_cp(("arbitrary",)))(dgl, gif, b_if)


def _chunk_gates(g8_ref, gt_ref, h, L):
    g8, gt = g8_ref[...], gt_ref[...]
    lane = lax.broadcasted_iota(jnp.int32, g8.shape, 1)
    sub = lax.broadcasted_iota(jnp.int32, gt.shape, 0)
    li_col = jnp.sum(jnp.where(lane == h, g8, 0.0), axis=1, keepdims=True)
    lf_col = jnp.sum(jnp.where(lane == h + N_HEADS, g8, 0.0), axis=1, keepdims=True)
    li_row = jnp.sum(jnp.where(sub == h, gt, 0.0), axis=0, keepdims=True)
    lf_row = jnp.sum(jnp.where(sub == h + N_HEADS, gt, 0.0), axis=0, keepdims=True)
    t = lax.broadcasted_iota(jnp.int32, (L, L), 0)
    s = lax.broadcasted_iota(jnp.int32, (L, L), 1)
    causal = s <= t
    b_col = jnp.sum(jnp.where(causal, lf_row, 0.0), axis=1, keepdims=True)
    b_row = jnp.sum(jnp.where(t <= s, lf_col, 0.0), axis=0, keepdims=True)
    return li_col, li_row, b_col, b_row, causal


def _chunk_fwd(q, k, v, C, n, m, li_row, b_col, b_row, causal):
    d = jnp.where(causal, b_col - b_row + li_row, NEG)
    inter = b_col + m
    m_row = jnp.maximum(inter, jnp.max(d, axis=1, keepdims=True))
    wi = jnp.exp(d - m_row)
    wn = jnp.exp(inter - m_row)
    s = _dot(q, k, NT) * wi
    num = _dot(s, v, NN) + wn * _dot(q, C, NN)
    den = jnp.sum(s, axis=1, keepdims=True) + wn * jnp.sum(q * n, axis=1, keepdims=True)
    e = jnp.exp(-m_row)
    inv = 1.0 / jnp.maximum(jnp.abs(den), e)
    return wi, wn, s, num * inv, den, e, inv


def _chunk_state(li_col, li_row, b_col, b_row, m, L):
    blast = b_col[L - 1:L, :]
    g_col = blast - b_col + li_col
    g_row = blast - b_row + li_row
    m_new = jnp.maximum(blast + m, jnp.max(g_row, axis=1, keepdims=True))
    decay = jnp.exp(blast + m - m_new)
    wk_col = jnp.exp(g_col - m_new)
    return m_new, decay, wk_col


def _mlstm_fwd(q, k, P, g8, gt, D, vcol):
    S = q.shape[0]
    H, dh = N_HEADS, D // N_HEADS
    L = min(ML_CHUNK, S)
    nc = S // L

    def body(q_ref, k_ref, v_ref, g8_ref, gt_ref, h_ref, cs_ref, ns_ref, ms_ref, C, n, m):
        h, j = pl.program_id(0), pl.program_id(1)

        @pl.when(j == 0)
        def _():
            C[...] = jnp.zeros_like(C)
            n[...] = jnp.zeros_like(n)
            m[...] = jnp.zeros_like(m)

        qv, kv, vv = q_ref[...], k_ref[...], v_ref[...]
        Cv, nv, mv = C[...], n[...], m[:, 0:1]
        cs_ref[...] = Cv.astype(BF16)
        ns_ref[...] = nv
        ms_ref[...] = m[...]
        li_col, li_row, b_col, b_row, causal = _chunk_gates(g8_ref, gt_ref, h, L)
        _, _, _, hv, _, _, _ = _chunk_fwd(qv, kv, vv, Cv, nv, mv, li_row, b_col, b_row, causal)
        h_ref[...] = hv
        m_new, decay, wk_col = _chunk_state(li_col, li_row, b_col, b_row, mv, L)
        kw = kv * wk_col
        C[...] = decay * Cv + _dot(kw, vv, TN)
        n[...] = decay * nv + jnp.sum(kw, axis=0, keepdims=True)
        m[...] = jnp.broadcast_to(m_new, m.shape)

    blk = lambda c0: pl.BlockSpec((L, dh), lambda h, j: (j, c0 + h))
    return _call(body, name="mlstm_fwd", grid=(H, nc),
                 in_specs=[blk(0), blk(0), blk(vcol * H), pl.BlockSpec((L, 8), lambda h, j: (j, 0)),
                           pl.BlockSpec((8, L), lambda h, j: (0, j))],
                 out_specs=[blk(0), pl.BlockSpec((None, None, dh, dh), lambda h, j: (h, j, 0, 0)),
                            pl.BlockSpec((None, None, 1, dh), lambda h, j: (h, j, 0, 0)),
                            pl.BlockSpec((None, None, 1, 128), lambda h, j: (h, j, 0, 0))],
                 out_shape=[_sds((S, D), F32), _sds((H, nc, dh, dh), BF16), _sds((H, nc, 1, dh), F32),
                            _sds((H, nc, 1, 128), F32)],
                 scratch_shapes=[pltpu.VMEM((dh, dh), F32), pltpu.VMEM((1, dh), F32), pltpu.VMEM((1, 128), F32)],
                 compiler_params=_cp(("arbitrary", "arbitrary")))(q, k, P, g8, gt)


def _mlstm_bwd(dhm, q, k, P, g8, gt, cs, ns, ms, D, vcol):
    S = q.shape[0]
    H, dh = N_HEADS, D // N_HEADS
    L = min(ML_CHUNK, S)
    nc = S // L

    def body(dh_ref, q_ref, k_ref, v_ref, g8_ref, gt_ref, cs_ref, ns_ref, ms_ref,
             dq_ref, dk_ref, dv_ref, dg_ref, dC, dnv):
        h, j = pl.program_id(0), pl.program_id(1)

        @pl.when(j == 0)
        def _():
            dC[...] = jnp.zeros_like(dC)
            dnv[...] = jnp.zeros_like(dnv)

        qv, kv, vv, dhv = q_ref[...], k_ref[...], v_ref[...], dh_ref[...]
        Cv, nv, mv = cs_ref[...], ns_ref[...], ms_ref[:, 0:1]
        li_col, li_row, b_col, b_row, causal = _chunk_gates(g8_ref, gt_ref, h, L)
        wi, wn, s, hv, den, e, inv = _chunk_fwd(qv, kv, vv, Cv, nv, mv, li_row, b_col, b_row, causal)
        dn = dhv * inv
        dd = -jnp.sum(dhv * hv, axis=1, keepdims=True) * inv * jnp.where(jnp.abs(den) > e, jnp.sign(den), 0.0)
        dS = _dot(dn, vv, NT) + dd
        dqk = dS * wi
        dq_i = wn * (_dot(dn, Cv, NT) + dd * nv)
        dq = _dot(dqk, kv, NN) + dq_i
        dv = _dot(s, dn, TN)
        m_new, decay, wk_col = _chunk_state(li_col, li_row, b_col, b_row, mv, L)
        dCv, dnvv = dC[...], dnv[...]
        dk_i = wk_col * (_dot(vv, dCv, NT) + dnvv)
        dk = _dot(dqk, qv, TN) + dk_i
        dv = dv + _dot(kv * wk_col, dCv, NN)
        qw = qv * wn
        dC[...] = decay * dCv + _dot(qw, dn, TN)
        dnv[...] = decay * dnvv + jnp.sum(qw * dd, axis=0, keepdims=True)
        dq_ref[...] = dq
        dk_ref[...] = dk
        dv_ref[...] = dv.astype(BF16)
        G = dS * s
        t = lax.broadcasted_iota(jnp.int32, (L, L), 0)
        r = lax.broadcasted_iota(jnp.int32, (L, L), 1)
        eye = t == r
        col_row = jnp.sum(G, axis=0, keepdims=True)
        col_g = jnp.sum(jnp.where(eye, col_row, 0.0), axis=1, keepdims=True)
        row_i = jnp.sum(qv * dq_i, axis=1, keepdims=True)
        col_i = jnp.sum(kv * dk_i, axis=1, keepdims=True)
        dF = jnp.sum(G, axis=1, keepdims=True) - col_g + row_i
        across = decay * (jnp.sum(jnp.sum(dCv * Cv.astype(F32), axis=0, keepdims=True), axis=1, keepdims=True)
                          + jnp.sum(dnvv * nv, axis=1, keepdims=True))
        dlf = jnp.sum(jnp.where(t >= r, dF, 0.0) + jnp.where(t < r, col_i, 0.0), axis=0, keepdims=True) + across
        dli = col_row + jnp.sum(jnp.where(eye, col_i, 0.0), axis=0, keepdims=True)
        sub = lax.broadcasted_iota(jnp.int32, (8, L), 0)
        dg_ref[...] = jnp.where(sub == 0, dli, jnp.where(sub == 1, dlf, 0.0))

    blk = lambda c0: pl.BlockSpec((L, dh), lambda h, j: (nc - 1 - j, c0 + h))
    st = lambda r, c: pl.BlockSpec((None, None, r, c), lambda h, j: (h, nc - 1 - j, 0, 0))
    return _call(body, name="mlstm_bwd", grid=(H, nc),
                 in_specs=[blk(0), blk(0), blk(0), blk(vcol * H), pl.BlockSpec((L, 8), lambda h, j: (nc - 1 - j, 0)),
                           pl.BlockSpec((8, L), lambda h, j: (0, nc - 1 - j)), st(dh, dh), st(1, dh), st(1, 128)],
                 out_specs=[blk(0), blk(0), blk(0), pl.BlockSpec((None, 8, L), lambda h, j: (h, 0, nc - 1 - j))],
                 out_shape=[_sds((S, D), F32), _sds((S, D), F32), _sds((S, D), BF16), _sds((H, 8, S), F32)],
                 scratch_shapes=[pltpu.VMEM((dh, dh), F32), pltpu.VMEM((1, dh), F32)],
                 compiler_params=_cp(("arbitrary", "arbitrary")))(dhm, q, k, P, g8, gt, cs, ns, ms)


def _ml_post_fwd(hm, P, g, D, ocol, zcol):
    S = hm.shape[0]
    tm = min(ROW_TILE, S)
    dh = D // N_HEADS

    def body(h_ref, o_ref, z_ref, g_ref, cb_ref):
        for hd in range(N_HEADS):
            sl = slice(hd * dh, (hd + 1) * dh)
            x = _sg(o_ref[:, sl]) * h_ref[:, sl]
            r = lax.rsqrt(_mean(x * x) + EPS)
            z = z_ref[:, sl]
            cb_ref[:, sl] = (x * r * g_ref[:, sl] * (z * _sg(z))).astype(BF16)

    return _call(body, name="ml_post_fwd", grid=(S // tm,),
                 in_specs=[_row(tm, D), _row(tm, D, ocol), _row(tm, D, zcol), _par(1, D)], out_specs=_row(tm, D),
                 out_shape=_sds((S, D), BF16), compiler_params=_cp(("parallel",)))(hm, P, P, g)


def _ml_post_bwd(d_cb, hm, P, g, D, ocol, zcol):
    S = hm.shape[0]
    tm = min(ROW_TILE, S)
    dh = D // N_HEADS

    def body(d_ref, h_ref, o_ref, z_ref, g_ref, dh_ref, do_ref, dz_ref, dg_ref):
        @pl.when(pl.program_id(0) == 0)
        def _():
            dg_ref[...] = jnp.zeros_like(dg_ref)

        for hd in range(N_HEADS):
            sl = slice(hd * dh, (hd + 1) * dh)
            hv, z, gv, d = h_ref[:, sl], z_ref[:, sl], g_ref[:, sl], d_ref[:, sl]
            so, sz = _sg(o_ref[:, sl]), _sg(z)
            x = so * hv
            r = lax.rsqrt(_mean(x * x) + EPS)
            xh = x * r
            d3 = d * (z * sz)
            dz_ref[:, sl] = (d * xh * gv * _dsilu(z, sz)).astype(BF16)
            dg_ref[:, sl] += jnp.sum(d3 * xh, axis=0, keepdims=True)
            dxh = d3 * gv
            d2 = r * (dxh - xh * _mean(dxh * xh))
            do_ref[:, sl] = (d2 * hv * so * (1.0 - so)).astype(BF16)
            dh_ref[:, sl] = d2 * so

    return _call(body, name="ml_post_bwd", grid=(S // tm,),
                 in_specs=[_row(tm, D), _row(tm, D), _row(tm, D, ocol), _row(tm, D, zcol), _par(1, D)],
                 out_specs=[_row(tm, D), _row(tm, D), _row(tm, D), _par(1, D)],
                 out_shape=[_sds((S, D), F32), _sds((S, D), BF16), _sds((S, D), BF16), _sds((1, D), F32)],
                 compiler_params=_cp(("arbitrary",)))(d_cb, hm, P, P, g)


def _attn_scores(q, kh, scale):
    sc = _dot(q, kh, NT) * scale
    ex = jnp.exp(sc - jnp.max(sc, axis=1, keepdims=True))
    return ex / jnp.sum(ex, axis=1, keepdims=True)


def _attn_fwd(P, kv, D, qcol, zcol):
    S, M = P.shape[0], kv.shape[0]
    tm = min(ROW_TILE, S)
    dh = D // N_HEADS
    scale = float(dh) ** -0.5

    def body(q_ref, z_ref, kv_ref, cc_ref):
        for hd in range(N_HEADS):
            sl = slice(hd * dh, (hd + 1) * dh)
            p = _attn_scores(q_ref[:, sl], kv_ref[:, sl], scale)
            o = _dot(p, kv_ref[:, D + hd * dh:D + (hd + 1) * dh], NN)
            z = z_ref[:, sl]
            cc_ref[:, sl] = (o * (z * _sg(z))).astype(BF16)

    return _call(body, name="attn_fwd", grid=(S // tm,),
                 in_specs=[_row(tm, D, qcol), _row(tm, D, zcol), _par(M, 2 * D)], out_specs=_row(tm, D),
                 out_shape=_sds((S, D), BF16), compiler_params=_cp(("parallel",)))(P, P, kv)


def _attn_bwd(d_cc, P, kv, D, qcol, zcol):
    S, M = P.shape[0], kv.shape[0]
    tm = min(ROW_TILE, S)
    dh = D // N_HEADS
    scale = float(dh) ** -0.5

    def body(d_ref, q_ref, z_ref, kv_ref, dq_ref, dz_ref, dkv_ref):
        @pl.when(pl.program_id(0) == 0)
        def _():
            dkv_ref[...] = jnp.zeros_like(dkv_ref)

        for hd in range(N_HEADS):
            sl = slice(hd * dh, (hd + 1) * dh)
            vl = slice(D + hd * dh, D + (hd + 1) * dh)
            q, kh, vh = q_ref[:, sl], kv_ref[:, sl], kv_ref[:, vl]
            p = _attn_scores(q, kh, scale)
            o = _dot(p, vh, NN)
            z, d = z_ref[:, sl], d_ref[:, sl]
            sz = _sg(z)
            do = d * (z * sz)
            dz_ref[:, sl] = (d * o * _dsilu(z, sz)).astype(BF16)
            dp = _dot(do, vh, NT)
            ds = p * (dp - jnp.sum(p * dp, axis=1, keepdims=True)) * scale
            dq_ref[:, sl] = _dot(ds, kh, NN).astype(BF16)
            dkv_ref[:, sl] += _dot(ds, q, TN)
            dkv_ref[:, vl] += _dot(p, do, TN)

    return _call(body, name="attn_bwd", grid=(S // tm,),
                 in_specs=[_row(tm, D), _row(tm, D, qcol), _row(tm, D, zcol), _par(M, 2 * D)],
                 out_specs=[_row(tm, D), _row(tm, D), _par(M, 2 * D)],
                 out_shape=[_sds((S, D), BF16), _sds((S, D), BF16), _sds((M, 2 * D), F32)],
                 compiler_params=_cp(("arbitrary",)))(d_cc, P, P, kv)


def _merge_fwd(yc, ym, yx, P, D, gcol):
    S = yc.shape[0]
    tm = min(ROW_TILE, S)

    def body(c_ref, m_ref, x_ref, gc_ref, gm_ref, gx_ref, o_ref):
        o_ref[...] = (_sg(gc_ref[...]) * c_ref[...] + _sg(gm_ref[...]) * m_ref[...]
                      + _sg(gx_ref[...]) * x_ref[...]).astype(BF16)

    return _call(body, name="merge_fwd", grid=(S // tm,),
                 in_specs=[_row(tm, D)] * 3 + [_row(tm, D, gcol), _row(tm, D, gcol + 1), _row(tm, D, gcol + 2)],
                 out_specs=_row(tm, D), out_shape=_sds((S, D), BF16), compiler_params=_cp(("parallel",)))(yc, ym, yx, P, P, P)


def _merge_bwd(dm, yc, ym, yx, P, D, gcol):
    S = yc.shape[0]
    tm = min(ROW_TILE, S)

    def body(d_ref, c_ref, m_ref, x_ref, gc_ref, gm_ref, gx_ref, dc_ref, dmm_ref, dx_ref, dgc_ref, dgm_ref, dgx_ref):
        d = d_ref[...]
        for y_ref, g_ref, dy_ref, dg_ref in ((c_ref, gc_ref, dc_ref, dgc_ref), (m_ref, gm_ref, dmm_ref, dgm_ref),
                                             (x_ref, gx_ref, dx_ref, dgx_ref)):
            s = _sg(g_ref[...])
            dy_ref[...] = (d * s).astype(BF16)
            dg_ref[...] = (d * y_ref[...] * s * (1.0 - s)).astype(BF16)

    return _call(body, name="merge_bwd", grid=(S // tm,),
                 in_specs=[_row(tm, D)] * 4 + [_row(tm, D, gcol), _row(tm, D, gcol + 1), _row(tm, D, gcol + 2)],
                 out_specs=[_row(tm, D)] * 6, out_shape=[_sds((S, D), BF16)] * 6,
                 compiler_params=_cp(("parallel",)))(dm, yc, ym, yx, P, P, P)


def _post(o, x, tgt, g):
    S, D = x.shape
    tm = min(ROW_TILE, S)

    def body(o_ref, x_ref, t_ref, g_ref, dy_ref, do_ref, dg_ref, l_ref):
        ov, gv = o_ref[...], g_ref[...]
        r = lax.rsqrt(_mean(ov * ov) + EPS)
        e = x_ref[...] + ov * r * gv - t_ref[...]
        dy = e / D
        dy_ref[...] = dy
        dxh = dy * gv
        do_ref[...] = (r * dxh - ov * (r * r * r) * _mean(dxh * ov)).astype(BF16)

        @pl.when(pl.program_id(0) == 0)
        def _():
            dg_ref[...] = jnp.zeros_like(dg_ref)
            l_ref[...] = jnp.zeros_like(l_ref)

        dg_ref[...] += jnp.sum(dy * ov * r, axis=0, keepdims=True)
        l_ref[...] += jnp.sum(e * e, axis=0, keepdims=True)

    return _call(body, name="post", grid=(S // tm,), in_specs=[_row(tm, D)] * 3 + [_par(1, D)],
                 out_specs=[_row(tm, D), _row(tm, D), _par(1, D), _par(1, D)],
                 out_shape=[_sds((S, D), F32), _sds((S, D), BF16), _sds((1, D), F32), _sds((1, D), F32)],
                 compiler_params=_cp(("arbitrary",)))(o, x, tgt, g)


def _chip_peers():
    x, y, c = lax.axis_index("x"), lax.axis_index("y"), lax.axis_index("c")
    return x, y, c, [(1 - x, y), (x, 1 - y), (1 - x, 1 - y)]


def _gather_chips(halved, whole):
    TH, TW = len(halved), len(whole)
    T = TH + TW

    def body(*refs):
        ins, outs = refs[:T], refs[T:2 * T]
        send, recv, fsend, frecv = refs[2 * T:]
        x, y, c, chips = _chip_peers()
        mine = 2 * x + y

        def over_ici(t, j, chip, slot):
            src = ins[t].at[c] if t < TH else ins[t]
            dst = outs[t].at[slot, c] if t < TH else outs[t].at[slot]
            return pltpu.make_async_remote_copy(src_ref=src, dst_ref=dst, send_sem=send.at[t, j], recv_sem=recv.at[t, j],
                                                device_id=(*chip, c), device_id_type=MESH)

        def to_sibling(t, j, slot, half):
            place = outs[t].at[slot, half]
            return pltpu.make_async_remote_copy(src_ref=place, dst_ref=place, send_sem=fsend.at[t, j], recv_sem=frecv.at[t, j],
                                                device_id=(x, y, 1 - c), device_id_type=MESH)

        sends = []
        for t in range(T):
            for j, chip in enumerate(chips):
                cp = over_ici(t, j, chip, mine)
                cp.start()
                sends.append(cp)
        for t in range(T):
            for j, (px, py) in enumerate(chips):
                over_ici(t, j, (px, py), 2 * px + py).wait_recv()
                if t < TH:
                    cp = to_sibling(t, j, 2 * px + py, c)
                    cp.start()
                    sends.append(cp)
        for t in range(TH):
            for j, (px, py) in enumerate(chips):
                to_sibling(t, j, 2 * px + py, 1 - c).wait_recv()
        for cp in sends:
            cp.wait_send()

    anyspec = pl.BlockSpec(memory_space=pl.ANY)
    dma = pltpu.SemaphoreType.DMA
    return _call(body, name="gather_weights", in_specs=[anyspec] * T, out_specs=[anyspec] * T,
                 out_shape=[_sds((4, *s.shape), s.dtype) for s in (*halved, *whole)],
                 scratch_shapes=[dma((T, 3)), dma((T, 3)), dma((TH, 3)), dma((TH, 3))],
                 compiler_params=pltpu.CompilerParams(has_side_effects=True))(*halved, *whole)


def _to_sibling(tensors, name):
    T = len(tensors)

    def body(*refs):
        ins, outs = refs[:T], refs[T:2 * T]
        send, recv = refs[2 * T:]
        x, y, c = lax.axis_index("x"), lax.axis_index("y"), lax.axis_index("c")
        cps = [pltpu.make_async_remote_copy(src_ref=ins[t], dst_ref=outs[t], send_sem=send.at[t], recv_sem=recv.at[t],
                                            device_id=(x, y, 1 - c), device_id_type=MESH) for t in range(T)]
        for cp in cps:
            cp.start()
        for cp in cps:
            cp.wait_recv()
        for cp in cps:
            cp.wait_send()

    anyspec = pl.BlockSpec(memory_space=pl.ANY)
    return _call(body, name=name, in_specs=[anyspec] * T, out_specs=[anyspec] * T,
                 out_shape=[_sds(a.shape, a.dtype) for a in tensors],
                 scratch_shapes=[pltpu.SemaphoreType.DMA((T,)), pltpu.SemaphoreType.DMA((T,))],
                 compiler_params=pltpu.CompilerParams(has_side_effects=True))(*tensors)


def _chip_scatter(tensors):
    T = len(tensors)

    def body(*refs):
        ins, outs = refs[:T], refs[T:2 * T]
        send, recv = refs[2 * T:]
        x, y, c, chips = _chip_peers()
        mine = 2 * x + y

        def over_ici(t, j, chip, src_slot, dst_slot):
            return pltpu.make_async_remote_copy(src_ref=ins[t].at[src_slot], dst_ref=outs[t].at[dst_slot], send_sem=send.at[t, j],
                                                recv_sem=recv.at[t, j], device_id=(*chip, c), device_id_type=MESH)

        sends = []
        for t in range(T):
            for j, (px, py) in enumerate(chips):
                cp = over_ici(t, j, (px, py), 2 * px + py, mine)
                cp.start()
                sends.append(cp)
        for t in range(T):
            for j, (px, py) in enumerate(chips):
                over_ici(t, j, (px, py), mine, 2 * px + py).wait_recv()
        for cp in sends:
            cp.wait_send()

    anyspec = pl.BlockSpec(memory_space=pl.ANY)
    dma = pltpu.SemaphoreType.DMA
    return _call(body, name="chip_scatter", in_specs=[anyspec] * T, out_specs=[anyspec] * T,
                 out_shape=[_sds(a.shape, a.dtype) for a in tensors], scratch_shapes=[dma((T, 3)), dma((T, 3))],
                 compiler_params=pltpu.CompilerParams(has_side_effects=True))(*tensors)


def _row_tile(R, C, slots):
    tr = max(16, ((8 << 20) // (C * 4 * slots)) // 16 * 16)
    if tr >= R:
        return R
    while R % tr:
        tr -= 16
    return tr


def _add_pair(a, b, name):
    lead, R, C = a.shape
    tr = _row_tile(R, C, 3)

    def body(a_ref, b_ref, o_ref):
        o_ref[...] = (a_ref[...].astype(F32) + b_ref[...].astype(F32)).astype(BF16)

    blk = pl.BlockSpec((None, tr, C), lambda k, i: (k, i, 0))
    return _call(body, name=name, grid=(lead, R // tr), in_specs=[blk, blk], out_specs=blk, out_shape=_sds(a.shape, BF16),
                 compiler_params=_cp(("parallel", "parallel")))(a, b)


def _sum_chips(land, name):
    _, R, C = land.shape
    tr = _row_tile(R, C, 5)

    def body(l_ref, o_ref):
        o_ref[...] = ((l_ref[0].astype(F32) + l_ref[1].astype(F32)) + l_ref[2].astype(F32)) + l_ref[3].astype(F32)

    return _call(body, name=name, grid=(R // tr,), in_specs=[pl.BlockSpec((4, tr, C), lambda i: (0, i, 0))],
                 out_specs=pl.BlockSpec((tr, C), lambda i: (i, 0)), out_shape=_sds((R, C), F32),
                 compiler_params=_cp(("parallel",)))(land)


def _scatter_grads(blocks, small):
    T = len(blocks)

    def body(*refs):
        ins, sm = refs[:T], refs[T]
        outs, smo = refs[T + 1:2 * T + 1], refs[2 * T + 1]
        send, recv, loc = refs[2 * T + 2:]
        x, y, c = lax.axis_index("x"), lax.axis_index("y"), lax.axis_index("c")
        me = 4 * x + 2 * y + c
        peers = [(x ^ ((k >> 2) & 1), y ^ ((k >> 1) & 1), c ^ (k & 1)) for k in range(1, N_DEV)]
        copies = []
        for t in range(T):
            lc = pltpu.make_async_copy(ins[t].at[2 * x + y], outs[t].at[me], loc.at[t])
            lc.start()
            copies.append(lc)
        lc = pltpu.make_async_copy(sm, smo.at[me], loc.at[T])
        lc.start()
        copies.append(lc)

        def remote(t, j, peer, sender):
            px, py, pc = peer
            src = sm if t == T else ins[t].at[2 * px + py]
            dst = (smo if t == T else outs[t]).at[sender]
            return pltpu.make_async_remote_copy(src_ref=src, dst_ref=dst, send_sem=send.at[t, j], recv_sem=recv.at[t, j],
                                                device_id=peer, device_id_type=MESH)

        sends = []
        for t in range(T + 1):
            for j, peer in enumerate(peers):
                cp = remote(t, j, peer, me)
                cp.start()
                sends.append(cp)
        for t in range(T + 1):
            for j, (px, py, pc) in enumerate(peers):
                remote(t, j, (x, y, c), 4 * px + 2 * py + pc).wait_recv()
        for cp in sends:
            cp.wait_send()
        for lc in copies:
            lc.wait()

    anyspec = pl.BlockSpec(memory_space=pl.ANY)
    return _call(body, name="scatter_grads", in_specs=[anyspec] * (T + 1), out_specs=[anyspec] * (T + 1),
                 out_shape=[_sds((N_DEV, *b.shape[1:]), b.dtype) for b in blocks] + [_sds((N_DEV, *small.shape), small.dtype)],
                 scratch_shapes=[pltpu.SemaphoreType.DMA((T + 1, N_DEV - 1)), pltpu.SemaphoreType.DMA((T + 1, N_DEV - 1)),
                                 pltpu.SemaphoreType.DMA((T + 1,))],
                 compiler_params=pltpu.CompilerParams(has_side_effects=True))(*blocks, small)


def _adam_update(g, w_ref, m_ref, v_ref, g_ref, d_ref, mo_ref, vo_ref):
    c1, c2 = 1.0 - ADAM_B1 ** ADAM_STEP, 1.0 - ADAM_B2 ** ADAM_STEP
    mn = ADAM_B1 * m_ref[...] + (1.0 - ADAM_B1) * g
    vn = ADAM_B2 * v_ref[...] + (1.0 - ADAM_B2) * (g * g)
    g_ref[...] = g
    mo_ref[...] = mn
    vo_ref[...] = vn
    d_ref[...] = -ADAM_LR * ((mn / c1) / (jnp.sqrt(vn / c2) + ADAM_EPS) + ADAM_WD * w_ref[...])


def _adam_rows(R, C):
    tr = R if R * C * 4 <= (1 << 20) else max(16, ((1 << 20) // (C * 4)) // 16 * 16)
    while R % tr:
        tr -= 16
    return tr


def _adamw(land, w, m, v, name):
    R, C = w.shape
    tr = _adam_rows(R, C)

    def body(l_ref, w_ref, m_ref, v_ref, *outs):
        g = l_ref[0].astype(F32)
        for k in range(1, N_DEV):
            g = g + l_ref[k].astype(F32)
        _adam_update(g, w_ref, m_ref, v_ref, *outs)

    blk = pl.BlockSpec((tr, C), lambda i: (i, 0))
    return _call(body, name=name, grid=(R // tr,), in_specs=[pl.BlockSpec((N_DEV, tr, C), lambda i: (0, i, 0)), blk, blk, blk],
                 out_specs=[blk] * 4, out_shape=[_sds((R, C), F32)] * 4, compiler_params=_cp(("parallel",)))(land, w, m, v)


def _adamw_halves(mine, theirs, w, m, v, name):
    r, C = mine.shape
    tr = _adam_rows(r, C)
    nt = r // tr

    def body(a_ref, b_ref, w_ref, m_ref, v_ref, *outs):
        g = jnp.where(pl.program_id(0) == lax.axis_index("c"), a_ref[...], b_ref[...])
        _adam_update(g, w_ref, m_ref, v_ref, *outs)

    half = pl.BlockSpec((tr, C), lambda hh, i: (i, 0))
    blk = pl.BlockSpec((tr, C), lambda hh, i: (hh * nt + i, 0))
    return _call(body, name=name, grid=(2, nt), in_specs=[half, half, blk, blk, blk], out_specs=[blk] * 4,
                 out_shape=[_sds((2 * r, C), F32)] * 4, compiler_params=_cp(("parallel", "parallel")))(mine, theirs, w, m, v)


def _pad_rows(a, rows):
    return jnp.pad(a, ((0, rows - a.shape[0]), (0, 0)))


def _local_step(x, mem, tgt, g_pre, w_main, w_if, b_if, wq, wk, w_dw, b_dw, g_ln, b_ln, w_conv_out, g_ml_head, w_ml_out,
                g_mem, w_mem_kv, w_xa_out, w_out, g_post):
    S, D = x.shape
    row = lambda a: a.reshape(1, -1)
    bif = jnp.pad(b_if, (0, 128 - b_if.shape[0])).reshape(1, 128)
    wdw, wq8, wk8 = _pad_rows(w_dw, HALO), _pad_rows(wq, QK_HALO), _pad_rows(wk, QK_HALO)

    h = _rms_fwd(x, row(g_pre), "rms_pre")
    P = _mm(h, w_main, "nn", F32, "proj_in")
    gif = _mm(h, w_if, "nn", F32, "proj_if")
    ca, u1 = _conv_fwd(P, wdw, row(b_dw), row(g_ln), row(b_ln), D)
    q, k, cq, ck = _qk_fwd(P, wq8, wk8, D, 3)
    gl = _gates_fwd(gif, bif)
    g8 = gl[:, :8]
    gt = g8.T
    hm, cs, ns, ms = _mlstm_fwd(q, k, P, g8, gt, D, 5)
    cb = _ml_post_fwd(hm, P, row(g_ml_head), D, 6, 7)
    mn = _rms_fwd(mem, row(g_mem), "rms_mem")
    kv = _mm(mn, w_mem_kv, "nn", BF16, "proj_kv")
    cc = _attn_fwd(P, kv, D, 8, 9)
    yc = _mm(ca, w_conv_out, "nn", F32, "out_conv")
    ym = _mm(cb, w_ml_out, "nn", F32, "out_ml")
    yx = _mm(cc, w_xa_out, "nn", F32, "out_xa")
    merged = _merge_fwd(yc, ym, yx, P, D, 10)
    o = _mm(merged, w_out, "nn", F32, "out_proj")
    dy, d_o, gg_post, sq = _post(o, x, tgt, row(g_post))

    d_merged = _mm(d_o, w_out, "nt", F32, "d_merged")
    gw_out = _mm(merged, d_o, "tn", BF16, "gw_out")
    d_yc, d_ym, d_yx, d_gc, d_gm, d_gx = _merge_bwd(d_merged, yc, ym, yx, P, D, 10)
    d_ca = _mm(d_yc, w_conv_out, "nt", F32, "d_ca")
    gw_conv_out = _mm(ca, d_yc, "tn", BF16, "gw_conv_out")
    d_cb = _mm(d_ym, w_ml_out, "nt", F32, "d_cb")
    gw_ml_out = _mm(cb, d_ym, "tn", BF16, "gw_ml_out")
    d_cc = _mm(d_yx, w_xa_out, "nt", F32, "d_cc")
    gw_xa_out = _mm(cc, d_yx, "tn", BF16, "gw_xa_out")

    d_u1, d_zc, gg_ln, gb_ln = _conv_bwd_local(d_ca, u1, P, row(g_ln), row(b_ln), D)
    d_a, d_b, gw_dw, gb_dw = _conv_bwd_taps(d_u1, P, wdw, D)

    d_qx, d_zx, d_kv = _attn_bwd(d_cc, P, kv, D, 8, 9)
    d_kvb = d_kv.astype(BF16)
    gw_mem_kv = _mm(mn, d_kvb, "tn", BF16, "gw_mem_kv")
    d_mn = _mm(d_kvb, w_mem_kv, "nt", F32, "d_mn")
    _, gg_mem = _rms_bwd(mem, row(g_mem), [d_mn], None, "rms_mem_bwd")

    d_hm, d_om, d_zm, gg_ml = _ml_post_bwd(d_cb, hm, P, row(g_ml_head), D, 6, 7)
    dq, dk, d_v, dg = _mlstm_bwd(d_hm, q, k, P, g8, gt, cs, ns, ms, D, 5)
    d_qp, d_kp, gwq, gwk = _qk_bwd(dq, dk, cq, ck, P, wq8, wk8, D, 3)
    dgl = jnp.pad(jnp.concatenate([dg[:, 0, :].T, dg[:, 1, :].T], axis=1), ((0, 0), (0, 128 - 2 * N_HEADS)))
    d_gif, gb_if = _gates_bwd(dgl, gif, bif)

    dP = jnp.concatenate([d_a, d_b, d_zc, d_qp, d_kp, d_v, d_om, d_zm, d_qx, d_zx, d_gc, d_gm, d_gx], axis=1)
    d_h1 = _mm(dP, w_main, "nt", F32, "d_h_main")
    d_h2 = _mm(d_gif, w_if, "nt", F32, "d_h_if")
    gw_main = _mm(h, dP, "tn", BF16, "gw_main")
    gw_if = _mm(h, d_gif, "tn", BF16, "gw_if")
    grad_x, gg_pre = _rms_bwd(x, row(g_pre), [d_h1, d_h2], dy, "rms_pre_bwd")

    grads = dict(g_pre=gg_pre[0], w_main=gw_main, w_if=gw_if, b_if=gb_if[0, :2 * N_HEADS], wq=gwq[:QK_W], wk=gwk[:QK_W],
                 w_dw=gw_dw[:CONV_W], b_dw=gb_dw[0], g_ln=gg_ln[0], b_ln=gb_ln[0], w_conv_out=gw_conv_out,
                 g_ml_head=gg_ml[0], w_ml_out=gw_ml_out, g_mem=gg_mem[0], w_mem_kv=gw_mem_kv, w_xa_out=gw_xa_out,
                 w_out=gw_out, g_post=gg_post[0])
    return jnp.sum(sq), grad_x, grads


SMALL = ("g_pre", "b_if", "b_dw", "g_ln", "b_ln", "g_ml_head", "g_mem", "g_post")


def kernel(x, mem, g_pre, w_in, b_if, w_qk_conv, w_dw, b_dw, g_ln, b_ln, w_conv_out, g_ml_head, w_ml_out, g_mem, w_mem_kv, w_xa_out, w_out, g_post, loss_target, m_g_pre, m_w_in, m_b_if, m_w_qk_conv, m_w_dw, m_b_dw, m_g_ln, m_b_ln, m_w_conv_out, m_g_ml_head, m_w_ml_out, m_g_mem, m_w_mem_kv, m_w_xa_out, m_w_out, m_g_post, v_g_pre, v_w_in, v_b_if, v_w_qk_conv, v_w_dw, v_b_dw, v_g_ln, v_b_ln, v_w_conv_out, v_g_ml_head, v_w_ml_out, v_g_mem, v_w_mem_kv, v_w_xa_out, v_w_out, v_g_post):
    W = dict(g_pre=g_pre, w_in=w_in, b_if=b_if, w_qk_conv=w_qk_conv, w_dw=w_dw, b_dw=b_dw, g_ln=g_ln, b_ln=b_ln,
             w_conv_out=w_conv_out, g_ml_head=g_ml_head, w_ml_out=w_ml_out, g_mem=g_mem, w_mem_kv=w_mem_kv,
             w_xa_out=w_xa_out, w_out=w_out, g_post=g_post)
    Mo = dict(g_pre=m_g_pre, w_in=m_w_in, b_if=m_b_if, w_qk_conv=m_w_qk_conv, w_dw=m_w_dw, b_dw=m_b_dw, g_ln=m_g_ln,
              b_ln=m_b_ln, w_conv_out=m_w_conv_out, g_ml_head=m_g_ml_head, w_ml_out=m_w_ml_out, g_mem=m_g_mem,
              w_mem_kv=m_w_mem_kv, w_xa_out=m_w_xa_out, w_out=m_w_out, g_post=m_g_post)
    Vo = dict(g_pre=v_g_pre, w_in=v_w_in, b_if=v_b_if, w_qk_conv=v_w_qk_conv, w_dw=v_w_dw, b_dw=v_b_dw, g_ln=v_g_ln,
              b_ln=v_b_ln, w_conv_out=v_w_conv_out, g_ml_head=v_g_ml_head, w_ml_out=v_w_ml_out, g_mem=v_g_mem,
              w_mem_kv=v_w_mem_kv, w_xa_out=v_w_xa_out, w_out=v_w_out, g_post=v_g_post)
    D = x.shape[-1]
    n_in = 4 * w_in.shape[1]
    off_if = 8 * D

    big = ("w_in", "w_conv_out", "w_ml_out", "w_mem_kv", "w_xa_out", "w_out")
    halves = lambda a: a.reshape(2, a.shape[0] // 2, a.shape[1])
    own = [halves(W[n].astype(BF16)) for n in big] + [w_qk_conv, w_dw]
    my_chip = 2 * lax.axis_index("x") + lax.axis_index("y")
    got = [lax.dynamic_update_index_in_dim(g, o, my_chip, 0) for g, o in zip(_gather_chips(own[:len(big)], own[len(big):]), own)]
    got = [g.reshape(4, -1, g.shape[-1]) for g in got]
    cols = lambda a: jnp.transpose(a, (1, 0, 2)).reshape(a.shape[1], -1)
    rows = lambda a: a.reshape(-1, a.shape[2])
    blk = w_in.shape[1]
    k_if, a_if = divmod(off_if, blk)
    assert a_if + 2 * N_HEADS <= blk
    pieces = []
    for k in range(4):
        pieces += [got[0][k][:, :a_if], got[0][k][:, a_if + 2 * N_HEADS:]] if k == k_if else [got[0][k]]
    w_main = jnp.concatenate(pieces, axis=1)
    w_if = jnp.pad(got[0][k_if][:, a_if:a_if + 2 * N_HEADS], ((0, 0), (0, 128 - 2 * N_HEADS)))
    wqk_f = cols(got[6])

    sq, grad_x, G = _local_step(
        x[0], mem[0], loss_target[0], g_pre, w_main, w_if, b_if, wqk_f[:, :D], wqk_f[:, D:], cols(got[7]), b_dw, g_ln, b_ln,
        rows(got[1]), g_ml_head, rows(got[2]), g_mem, cols(got[3]), rows(got[4]), rows(got[5]), g_post)
    loss = lax.psum(0.5 * sq / D, ("x", "y", "c"))

    colblk = lambda a: jnp.transpose(a.reshape(a.shape[0], 4, -1), (1, 0, 2))
    colblk_h = lambda a: jnp.transpose(a.reshape(2, a.shape[0] // 2, 4, -1), (0, 2, 1, 3))
    rowblk_h = lambda a: jnp.transpose(a.reshape(4, 2, -1, a.shape[1]), (1, 0, 2, 3))
    gm, gi = G["w_main"], G["w_if"][:, :2 * N_HEADS]

    def w_in_block(k):
        if k < k_if:
            return gm[:, k * blk:(k + 1) * blk]
        if k > k_if:
            return gm[:, k * blk - 2 * N_HEADS:(k + 1) * blk - 2 * N_HEADS]
        return jnp.concatenate([gm[:, k * blk:off_if], gi, gm[:, off_if:(k + 1) * blk - 2 * N_HEADS]], axis=1)

    assert gm.shape[1] + 2 * N_HEADS == n_in
    gw_in_h = jnp.stack([w_in_block(k).reshape(2, -1, blk) for k in range(4)], axis=1)
    gh = dict(w_in=gw_in_h, w_conv_out=rowblk_h(G["w_conv_out"]), w_ml_out=rowblk_h(G["w_ml_out"]),
              w_mem_kv=colblk_h(G["w_mem_kv"]), w_xa_out=rowblk_h(G["w_xa_out"]), w_out=rowblk_h(G["w_out"]))
    c = lax.axis_index("c")
    gh = {n: a.astype(BF16) for n, a in gh.items()}
    mine = [lax.dynamic_index_in_dim(gh[n], c, 0, keepdims=False) for n in big]
    theirs = _to_sibling([lax.dynamic_index_in_dim(gh[n], 1 - c, 0, keepdims=False) for n in big], "pair_swap")
    chip_sums = [_add_pair(a, b, "chip_sum_" + n) for n, a, b in zip(big, mine, theirs)]
    landed = [lax.dynamic_update_index_in_dim(l, lax.dynamic_index_in_dim(s, my_chip, 0), my_chip, 0)
              for l, s in zip(_chip_scatter(chip_sums), chip_sums)]
    halves_mine = [_sum_chips(a, "sum_chips_" + n) for n, a in zip(big, landed)]
    halves_theirs = _to_sibling(halves_mine, "share_halves")
    out = {}
    for n, a, b in zip(big, halves_mine, halves_theirs):
        out[n] = _adamw_halves(a, b, W[n], Mo[n], Vo[n], "adamw_" + n)

    pad = lambda a: jnp.pad(a, (0, D - a.shape[0]))
    small_g = jnp.stack([pad(G[n]) for n in SMALL])
    lands = _scatter_grads([colblk(jnp.concatenate([G["wq"], G["wk"]], axis=1)), colblk(G["w_dw"])], small_g)
    for n, land in zip(("w_qk_conv", "w_dw"), lands[:-1]):
        out[n] = _adamw(land, W[n], Mo[n], Vo[n], "adamw_" + n)
    res = _adamw(lands[-1], jnp.stack([pad(W[n]) for n in SMALL]), jnp.stack([pad(Mo[n]) for n in SMALL]),
                 jnp.stack([pad(Vo[n]) for n in SMALL]), "adamw_small")
    for i, n in enumerate(SMALL):
        out[n] = tuple(r[i, :W[n].shape[0]] for r in res)
    order = ("g_pre", "w_in", "b_if", "w_qk_conv", "w_dw", "b_dw", "g_ln", "b_ln", "w_conv_out", "g_ml_head", "w_ml_out",
             "g_mem", "w_mem_kv", "w_xa_out", "w_out", "g_post")
    return (loss, grad_x[None], *[out[n][0] for n in order], *[out[n][1] for n in order], *[out[n][2] for n in order],
            *[out[n][3] for n in order])
```

```python
import jax
import jax.numpy as jnp
from jax import lax
from jax.experimental import pallas as pl
from jax.experimental.pallas import tpu as pltpu

F32, BF16 = jnp.float32, jnp.bfloat16
EPS = 1e-6
N_HEADS = 4
CONV_W = 31
QK_W = 4
HALO = 32
QK_HALO = 8
ROW_TILE = 256
ML_CHUNK = 256
VMEM_LIMIT = 56 * 1024 * 1024
NEG = -1e30
ADAM_LR, ADAM_B1, ADAM_B2, ADAM_EPS, ADAM_WD, ADAM_STEP = 0.001, 0.9, 0.999, 1e-08, 0.01, 10
MESH = pl.DeviceIdType.MESH
N_DEV = 8

NN = (((1,), (0,)), ((), ()))
NT = (((1,), (1,)), ((), ()))
TN = (((0,), (0,)), ((), ()))


def _call(body, **kw):
    return pl.pallas_call(body, **kw)


def _cp(sem):
    return pltpu.CompilerParams(dimension_semantics=sem, vmem_limit_bytes=VMEM_LIMIT)


def _sds(shape, dt):
    return jax.ShapeDtypeStruct(shape, dt)


def _row(tm, d, col=0):
    return pl.BlockSpec((tm, d), lambda i: (i, col))


def _par(r, d):
    return pl.BlockSpec((r, d), lambda i: (0, 0))


def _prev(tm, hb, d, col=0):
    return pl.BlockSpec((hb, d), lambda i: (jnp.maximum(i * (tm // hb) - 1, 0), col))


def _next(tm, hb, d, nblk, col=0):
    return pl.BlockSpec((hb, d), lambda i: (jnp.minimum((i + 1) * (tm // hb), nblk - 1), col))


def _dot(a, b, dn):
    return lax.dot_general(a.astype(BF16), b.astype(BF16), dn, preferred_element_type=F32)


def _sg(x):
    return jax.nn.sigmoid(x)


def _dsilu(z, s):
    return s * (1.0 + z * (1.0 - s))


def _mean(x):
    return jnp.mean(x, axis=-1, keepdims=True)


def _pick(n, pref):
    if n <= pref:
        return n
    t = pref
    while n % t:
        t -= 128
    return t


def _mm(a, b, mode, out_dtype, name, tm=1024, tn=1024, tk=2048):
    if mode == "nn":
        (M, K), N = a.shape, b.shape[1]
    elif mode == "nt":
        (M, K), N = a.shape, b.shape[0]
    else:
        (K, M), N = a.shape, b.shape[1]
    tm, tn, tk = _pick(M, tm), _pick(N, tn), _pick(K, tk)
    nk = K // tk
    if mode == "nn":
        sa, sb, dn = pl.BlockSpec((tm, tk), lambda i, j, k: (i, k)), pl.BlockSpec((tk, tn), lambda i, j, k: (k, j)), NN
    elif mode == "nt":
        sa, sb, dn = pl.BlockSpec((tm, tk), lambda i, j, k: (i, k)), pl.BlockSpec((tn, tk), lambda i, j, k: (j, k)), NT
    else:
        sa, sb, dn = pl.BlockSpec((tk, tm), lambda i, j, k: (k, i)), pl.BlockSpec((tk, tn), lambda i, j, k: (k, j)), TN

    def body(a_ref, b_ref, o_ref, *acc):
        p = _dot(a_ref[...], b_ref[...], dn)
        if nk == 1:
            o_ref[...] = p.astype(out_dtype)
        else:
            acc_ref, k = acc[0], pl.program_id(2)

            @pl.when(k == 0)
            def _():
                acc_ref[...] = p

            @pl.when(k > 0)
            def _():
                acc_ref[...] += p

            @pl.when(k == nk - 1)
            def _():
                o_ref[...] = acc_ref[...].astype(out_dtype)

    return _call(body, name=name, grid=(M // tm, N // tn, nk), in_specs=[sa, sb],
                 out_specs=pl.BlockSpec((tm, tn), lambda i, j, k: (i, j)), out_shape=_sds((M, N), out_dtype),
                 scratch_shapes=[pltpu.VMEM((tm, tn), F32)] if nk > 1 else [],
                 compiler_params=_cp(("parallel", "parallel", "arbitrary")))(a, b)


def _rms_fwd(x, g, name):
    S, D = x.shape
    tm = min(ROW_TILE, S)

    def body(x_ref, g_ref, h_ref):
        xv = x_ref[...]
        r = lax.rsqrt(_mean(xv * xv) + EPS)
        h_ref[...] = (xv * r * g_ref[...]).astype(BF16)

    return _call(body, name=name, grid=(S // tm,), in_specs=[_row(tm, D), _par(1, D)], out_specs=_row(tm, D),
                 out_shape=_sds((S, D), BF16), compiler_params=_cp(("parallel",)))(x, g)


def _rms_bwd(x, g, ds, resid, name):
    S, D = x.shape
    tm = min(ROW_TILE, S)
    nd = len(ds)

    def body(*refs):
        x_ref, g_ref = refs[:2]
        d_refs = refs[2:2 + nd]
        r_ref = refs[2 + nd] if resid is not None else None
        dx_ref, dg_ref = refs[-2:]
        xv = x_ref[...]
        d = d_refs[0][...]
        for dr in d_refs[1:]:
            d = d + dr[...]
        r = lax.rsqrt(_mean(xv * xv) + EPS)
        dxh = d * g_ref[...]
        dx = r * dxh - xv * (r * r * r) * _mean(dxh * xv)
        if r_ref is not None:
            dx = dx + r_ref[...]
        dx_ref[...] = dx

        @pl.when(pl.program_id(0) == 0)
        def _():
            dg_ref[...] = jnp.zeros_like(dg_ref)

        dg_ref[...] += jnp.sum(d * xv * r, axis=0, keepdims=True)

    ins = [x, g, *ds] + ([resid] if resid is not None else [])
    specs = [_row(tm, D), _par(1, D)] + [_row(tm, D)] * (len(ins) - 2)
    return _call(body, name=name, grid=(S // tm,), in_specs=specs, out_specs=[_row(tm, D), _par(1, D)],
                 out_shape=[_sds((S, D), F32), _sds((1, D), F32)], compiler_params=_cp(("arbitrary",)))(*ins)


def _shifted_views(buf, sh, shifts, tm):
    for r in range(8):
        group = [(i, s) for i, s in enumerate(shifts) if s % 8 == r]
        if not group:
            continue
        if r:
            top = max(s for _, s in group) - r + tm
            sh[0:top, :] = buf[r:r + top, :]
        src = sh if r else buf
        for i, s in group:
            yield i, src[s - r:s - r + tm, :]


def _conv_fwd(P, w_dw, b_dw, g_ln, b_ln, D):
    S = P.shape[0]
    tm = min(ROW_TILE, S)

    def body(a_ref, b_ref, ap_ref, bp_ref, z_ref, w_ref, bd_ref, g_ref, bl_ref, ca_ref, u1_ref, buf, sh):
        first = pl.program_id(0) == 0
        hp = ap_ref[...] * _sg(bp_ref[...])
        buf[0:HALO, :] = jnp.where(first, 0.0, hp)
        buf[HALO:HALO + tm, :] = a_ref[...] * _sg(b_ref[...])
        acc = jnp.zeros((tm, D), F32)
        for j, view in _shifted_views(buf, sh, [HALO - CONV_W + 1 + j for j in range(CONV_W)], tm):
            acc = acc + w_ref[j:j + 1, :] * view
        u1 = acc + bd_ref[...]
        u1_ref[...] = u1
        xc = u1 - _mean(u1)
        r = lax.rsqrt(_mean(xc * xc) + EPS)
        u2 = xc * r * g_ref[...] + bl_ref[...]
        z = z_ref[...]
        ca_ref[...] = (u2 * _sg(u2) * (z * _sg(z))).astype(BF16)

    return _call(body, name="conv_fwd", grid=(S // tm,),
                 in_specs=[_row(tm, D, 0), _row(tm, D, 1), _prev(tm, HALO, D, 0), _prev(tm, HALO, D, 1), _row(tm, D, 2),
                           _par(HALO, D), _par(1, D), _par(1, D), _par(1, D)],
                 out_specs=[_row(tm, D), _row(tm, D)], out_shape=[_sds((S, D), BF16), _sds((S, D), F32)],
                 scratch_shapes=[pltpu.VMEM((tm + HALO, D), F32)] * 2, compiler_params=_cp(("parallel",)))(
        P, P, P, P, P, w_dw, b_dw, g_ln, b_ln)


def _conv_bwd_local(d_ca, u1, P, g_ln, b_ln, D):
    S = P.shape[0]
    tm = min(ROW_TILE, S)

    def body(dca_ref, u1_ref, z_ref, g_ref, bl_ref, du1_ref, dz_ref, dg_ref, db_ref):
        u1, z, dca, g = u1_ref[...], z_ref[...], dca_ref[...], g_ref[...]
        xc = u1 - _mean(u1)
        r = lax.rsqrt(_mean(xc * xc) + EPS)
        xh = xc * r
        u2 = xh * g + bl_ref[...]
        s2, sz = _sg(u2), _sg(z)
        d_u3 = dca * (z * sz)
        dz_ref[...] = (dca * (u2 * s2) * _dsilu(z, sz)).astype(BF16)
        d_u2 = d_u3 * _dsilu(u2, s2)
        dxh = d_u2 * g
        du1_ref[...] = r * (dxh - _mean(dxh) - xh * _mean(dxh * xh))

        @pl.when(pl.program_id(0) == 0)
        def _():
            dg_ref[...] = jnp.zeros_like(dg_ref)
            db_ref[...] = jnp.zeros_like(db_ref)

        dg_ref[...] += jnp.sum(d_u2 * xh, axis=0, keepdims=True)
        db_ref[...] += jnp.sum(d_u2, axis=0, keepdims=True)

    return _call(body, name="conv_bwd_local", grid=(S // tm,),
                 in_specs=[_row(tm, D), _row(tm, D), _row(tm, D, 2), _par(1, D), _par(1, D)],
                 out_specs=[_row(tm, D), _row(tm, D), _par(1, D), _par(1, D)],
                 out_shape=[_sds((S, D), F32), _sds((S, D), BF16), _sds((1, D), F32), _sds((1, D), F32)],
                 compiler_params=_cp(("arbitrary",)))(d_ca, u1, P, g_ln, b_ln)


def _conv_bwd_taps(d_u1, P, w_dw, D):
    S = P.shape[0]
    tm = min(ROW_TILE, S)
    nblk = S // HALO

    def body(d_ref, dn_ref, a_ref, b_ref, ap_ref, bp_ref, w_ref, da_ref, db_ref, gw_ref, gb_ref, dbuf, ubuf, sh):
        i = pl.program_id(0)
        a, sb = a_ref[...], _sg(b_ref[...])
        d = d_ref[...]
        dbuf[0:tm, :] = d
        dbuf[tm:tm + HALO, :] = jnp.where(i == pl.num_programs(0) - 1, 0.0, dn_ref[...])
        ubuf[0:HALO, :] = jnp.where(i == 0, 0.0, ap_ref[...] * _sg(bp_ref[...]))
        ubuf[HALO:HALO + tm, :] = a * sb

        @pl.when(i == 0)
        def _():
            gw_ref[...] = jnp.zeros_like(gw_ref)
            gb_ref[...] = jnp.zeros_like(gb_ref)

        acc = jnp.zeros((tm, D), F32)
        for j, view in _shifted_views(dbuf, sh, [CONV_W - 1 - j for j in range(CONV_W)], tm):
            acc = acc + w_ref[j:j + 1, :] * view
        for j, view in _shifted_views(ubuf, sh, [HALO - CONV_W + 1 + j for j in range(CONV_W)], tm):
            gw_ref[j:j + 1, :] += jnp.sum(d * view, axis=0, keepdims=True)
        gb_ref[...] += jnp.sum(d, axis=0, keepdims=True)
        da_ref[...] = (acc * sb).astype(BF16)
        db_ref[...] = (acc * a * sb * (1.0 - sb)).astype(BF16)

    return _call(body, name="conv_bwd_taps", grid=(S // tm,),
                 in_specs=[_row(tm, D), _next(tm, HALO, D, nblk), _row(tm, D, 0), _row(tm, D, 1),
                           _prev(tm, HALO, D, 0), _prev(tm, HALO, D, 1), _par(HALO, D)],
                 out_specs=[_row(tm, D), _row(tm, D), _par(HALO, D), _par(1, D)],
                 out_shape=[_sds((S, D), BF16), _sds((S, D), BF16), _sds((HALO, D), F32), _sds((1, D), F32)],
                 scratch_shapes=[pltpu.VMEM((tm + HALO, D), F32)] * 3,
                 compiler_params=_cp(("arbitrary",)))(d_u1, d_u1, P, P, P, P, w_dw)


def _qk_fwd(P, wq, wk, D, col):
    S = P.shape[0]
    tm = min(ROW_TILE, S)
    scale = float(D // N_HEADS) ** -0.5

    def body(q_ref, k_ref, qp_ref, kp_ref, wq_ref, wk_ref, qo_ref, ko_ref, cq_ref, ck_ref, buf):
        first = pl.program_id(0) == 0
        for x_ref, p_ref, w_ref, o_ref, c_ref, sc in ((q_ref, qp_ref, wq_ref, qo_ref, cq_ref, 1.0),
                                                      (k_ref, kp_ref, wk_ref, ko_ref, ck_ref, scale)):
            buf[0:QK_HALO, :] = jnp.where(first, 0.0, p_ref[...])
            buf[QK_HALO:QK_HALO + tm, :] = x_ref[...]
            acc = jnp.zeros((tm, D), F32)
            for j in range(QK_W):
                sh = QK_HALO - QK_W + 1 + j
                acc = acc + w_ref[j:j + 1, :] * buf[sh:sh + tm, :]
            c_ref[...] = acc
            o_ref[...] = acc * _sg(acc) * sc

    return _call(body, name="qk_fwd", grid=(S // tm,),
                 in_specs=[_row(tm, D, col), _row(tm, D, col + 1), _prev(tm, QK_HALO, D, col), _prev(tm, QK_HALO, D, col + 1),
                           _par(QK_HALO, D), _par(QK_HALO, D)],
                 out_specs=[_row(tm, D)] * 4, out_shape=[_sds((S, D), F32)] * 4,
                 scratch_shapes=[pltpu.VMEM((tm + QK_HALO, D), F32)], compiler_params=_cp(("parallel",)))(P, P, P, P, wq, wk)


def _qk_bwd(dq, dk, cq, ck, P, wq, wk, D, col):
    S = P.shape[0]
    tm = min(ROW_TILE, S)
    nblk = S // QK_HALO
    scale = float(D // N_HEADS) ** -0.5

    def body(dq_ref, dqn_ref, cq_ref, cqn_ref, xq_ref, xqp_ref, wq_ref,
             dk_ref, dkn_ref, ck_ref, ckn_ref, xk_ref, xkp_ref, wk_ref,
             oq_ref, ok_ref, gq_ref, gk_ref, dbuf, xbuf):
        i = pl.program_id(0)
        last = i == pl.num_programs(0) - 1

        @pl.when(i == 0)
        def _():
            gq_ref[...] = jnp.zeros_like(gq_ref)
            gk_ref[...] = jnp.zeros_like(gk_ref)

        for d_ref, dn_ref, c_ref, cn_ref, x_ref, xp_ref, w_ref, o_ref, g_ref, sc in (
                (dq_ref, dqn_ref, cq_ref, cqn_ref, xq_ref, xqp_ref, wq_ref, oq_ref, gq_ref, 1.0),
                (dk_ref, dkn_ref, ck_ref, ckn_ref, xk_ref, xkp_ref, wk_ref, ok_ref, gk_ref, scale)):
            c, cn = c_ref[...], cn_ref[...]
            dc = d_ref[...] * sc * _dsilu(c, _sg(c))
            dcn = dn_ref[...] * sc * _dsilu(cn, _sg(cn))
            dbuf[0:tm, :] = dc
            dbuf[tm:tm + QK_HALO, :] = jnp.where(last, 0.0, dcn)
            xbuf[0:QK_HALO, :] = jnp.where(i == 0, 0.0, xp_ref[...])
            xbuf[QK_HALO:QK_HALO + tm, :] = x_ref[...]
            acc = jnp.zeros((tm, D), F32)
            for j in range(QK_W):
                acc = acc + w_ref[j:j + 1, :] * dbuf[QK_W - 1 - j:QK_W - 1 - j + tm, :]
                sh = QK_HALO - QK_W + 1 + j
                g_ref[j:j + 1, :] += jnp.sum(dc * xbuf[sh:sh + tm, :], axis=0, keepdims=True)
            o_ref[...] = acc.astype(BF16)

    one = [_row(tm, D), _next(tm, QK_HALO, D, nblk)]
    specs = (one + one + [_row(tm, D, col), _prev(tm, QK_HALO, D, col), _par(QK_HALO, D)]
             + one + one + [_row(tm, D, col + 1), _prev(tm, QK_HALO, D, col + 1), _par(QK_HALO, D)])
    return _call(body, name="qk_bwd", grid=(S // tm,), in_specs=specs,
                 out_specs=[_row(tm, D), _row(tm, D), _par(QK_HALO, D), _par(QK_HALO, D)],
                 out_shape=[_sds((S, D), BF16), _sds((S, D), BF16), _sds((QK_HALO, D), F32), _sds((QK_HALO, D), F32)],
                 scratch_shapes=[pltpu.VMEM((tm + QK_HALO, D), F32), pltpu.VMEM((tm + QK_HALO, D), F32)],
                 compiler_params=_cp(("arbitrary",)))(dq, dq, cq, cq, P, P, wq, dk, dk, ck, ck, P, P, wk)


def _gates_fwd(gif, b_if):
    S = gif.shape[0]
    tm = min(ROW_TILE, S)

    def body(g_ref, b_ref, o_ref):
        z = g_ref[...] + b_ref[...]
        lane = lax.broadcasted_iota(jnp.int32, z.shape, 1)
        ls = jnp.minimum(z, 0.0) - jnp.log(1.0 + jnp.exp(-jnp.abs(z)))
        o_ref[...] = jnp.where(lane < N_HEADS, z, jnp.where(lane < 2 * N_HEADS, ls, 0.0))

    return _call(body, name="gates_fwd", grid=(S // tm,), in_specs=[_row(tm, 128), _par(1, 128)], out_specs=_row(tm, 128),
                 out_shape=_sds((S, 128), F32), compiler_params=_cp(("parallel",)))(gif, b_if)


def _gates_bwd(dgl, gif, b_if):
    S = gif.shape[0]
    tm = min(ROW_TILE, S)

    def body(d_ref, g_ref, b_ref, o_ref, gb_ref):
        z = g_ref[...] + b_ref[...]
        lane = lax.broadcasted_iota(jnp.int32, z.shape, 1)
        d = d_ref[...]
        dz = jnp.where(lane < N_HEADS, d, jnp.where(lane < 2 * N_HEADS, d * _sg(-z), 0.0))
        o_ref[...] = dz.astype(BF16)

        @pl.when(pl.program_id(0) == 0)
        def _():
            gb_ref[...] = jnp.zeros_like(gb_ref)

        gb_ref[...] += jnp.sum(dz, axis=0, keepdims=True)

    return _call(body, name="gates_bwd", grid=(S // tm,), in_specs=[_row(tm, 128), _row(tm, 128), _par(1, 128)],
                 out_specs=[_row(tm, 128), _par(1, 128)], out_shape=[_sds((S, 128), BF16), _sds((1, 128), F32)],
                 compiler_params=_cp(("arbitrary",)))(dgl, gif, b_if)


def _chunk_gates(g8_ref, gt_ref, h, L):
    g8, gt = g8_ref[...], gt_ref[...]
    lane = lax.broadcasted_iota(jnp.int32, g8.shape, 1)
    sub = lax.broadcasted_iota(jnp.int32, gt.shape, 0)
    li_col = jnp.sum(jnp.where(lane == h, g8, 0.0), axis=1, keepdims=True)
    lf_col = jnp.sum(jnp.where(lane == h + N_HEADS, g8, 0.0), axis=1, keepdims=True)
    li_row = jnp.sum(jnp.where(sub == h, gt, 0.0), axis=0, keepdims=True)
    lf_row = jnp.sum(jnp.where(sub == h + N_HEADS, gt, 0.0), axis=0, keepdims=True)
    t = lax.broadcasted_iota(jnp.int32, (L, L), 0)
    s = lax.broadcasted_iota(jnp.int32, (L, L), 1)
    causal = s <= t
    b_col = jnp.sum(jnp.where(causal, lf_row, 0.0), axis=1, keepdims=True)
    b_row = jnp.sum(jnp.where(t <= s, lf_col, 0.0), axis=0, keepdims=True)
    return li_col, li_row, b_col, b_row, causal


def _chunk_fwd(q, k, v, C, n, m, li_row, b_col, b_row, causal):
    d = jnp.where(causal, b_col - b_row + li_row, NEG)
    inter = b_col + m
    m_row = jnp.maximum(inter, jnp.max(d, axis=1, keepdims=True))
    wi = jnp.exp(d - m_row)
    wn = jnp.exp(inter - m_row)
    s = _dot(q, k, NT) * wi
    num = _dot(s, v, NN) + wn * _dot(q, C, NN)
    den = jnp.sum(s, axis=1, keepdims=True) + wn * jnp.sum(q * n, axis=1, keepdims=True)
    e = jnp.exp(-m_row)
    inv = 1.0 / jnp.maximum(jnp.abs(den), e)
    return wi, wn, s, num * inv, den, e, inv


def _chunk_state(li_col, li_row, b_col, b_row, m, L):
    blast = b_col[L - 1:L, :]
    g_col = blast - b_col + li_col
    g_row = blast - b_row + li_row
    m_new = jnp.maximum(blast + m, jnp.max(g_row, axis=1, keepdims=True))
    decay = jnp.exp(blast + m - m_new)
    wk_col = jnp.exp(g_col - m_new)
    return m_new, decay, wk_col


def _mlstm_fwd(q, k, P, g8, gt, D, vcol):
    S = q.shape[0]
    H, dh = N_HEADS, D // N_HEADS
    L = min(ML_CHUNK, S)
    nc = S // L

    def body(q_ref, k_ref, v_ref, g8_ref, gt_ref, h_ref, cs_ref, ns_ref, ms_ref, C, n, m):
        h, j = pl.program_id(0), pl.program_id(1)

        @pl.when(j == 0)
        def _():
            C[...] = jnp.zeros_like(C)
            n[...] = jnp.zeros_like(n)
            m[...] = jnp.zeros_like(m)

        qv, kv, vv = q_ref[...], k_ref[...], v_ref[...]
        Cv, nv, mv = C[...], n[...], m[:, 0:1]
        cs_ref[...] = Cv.astype(BF16)
        ns_ref[...] = nv
        ms_ref[...] = m[...]
        li_col, li_row, b_col, b_row, causal = _chunk_gates(g8_ref, gt_ref, h, L)
        _, _, _, hv, _, _, _ = _chunk_fwd(qv, kv, vv, Cv, nv, mv, li_row, b_col, b_row, causal)
        h_ref[...] = hv
        m_new, decay, wk_col = _chunk_state(li_col, li_row, b_col, b_row, mv, L)
        kw = kv * wk_col
        C[...] = decay * Cv + _dot(kw, vv, TN)
        n[...] = decay * nv + jnp.sum(kw, axis=0, keepdims=True)
        m[...] = jnp.broadcast_to(m_new, m.shape)

    blk = lambda c0: pl.BlockSpec((L, dh), lambda h, j: (j, c0 + h))
    return _call(body, name="mlstm_fwd", grid=(H, nc),
                 in_specs=[blk(0), blk(0), blk(vcol * H), pl.BlockSpec((L, 8), lambda h, j: (j, 0)),
                           pl.BlockSpec((8, L), lambda h, j: (0, j))],
                 out_specs=[blk(0), pl.BlockSpec((None, None, dh, dh), lambda h, j: (h, j, 0, 0)),
                            pl.BlockSpec((None, None, 1, dh), lambda h, j: (h, j, 0, 0)),
                            pl.BlockSpec((None, None, 1, 128), lambda h, j: (h, j, 0, 0))],
                 out_shape=[_sds((S, D), F32), _sds((H, nc, dh, dh), BF16), _sds((H, nc, 1, dh), F32),
                            _sds((H, nc, 1, 128), F32)],
                 scratch_shapes=[pltpu.VMEM((dh, dh), F32), pltpu.VMEM((1, dh), F32), pltpu.VMEM((1, 128), F32)],
                 compiler_params=_cp(("arbitrary", "arbitrary")))(q, k, P, g8, gt)


def _mlstm_bwd(dhm, q, k, P, g8, gt, cs, ns, ms, D, vcol):
    S = q.shape[0]
    H, dh = N_HEADS, D // N_HEADS
    L = min(ML_CHUNK, S)
    nc = S // L

    def body(dh_ref, q_ref, k_ref, v_ref, g8_ref, gt_ref, cs_ref, ns_ref, ms_ref,
             dq_ref, dk_ref, dv_ref, dg_ref, dC, dnv):
        h, j = pl.program_id(0), pl.program_id(1)

        @pl.when(j == 0)
        def _():
            dC[...] = jnp.zeros_like(dC)
            dnv[...] = jnp.zeros_like(dnv)

        qv, kv, vv, dhv = q_ref[...], k_ref[...], v_ref[...], dh_ref[...]
        Cv, nv, mv = cs_ref[...], ns_ref[...], ms_ref[:, 0:1]
        li_col, li_row, b_col, b_row, causal = _chunk_gates(g8_ref, gt_ref, h, L)
        wi, wn, s, hv, den, e, inv = _chunk_fwd(qv, kv, vv, Cv, nv, mv, li_row, b_col, b_row, causal)
        dn = dhv * inv
        dd = -jnp.sum(dhv * hv, axis=1, keepdims=True) * inv * jnp.where(jnp.abs(den) > e, jnp.sign(den), 0.0)
        dS = _dot(dn, vv, NT) + dd
        dqk = dS * wi
        dq_i = wn * (_dot(dn, Cv, NT) + dd * nv)
        dq = _dot(dqk, kv, NN) + dq_i
        dv = _dot(s, dn, TN)
        m_new, decay, wk_col = _chunk_state(li_col, li_row, b_col, b_row, mv, L)
        dCv, dnvv = dC[...], dnv[...]
        dk_i = wk_col * (_dot(vv, dCv, NT) + dnvv)
        dk = _dot(dqk, qv, TN) + dk_i
        dv = dv + _dot(kv * wk_col, dCv, NN)
        qw = qv * wn
        dC[...] = decay * dCv + _dot(qw, dn, TN)
        dnv[...] = decay * dnvv + jnp.sum(qw * dd, axis=0, keepdims=True)
        dq_ref[...] = dq
        dk_ref[...] = dk
        dv_ref[...] = dv.astype(BF16)
        G = dS * s
        t = lax.broadcasted_iota(jnp.int32, (L, L), 0)
        r = lax.broadcasted_iota(jnp.int32, (L, L), 1)
        eye = t == r
        col_row = jnp.sum(G, axis=0, keepdims=True)
        col_g = jnp.sum(jnp.where(eye, col_row, 0.0), axis=1, keepdims=True)
        row_i = jnp.sum(qv * dq_i, axis=1, keepdims=True)
        col_i = jnp.sum(kv * dk_i, axis=1, keepdims=True)
        dF = jnp.sum(G, axis=1, keepdims=True) - col_g + row_i
        across = decay * (jnp.sum(jnp.sum(dCv * Cv.astype(F32), axis=0, keepdims=True), axis=1, keepdims=True)
                          + jnp.sum(dnvv * nv, axis=1, keepdims=True))
        dlf = jnp.sum(jnp.where(t >= r, dF, 0.0) + jnp.where(t < r, col_i, 0.0), axis=0, keepdims=True) + across
        dli = col_row + jnp.sum(jnp.where(eye, col_i, 0.0), axis=0, keepdims=True)
        sub = lax.broadcasted_iota(jnp.int32, (8, L), 0)
        dg_ref[...] = jnp.where(sub == 0, dli, jnp.where(sub == 1, dlf, 0.0))

    blk = lambda c0: pl.BlockSpec((L, dh), lambda h, j: (nc - 1 - j, c0 + h))
    st = lambda r, c: pl.BlockSpec((None, None, r, c), lambda h, j: (h, nc - 1 - j, 0, 0))
    return _call(body, name="mlstm_bwd", grid=(H, nc),
                 in_specs=[blk(0), blk(0), blk(0), blk(vcol * H), pl.BlockSpec((L, 8), lambda h, j: (nc - 1 - j, 0)),
                           pl.BlockSpec((8, L), lambda h, j: (0, nc - 1 - j)), st(dh, dh), st(1, dh), st(1, 128)],
                 out_specs=[blk(0), blk(0), blk(0), pl.BlockSpec((None, 8, L), lambda h, j: (h, 0, nc - 1 - j))],
                 out_shape=[_sds((S, D), F32), _sds((S, D), F32), _sds((S, D), BF16), _sds((H, 8, S), F32)],
                 scratch_shapes=[pltpu.VMEM((dh, dh), F32), pltpu.VMEM((1, dh), F32)],
                 compiler_params=_cp(("arbitrary", "arbitrary")))(dhm, q, k, P, g8, gt, cs, ns, ms)


def _ml_post_fwd(hm, P, g, D, ocol, zcol):
    S = hm.shape[0]
    tm = min(ROW_TILE, S)
    dh = D // N_HEADS

    def body(h_ref, o_ref, z_ref, g_ref, cb_ref):
        for hd in range(N_HEADS):
            sl = slice(hd * dh, (hd + 1) * dh)
            x = _sg(o_ref[:, sl]) * h_ref[:, sl]
            r = lax.rsqrt(_mean(x * x) + EPS)
            z = z_ref[:, sl]
            cb_ref[:, sl] = (x * r * g_ref[:, sl] * (z * _sg(z))).astype(BF16)

    return _call(body, name="ml_post_fwd", grid=(S // tm,),
                 in_specs=[_row(tm, D), _row(tm, D, ocol), _row(tm, D, zcol), _par(1, D)], out_specs=_row(tm, D),
                 out_shape=_sds((S, D), BF16), compiler_params=_cp(("parallel",)))(hm, P, P, g)


def _ml_post_bwd(d_cb, hm, P, g, D, ocol, zcol):
    S = hm.shape[0]
    tm = min(ROW_TILE, S)
    dh = D // N_HEADS

    def body(d_ref, h_ref, o_ref, z_ref, g_ref, dh_ref, do_ref, dz_ref, dg_ref):
        @pl.when(pl.program_id(0) == 0)
        def _():
            dg_ref[...] = jnp.zeros_like(dg_ref)

        for hd in range(N_HEADS):
            sl = slice(hd * dh, (hd + 1) * dh)
            hv, z, gv, d = h_ref[:, sl], z_ref[:, sl], g_ref[:, sl], d_ref[:, sl]
            so, sz = _sg(o_ref[:, sl]), _sg(z)
            x = so * hv
            r = lax.rsqrt(_mean(x * x) + EPS)
            xh = x * r
            d3 = d * (z * sz)
            dz_ref[:, sl] = (d * xh * gv * _dsilu(z, sz)).astype(BF16)
            dg_ref[:, sl] += jnp.sum(d3 * xh, axis=0, keepdims=True)
            dxh = d3 * gv
            d2 = r * (dxh - xh * _mean(dxh * xh))
            do_ref[:, sl] = (d2 * hv * so * (1.0 - so)).astype(BF16)
            dh_ref[:, sl] = d2 * so

    return _call(body, name="ml_post_bwd", grid=(S // tm,),
                 in_specs=[_row(tm, D), _row(tm, D), _row(tm, D, ocol), _row(tm, D, zcol), _par(1, D)],
                 out_specs=[_row(tm, D), _row(tm, D), _row(tm, D), _par(1, D)],
                 out_shape=[_sds((S, D), F32), _sds((S, D), BF16), _sds((S, D), BF16), _sds((1, D), F32)],
                 compiler_params=_cp(("arbitrary",)))(d_cb, hm, P, P, g)


def _attn_scores(q, kh, scale):
    sc = _dot(q, kh, NT) * scale
    ex = jnp.exp(sc - jnp.max(sc, axis=1, keepdims=True))
    return ex / jnp.sum(ex, axis=1, keepdims=True)


def _attn_fwd(P, kv, D, qcol, zcol):
    S, M = P.shape[0], kv.shape[0]
    tm = min(ROW_TILE, S)
    dh = D // N_HEADS
    scale = float(dh) ** -0.5

    def body(q_ref, z_ref, kv_ref, cc_ref):
        for hd in range(N_HEADS):
            sl = slice(hd * dh, (hd + 1) * dh)
            p = _attn_scores(q_ref[:, sl], kv_ref[:, sl], scale)
            o = _dot(p, kv_ref[:, D + hd * dh:D + (hd + 1) * dh], NN)
            z = z_ref[:, sl]
            cc_ref[:, sl] = (o * (z * _sg(z))).astype(BF16)

    return _call(body, name="attn_fwd", grid=(S // tm,),
                 in_specs=[_row(tm, D, qcol), _row(tm, D, zcol), _par(M, 2 * D)], out_specs=_row(tm, D),
                 out_shape=_sds((S, D), BF16), compiler_params=_cp(("parallel",)))(P, P, kv)


def _attn_bwd(d_cc, P, kv, D, qcol, zcol):
    S, M = P.shape[0], kv.shape[0]
    tm = min(ROW_TILE, S)
    dh = D // N_HEADS
    scale = float(dh) ** -0.5

    def body(d_ref, q_ref, z_ref, kv_ref, dq_ref, dz_ref, dkv_ref):
        @pl.when(pl.program_id(0) == 0)
        def _():
            dkv_ref[...] = jnp.zeros_like(dkv_ref)

        for hd in range(N_HEADS):
            sl = slice(hd * dh, (hd + 1) * dh)
            vl = slice(D + hd * dh, D + (hd + 1) * dh)
            q, kh, vh = q_ref[:, sl], kv_ref[:, sl], kv_ref[:, vl]
            p = _attn_scores(q, kh, scale)
            o = _dot(p, vh, NN)
            z, d = z_ref[:, sl], d_ref[:, sl]
            sz = _sg(z)
            do = d * (z * sz)
            dz_ref[:, sl] = (d * o * _dsilu(z, sz)).astype(BF16)
            dp = _dot(do, vh, NT)
            ds = p * (dp - jnp.sum(p * dp, axis=1, keepdims=True)) * scale
            dq_ref[:, sl] = _dot(ds, kh, NN).astype(BF16)
            dkv_ref[:, sl] += _dot(ds, q, TN)
            dkv_ref[:, vl] += _dot(p, do, TN)

    return _call(body, name="attn_bwd", grid=(S // tm,),
                 in_specs=[_row(tm, D), _row(tm, D, qcol), _row(tm, D, zcol), _par(M, 2 * D)],
                 out_specs=[_row(tm, D), _row(tm, D), _par(M, 2 * D)],
                 out_shape=[_sds((S, D), BF16), _sds((S, D), BF16), _sds((M, 2 * D), F32)],
                 compiler_params=_cp(("arbitrary",)))(d_cc, P, P, kv)


def _merge_fwd(yc, ym, yx, P, D, gcol):
    S = yc.shape[0]
    tm = min(ROW_TILE, S)

    def body(c_ref, m_ref, x_ref, gc_ref, gm_ref, gx_ref, o_ref):
        o_ref[...] = (_sg(gc_ref[...]) * c_ref[...] + _sg(gm_ref[...]) * m_ref[...]
                      + _sg(gx_ref[...]) * x_ref[...]).astype(BF16)

    return _call(body, name="merge_fwd", grid=(S // tm,),
                 in_specs=[_row(tm, D)] * 3 + [_row(tm, D, gcol), _row(tm, D, gcol + 1), _row(tm, D, gcol + 2)],
                 out_specs=_row(tm, D), out_shape=_sds((S, D), BF16), compiler_params=_cp(("parallel",)))(yc, ym, yx, P, P, P)


def _merge_bwd(dm, yc, ym, yx, P, D, gcol):
    S = yc.shape[0]
    tm = min(ROW_TILE, S)

    def body(d_ref, c_ref, m_ref, x_ref, gc_ref, gm_ref, gx_ref, dc_ref, dmm_ref, dx_ref, dgc_ref, dgm_ref, dgx_ref):
        d = d_ref[...]
        for y_ref, g_ref, dy_ref, dg_ref in ((c_ref, gc_ref, dc_ref, dgc_ref), (m_ref, gm_ref, dmm_ref, dgm_ref),
                                             (x_ref, gx_ref, dx_ref, dgx_ref)):
            s = _sg(g_ref[...])
            dy_ref[...] = (d * s).astype(BF16)
            dg_ref[...] = (d * y_ref[...] * s * (1.0 - s)).astype(BF16)

    return _call(body, name="merge_bwd", grid=(S // tm,),
                 in_specs=[_row(tm, D)] * 4 + [_row(tm, D, gcol), _row(tm, D, gcol + 1), _row(tm, D, gcol + 2)],
                 out_specs=[_row(tm, D)] * 6, out_shape=[_sds((S, D), BF16)] * 6,
                 compiler_params=_cp(("parallel",)))(dm, yc, ym, yx, P, P, P)


def _post(o, x, tgt, g):
    S, D = x.shape
    tm = min(ROW_TILE, S)

    def body(o_ref, x_ref, t_ref, g_ref, dy_ref, do_ref, dg_ref, l_ref):
        ov, gv = o_ref[...], g_ref[...]
        r = lax.rsqrt(_mean(ov * ov) + EPS)
        e = x_ref[...] + ov * r * gv - t_ref[...]
        dy = e / D
        dy_ref[...] = dy
        dxh = dy * gv
        do_ref[...] = (r * dxh - ov * (r * r * r) * _mean(dxh * ov)).astype(BF16)

        @pl.when(pl.program_id(0) == 0)
        def _():
            dg_ref[...] = jnp.zeros_like(dg_ref)
            l_ref[...] = jnp.zeros_like(l_ref)

        dg_ref[...] += jnp.sum(dy * ov * r, axis=0, keepdims=True)
        l_ref[...] += jnp.sum(e * e, axis=0, keepdims=True)

    return _call(body, name="post", grid=(S // tm,), in_specs=[_row(tm, D)] * 3 + [_par(1, D)],
                 out_specs=[_row(tm, D), _row(tm, D), _par(1, D), _par(1, D)],
                 out_shape=[_sds((S, D), F32), _sds((S, D), BF16), _sds((1, D), F32), _sds((1, D), F32)],
                 compiler_params=_cp(("arbitrary",)))(o, x, tgt, g)


def _chip_peers():
    x, y, c = lax.axis_index("x"), lax.axis_index("y"), lax.axis_index("c")
    return x, y, c, [(1 - x, y), (x, 1 - y), (1 - x, 1 - y)]


def _gather_chips(halved, whole):
    TH, TW = len(halved), len(whole)
    T = TH + TW

    def body(*refs):
        ins, outs = refs[:T], refs[T:2 * T]
        send, recv, fsend, frecv = refs[2 * T:]
        x, y, c, chips = _chip_peers()
        mine = 2 * x + y

        def over_ici(t, j, chip, slot):
            src = ins[t].at[c] if t < TH else ins[t]
            dst = outs[t].at[slot, c] if t < TH else outs[t].at[slot]
            return pltpu.make_async_remote_copy(src_ref=src, dst_ref=dst, send_sem=send.at[t, j], recv_sem=recv.at[t, j],
                                                device_id=(*chip, c), device_id_type=MESH)

        def to_sibling(t, j, slot, half):
            place = outs[t].at[slot, half]
            return pltpu.make_async_remote_copy(src_ref=place, dst_ref=place, send_sem=fsend.at[t, j], recv_sem=frecv.at[t, j],
                                                device_id=(x, y, 1 - c), device_id_type=MESH)

        sends = []
        for t in range(T):
            for j, chip in enumerate(chips):
                cp = over_ici(t, j, chip, mine)
                cp.start()
                sends.append(cp)
        for t in range(T):
            for j, (px, py) in enumerate(chips):
                over_ici(t, j, (px, py), 2 * px + py).wait_recv()
                if t < TH:
                    cp = to_sibling(t, j, 2 * px + py, c)
                    cp.start()
                    sends.append(cp)
        for t in range(TH):
            for j, (px, py) in enumerate(chips):
                to_sibling(t, j, 2 * px + py, 1 - c).wait_recv()
        for cp in sends:
            cp.wait_send()

    anyspec = pl.BlockSpec(memory_space=pl.ANY)
    dma = pltpu.SemaphoreType.DMA
    return _call(body, name="gather_weights", in_specs=[anyspec] * T, out_specs=[anyspec] * T,
                 out_shape=[_sds((4, *s.shape), s.dtype) for s in (*halved, *whole)],
                 scratch_shapes=[dma((T, 3)), dma((T, 3)), dma((TH, 3)), dma((TH, 3))],
                 compiler_params=pltpu.CompilerParams(has_side_effects=True))(*halved, *whole)


def _to_sibling(tensors, name):
    T = len(tensors)

    def body(*refs):
        ins, outs = refs[:T], refs[T:2 * T]
        send, recv = refs[2 * T:]
        x, y, c = lax.axis_index("x"), lax.axis_index("y"), lax.axis_index("c")
        cps = [pltpu.make_async_remote_copy(src_ref=ins[t], dst_ref=outs[t], send_sem=send.at[t], recv_sem=recv.at[t],
                                            device_id=(x, y, 1 - c), device_id_type=MESH) for t in range(T)]
        for cp in cps:
            cp.start()
        for cp in cps:
            cp.wait_recv()
        for cp in cps:
            cp.wait_send()

    anyspec = pl.BlockSpec(memory_space=pl.ANY)
    return _call(body, name=name, in_specs=[anyspec] * T, out_specs=[anyspec] * T,
                 out_shape=[_sds(a.shape, a.dtype) for a in tensors],
                 scratch_shapes=[pltpu.SemaphoreType.DMA((T,)), pltpu.SemaphoreType.DMA((T,))],
                 compiler_params=pltpu.CompilerParams(has_side_effects=True))(*tensors)


def _chip_scatter(tensors):
    T = len(tensors)

    def body(*refs):
        ins, outs = refs[:T], refs[T:2 * T]
        send, recv = refs[2 * T:]
        x, y, c, chips = _chip_peers()
        mine = 2 * x + y

        def over_ici(t, j, chip, src_slot, dst_slot):
            return pltpu.make_async_remote_copy(src_ref=ins[t].at[src_slot], dst_ref=outs[t].at[dst_slot], send_sem=send.at[t, j],
                                                recv_sem=recv.at[t, j], device_id=(*chip, c), device_id_type=MESH)

        sends = []
        for t in range(T):
            for j, (px, py) in enumerate(chips):
                cp = over_ici(t, j, (px, py), 2 * px + py, mine)
                cp.start()
                sends.append(cp)
        for t in range(T):
            for j, (px, py) in enumerate(chips):
                over_ici(t, j, (px, py), mine, 2 * px + py).wait_recv()
        for cp in sends:
            cp.wait_send()

    anyspec = pl.BlockSpec(memory_space=pl.ANY)
    dma = pltpu.SemaphoreType.DMA
    return _call(body, name="chip_scatter", in_specs=[anyspec] * T, out_specs=[anyspec] * T,
                 out_shape=[_sds(a.shape, a.dtype) for a in tensors], scratch_shapes=[dma((T, 3)), dma((T, 3))],
                 compiler_params=pltpu.CompilerParams(has_side_effects=True))(*tensors)


def _row_tile(R, C, slots):
    tr = max(16, ((8 << 20) // (C * 4 * slots)) // 16 * 16)
    if tr >= R:
        return R
    while R % tr:
        tr -= 16
    return tr


def _add_pair(a, b, name):
    lead, R, C = a.shape
    tr = _row_tile(R, C, 3)

    def body(a_ref, b_ref, o_ref):
        o_ref[...] = (a_ref[...].astype(F32) + b_ref[...].astype(F32)).astype(BF16)

    blk = pl.BlockSpec((None, tr, C), lambda k, i: (k, i, 0))
    return _call(body, name=name, grid=(lead, R // tr), in_specs=[blk, blk], out_specs=blk, out_shape=_sds(a.shape, BF16),
                 compiler_params=_cp(("parallel", "parallel")))(a, b)


def _sum_chips(land, name):
    _, R, C = land.shape
    tr = _row_tile(R, C, 5)

    def body(l_ref, o_ref):
        o_ref[...] = ((l_ref[0].astype(F32) + l_ref[1].astype(F32)) + l_ref[2].astype(F32)) + l_ref[3].astype(F32)

    return _call(body, name=name, grid=(R // tr,), in_specs=[pl.BlockSpec((4, tr, C), lambda i: (0, i, 0))],
                 out_specs=pl.BlockSpec((tr, C), lambda i: (i, 0)), out_shape=_sds((R, C), F32),
                 compiler_params=_cp(("parallel",)))(land)


def _scatter_grads(blocks, small):
    T = len(blocks)

    def body(*refs):
        ins, sm = refs[:T], refs[T]
        outs, smo = refs[T + 1:2 * T + 1], refs[2 * T + 1]
        send, recv, loc = refs[2 * T + 2:]
        x, y, c = lax.axis_index("x"), lax.axis_index("y"), lax.axis_index("c")
        me = 4 * x + 2 * y + c
        peers = [(x ^ ((k >> 2) & 1), y ^ ((k >> 1) & 1), c ^ (k & 1)) for k in range(1, N_DEV)]
        copies = []
        for t in range(T):
            lc = pltpu.make_async_copy(ins[t].at[2 * x + y], outs[t].at[me], loc.at[t])
            lc.start()
            copies.append(lc)
        lc = pltpu.make_async_copy(sm, smo.at[me], loc.at[T])
        lc.start()
        copies.append(lc)

        def remote(t, j, peer, sender):
            px, py, pc = peer
            src = sm if t == T else ins[t].at[2 * px + py]
            dst = (smo if t == T else outs[t]).at[sender]
            return pltpu.make_async_remote_copy(src_ref=src, dst_ref=dst, send_sem=send.at[t, j], recv_sem=recv.at[t, j],
                                                device_id=peer, device_id_type=MESH)

        sends = []
        for t in range(T + 1):
            for j, peer in enumerate(peers):
                cp = remote(t, j, peer, me)
                cp.start()
                sends.append(cp)
        for t in range(T + 1):
            for j, (px, py, pc) in enumerate(peers):
                remote(t, j, (x, y, c), 4 * px + 2 * py + pc).wait_recv()
        for cp in sends:
            cp.wait_send()
        for lc in copies:
            lc.wait()

    anyspec = pl.BlockSpec(memory_space=pl.ANY)
    return _call(body, name="scatter_grads", in_specs=[anyspec] * (T + 1), out_specs=[anyspec] * (T + 1),
                 out_shape=[_sds((N_DEV, *b.shape[1:]), b.dtype) for b in blocks] + [_sds((N_DEV, *small.shape), small.dtype)],
                 scratch_shapes=[pltpu.SemaphoreType.DMA((T + 1, N_DEV - 1)), pltpu.SemaphoreType.DMA((T + 1, N_DEV - 1)),
                                 pltpu.SemaphoreType.DMA((T + 1,))],
                 compiler_params=pltpu.CompilerParams(has_side_effects=True))(*blocks, small)


def _adam_update(g, w_ref, m_ref, v_ref, g_ref, d_ref, mo_ref, vo_ref):
    c1, c2 = 1.0 - ADAM_B1 ** ADAM_STEP, 1.0 - ADAM_B2 ** ADAM_STEP
    mn = ADAM_B1 * m_ref[...] + (1.0 - ADAM_B1) * g
    vn = ADAM_B2 * v_ref[...] + (1.0 - ADAM_B2) * (g * g)
    g_ref[...] = g
    mo_ref[...] = mn
    vo_ref[...] = vn
    d_ref[...] = -ADAM_LR * ((mn / c1) / (jnp.sqrt(vn / c2) + ADAM_EPS) + ADAM_WD * w_ref[...])


def _adam_rows(R, C):
    tr = R if R * C * 4 <= (1 << 20) else max(16, ((1 << 20) // (C * 4)) // 16 * 16)
    while R % tr:
        tr -= 16
    return tr


def _adamw(land, w, m, v, name):
    R, C = w.shape
    tr = _adam_rows(R, C)

    def body(l_ref, w_ref, m_ref, v_ref, *outs):
        g = l_ref[0].astype(F32)
        for k in range(1, N_DEV):
            g = g + l_ref[k].astype(F32)
        _adam_update(g, w_ref, m_ref, v_ref, *outs)

    blk = pl.BlockSpec((tr, C), lambda i: (i, 0))
    return _call(body, name=name, grid=(R // tr,), in_specs=[pl.BlockSpec((N_DEV, tr, C), lambda i: (0, i, 0)), blk, blk, blk],
                 out_specs=[blk] * 4, out_shape=[_sds((R, C), F32)] * 4, compiler_params=_cp(("parallel",)))(land, w, m, v)


def _adamw_halves(mine, theirs, w, m, v, name):
    r, C = mine.shape
    tr = _adam_rows(r, C)
    nt = r // tr

    def body(a_ref, b_ref, w_ref, m_ref, v_ref, *outs):
        g = jnp.where(pl.program_id(0) == lax.axis_index("c"), a_ref[...], b_ref[...])
        _adam_update(g, w_ref, m_ref, v_ref, *outs)

    half = pl.BlockSpec((tr, C), lambda hh, i: (i, 0))
    blk = pl.BlockSpec((tr, C), lambda hh, i: (hh * nt + i, 0))
    return _call(body, name=name, grid=(2, nt), in_specs=[half, half, blk, blk, blk], out_specs=[blk] * 4,
                 out_shape=[_sds((2 * r, C), F32)] * 4, compiler_params=_cp(("parallel", "parallel")))(mine, theirs, w, m, v)


def _pad_rows(a, rows):
    return jnp.pad(a, ((0, rows - a.shape[0]), (0, 0)))


def _local_step(x, mem, tgt, g_pre, w_main, w_if, b_if, wq, wk, w_dw, b_dw, g_ln, b_ln, w_conv_out, g_ml_head, w_ml_out,
                g_mem, w_mem_kv, w_xa_out, w_out, g_post):
    S, D = x.shape
    row = lambda a: a.reshape(1, -1)
    bif = jnp.pad(b_if, (0, 128 - b_if.shape[0])).reshape(1, 128)
    wdw, wq8, wk8 = _pad_rows(w_dw, HALO), _pad_rows(wq, QK_HALO), _pad_rows(wk, QK_HALO)

    h = _rms_fwd(x, row(g_pre), "rms_pre")
    P = _mm(h, w_main, "nn", F32, "proj_in")
    gif = _mm(h, w_if, "nn", F32, "proj_if")
    ca, u1 = _conv_fwd(P, wdw, row(b_dw), row(g_ln), row(b_ln), D)
    q, k, cq, ck = _qk_fwd(P, wq8, wk8, D, 3)
    gl = _gates_fwd(gif, bif)
    g8 = gl[:, :8]
    gt = g8.T
    hm, cs, ns, ms = _mlstm_fwd(q, k, P, g8, gt, D, 5)
    cb = _ml_post_fwd(hm, P, row(g_ml_head), D, 6, 7)
    mn = _rms_fwd(mem, row(g_mem), "rms_mem")
    kv = _mm(mn, w_mem_kv, "nn", BF16, "proj_kv")
    cc = _attn_fwd(P, kv, D, 8, 9)
    yc = _mm(ca, w_conv_out, "nn", F32, "out_conv")
    ym = _mm(cb, w_ml_out, "nn", F32, "out_ml")
    yx = _mm(cc, w_xa_out, "nn", F32, "out_xa")
    merged = _merge_fwd(yc, ym, yx, P, D, 10)
    o = _mm(merged, w_out, "nn", F32, "out_proj")
    dy, d_o, gg_post, sq = _post(o, x, tgt, row(g_post))

    d_merged = _mm(d_o, w_out, "nt", F32, "d_merged")
    gw_out = _mm(merged, d_o, "tn", BF16, "gw_out")
    d_yc, d_ym, d_yx, d_gc, d_gm, d_gx = _merge_bwd(d_merged, yc, ym, yx, P, D, 10)
    d_ca = _mm(d_yc, w_conv_out, "nt", F32, "d_ca")
    gw_conv_out = _mm(ca, d_yc, "tn", BF16, "gw_conv_out")
    d_cb = _mm(d_ym, w_ml_out, "nt", F32, "d_cb")
    gw_ml_out = _mm(cb, d_ym, "tn", BF16, "gw_ml_out")
    d_cc = _mm(d_yx, w_xa_out, "nt", F32, "d_cc")
    gw_xa_out = _mm(cc, d_yx, "tn", BF16, "gw_xa_out")

    d_u1, d_zc, gg_ln, gb_ln = _conv_bwd_local(d_ca, u1, P, row(g_ln), row(b_ln), D)
    d_a, d_b, gw_dw, gb_dw = _conv_bwd_taps(d_u1, P, wdw, D)

    d_qx, d_zx, d_kv = _attn_bwd(d_cc, P, kv, D, 8, 9)
    d_kvb = d_kv.astype(BF16)
    gw_mem_kv = _mm(mn, d_kvb, "tn", BF16, "gw_mem_kv")
    d_mn = _mm(d_kvb, w_mem_kv, "nt", F32, "d_mn")
    _, gg_mem = _rms_bwd(mem, row(g_mem), [d_mn], None, "rms_mem_bwd")

    d_hm, d_om, d_zm, gg_ml = _ml_post_bwd(d_cb, hm, P, row(g_ml_head), D, 6, 7)
    dq, dk, d_v, dg = _mlstm_bwd(d_hm, q, k, P, g8, gt, cs, ns, ms, D, 5)
    d_qp, d_kp, gwq, gwk = _qk_bwd(dq, dk, cq, ck, P, wq8, wk8, D, 3)
    dgl = jnp.pad(jnp.concatenate([dg[:, 0, :].T, dg[:, 1, :].T], axis=1), ((0, 0), (0, 128 - 2 * N_HEADS)))
    d_gif, gb_if = _gates_bwd(dgl, gif, bif)

    dP = jnp.concatenate([d_a, d_b, d_zc, d_qp, d_kp, d_v, d_om, d_zm, d_qx, d_zx, d_gc, d_gm, d_gx], axis=1)
    d_h1 = _mm(dP, w_main, "nt", F32, "d_h_main")
    d_h2 = _mm(d_gif, w_if, "nt", F32, "d_h_if")
    gw_main = _mm(h, dP, "tn", BF16, "gw_main", tm=2048)
    gw_if = _mm(h, d_gif, "tn", BF16, "gw_if")
    grad_x, gg_pre = _rms_bwd(x, row(g_pre), [d_h1, d_h2], dy, "rms_pre_bwd")

    grads = dict(g_pre=gg_pre[0], w_main=gw_main, w_if=gw_if, b_if=gb_if[0, :2 * N_HEADS], wq=gwq[:QK_W], wk=gwk[:QK_W],
                 w_dw=gw_dw[:CONV_W], b_dw=gb_dw[0], g_ln=gg_ln[0], b_ln=gb_ln[0], w_conv_out=gw_conv_out,
                 g_ml_head=gg_ml[0], w_ml_out=gw_ml_out, g_mem=gg_mem[0], w_mem_kv=gw_mem_kv, w_xa_out=gw_xa_out,
                 w_out=gw_out, g_post=gg_post[0])
    return jnp.sum(sq), grad_x, grads


SMALL = ("g_pre", "b_if", "b_dw", "g_ln", "b_ln", "g_ml_head", "g_mem", "g_post")


def kernel(x, mem, g_pre, w_in, b_if, w_qk_conv, w_dw, b_dw, g_ln, b_ln, w_conv_out, g_ml_head, w_ml_out, g_mem, w_mem_kv, w_xa_out, w_out, g_post, loss_target, m_g_pre, m_w_in, m_b_if, m_w_qk_conv, m_w_dw, m_b_dw, m_g_ln, m_b_ln, m_w_conv_out, m_g_ml_head, m_w_ml_out, m_g_mem, m_w_mem_kv, m_w_xa_out, m_w_out, m_g_post, v_g_pre, v_w_in, v_b_if, v_w_qk_conv, v_w_dw, v_b_dw, v_g_ln, v_b_ln, v_w_conv_out, v_g_ml_head, v_w_ml_out, v_g_mem, v_w_mem_kv, v_w_xa_out, v_w_out, v_g_post):
    W = dict(g_pre=g_pre, w_in=w_in, b_if=b_if, w_qk_conv=w_qk_conv, w_dw=w_dw, b_dw=b_dw, g_ln=g_ln, b_ln=b_ln,
             w_conv_out=w_conv_out, g_ml_head=g_ml_head, w_ml_out=w_ml_out, g_mem=g_mem, w_mem_kv=w_mem_kv,
             w_xa_out=w_xa_out, w_out=w_out, g_post=g_post)
    Mo = dict(g_pre=m_g_pre, w_in=m_w_in, b_if=m_b_if, w_qk_conv=m_w_qk_conv, w_dw=m_w_dw, b_dw=m_b_dw, g_ln=m_g_ln,
              b_ln=m_b_ln, w_conv_out=m_w_conv_out, g_ml_head=m_g_ml_head, w_ml_out=m_w_ml_out, g_mem=m_g_mem,
              w_mem_kv=m_w_mem_kv, w_xa_out=m_w_xa_out, w_out=m_w_out, g_post=m_g_post)
    Vo = dict(g_pre=v_g_pre, w_in=v_w_in, b_if=v_b_if, w_qk_conv=v_w_qk_conv, w_dw=v_w_dw, b_dw=v_b_dw, g_ln=v_g_ln,
              b_ln=v_b_ln, w_conv_out=v_w_conv_out, g_ml_head=v_g_ml_head, w_ml_out=v_w_ml_out, g_mem=v_g_mem,
              w_mem_kv=v_w_mem_kv, w_xa_out=v_w_xa_out, w_out=v_w_out, g_post=v_g_post)
    D = x.shape[-1]
    n_in = 4 * w_in.shape[1]
    off_if = 8 * D

    big = ("w_in", "w_conv_out", "w_ml_out", "w_mem_kv", "w_xa_out", "w_out")
    halves = lambda a: a.reshape(2, a.shape[0] // 2, a.shape[1])
    own = [halves(W[n].astype(BF16)) for n in big] + [w_qk_conv, w_dw]
    my_chip = 2 * lax.axis_index("x") + lax.axis_index("y")
    got = [lax.dynamic_update_index_in_dim(g, o, my_chip, 0) for g, o in zip(_gather_chips(own[:len(big)], own[len(big):]), own)]
    got = [g.reshape(4, -1, g.shape[-1]) for g in got]
    cols = lambda a: jnp.transpose(a, (1, 0, 2)).reshape(a.shape[1], -1)
    rows = lambda a: a.reshape(-1, a.shape[2])
    blk = w_in.shape[1]
    k_if, a_if = divmod(off_if, blk)
    assert a_if + 2 * N_HEADS <= blk
    pieces = []
    for k in range(4):
        pieces += [got[0][k][:, :a_if], got[0][k][:, a_if + 2 * N_HEADS:]] if k == k_if else [got[0][k]]
    w_main = jnp.concatenate(pieces, axis=1)
    w_if = jnp.pad(got[0][k_if][:, a_if:a_if + 2 * N_HEADS], ((0, 0), (0, 128 - 2 * N_HEADS)))
    wqk_f = cols(got[6])

    sq, grad_x, G = _local_step(
        x[0], mem[0], loss_target[0], g_pre, w_main, w_if, b_if, wqk_f[:, :D], wqk_f[:, D:], cols(got[7]), b_dw, g_ln, b_ln,
        rows(got[1]), g_ml_head, rows(got[2]), g_mem, cols(got[3]), rows(got[4]), rows(got[5]), g_post)
    loss = lax.psum(0.5 * sq / D, ("x", "y", "c"))

    colblk = lambda a: jnp.transpose(a.reshape(a.shape[0], 4, -1), (1, 0, 2))
    colblk_h = lambda a: jnp.transpose(a.reshape(2, a.shape[0] // 2, 4, -1), (0, 2, 1, 3))
    rowblk_h = lambda a: jnp.transpose(a.reshape(4, 2, -1, a.shape[1]), (1, 0, 2, 3))
    gm, gi = G["w_main"], G["w_if"][:, :2 * N_HEADS]

    def w_in_block(k):
        if k < k_if:
            return gm[:, k * blk:(k + 1) * blk]
        if k > k_if:
            return gm[:, k * blk - 2 * N_HEADS:(k + 1) * blk - 2 * N_HEADS]
        return jnp.concatenate([gm[:, k * blk:off_if], gi, gm[:, off_if:(k + 1) * blk - 2 * N_HEADS]], axis=1)

    assert gm.shape[1] + 2 * N_HEADS == n_in
    gw_in_h = jnp.stack([w_in_block(k).reshape(2, -1, blk) for k in range(4)], axis=1)
    gh = dict(w_in=gw_in_h, w_conv_out=rowblk_h(G["w_conv_out"]), w_ml_out=rowblk_h(G["w_ml_out"]),
              w_mem_kv=colblk_h(G["w_mem_kv"]), w_xa_out=rowblk_h(G["w_xa_out"]), w_out=rowblk_h(G["w_out"]))
    c = lax.axis_index("c")
    gh = {n: a.astype(BF16) for n, a in gh.items()}
    mine = [lax.dynamic_index_in_dim(gh[n], c, 0, keepdims=False) for n in big]
    theirs = _to_sibling([lax.dynamic_index_in_dim(gh[n], 1 - c, 0, keepdims=False) for n in big], "pair_swap")
    chip_sums = [_add_pair(a, b, "chip_sum_" + n) for n, a, b in zip(big, mine, theirs)]
    landed = [lax.dynamic_update_index_in_dim(l, lax.dynamic_index_in_dim(s, my_chip, 0), my_chip, 0)
              for l, s in zip(_chip_scatter(chip_sums), chip_sums)]
    halves_mine = [_sum_chips(a, "sum_chips_" + n) for n, a in zip(big, landed)]
    halves_theirs = _to_sibling(halves_mine, "share_halves")
    out = {}
    for n, a, b in zip(big, halves_mine, halves_theirs):
        out[n] = _adamw_halves(a, b, W[n], Mo[n], Vo[n], "adamw_" + n)

    pad = lambda a: jnp.pad(a, (0, D - a.shape[0]))
    small_g = jnp.stack([pad(G[n]) for n in SMALL])
    lands = _scatter_grads([colblk(jnp.concatenate([G["wq"], G["wk"]], axis=1)), colblk(G["w_dw"])], small_g)
    for n, land in zip(("w_qk_conv", "w_dw"), lands[:-1]):
        out[n] = _adamw(land, W[n], Mo[n], Vo[n], "adamw_" + n)
    res = _adamw(lands[-1], jnp.stack([pad(W[n]) for n in SMALL]), jnp.stack([pad(Mo[n]) for n in SMALL]),
                 jnp.stack([pad(Vo[n]) for n in SMALL]), "adamw_small")
    for i, n in enumerate(SMALL):
        out[n] = tuple(r[i, :W[n].shape[0]] for r in res)
    order = ("g_pre", "w_in", "b_if", "w_qk_conv", "w_dw", "b_dw", "g_ln", "b_ln", "w_conv_out", "g_ml_head", "w_ml_out",
             "g_mem", "w_mem_kv", "w_xa_out", "w_out", "g_post")
    return (loss, grad_x[None], *[out[n][0] for n in order], *[out[n][1] for n in order], *[out[n][2] for n in order],
            *[out[n][3] for n in order])
```
